```python
import math
import jax, jax.numpy as jnp
from jax import lax
import numpy as np

D_MODEL = 1024
BATCH = 16
SEQ = 256
DEPTH = 2
DEC_BATCH = 8
DEC_SEQ = 2048
PAST_LEN = 256

GRID_W = 64
CHUNK = 128
N_CHUNK_GROUPS = 4
CHUNK_GROUP_W = 128
CHUNK_W = N_CHUNK_GROUPS * CHUNK_GROUP_W
N_HEADS = 4
QK_NOPE = 128
ROPE_DIM = 64
V_DIM = 128
Q_RANK = 384
KV_RANK = 256
MLA_W = N_HEADS * V_DIM
MIX_W = CHUNK_W + MLA_W
IN_COLS = 2 * CHUNK_W + Q_RANK + KV_RANK + ROPE_DIM
Q_BLOCK = 128
ROPE_BASE = 10000.0
D_FF = 2816
N_EXPERTS = 8
TOP_K = 2
D_FF_EXPERT = 1792
N_DENSE = (DEPTH + 1) // 2
N_MOE = DEPTH // 2
ALPHA = (2 * DEPTH) ** 0.25
BETA = (8 * DEPTH) ** -0.25
EPS = 1e-6

kernel_name = "hybrid_chunkmlp_mla_dit_step"


def _layernorm(x, g, b):
    xf = x.astype(jnp.float32)
    mu = jnp.mean(xf, axis=-1, keepdims=True)
    var = jnp.mean(jnp.square(xf - mu), axis=-1, keepdims=True)
    y = (xf - mu) * lax.rsqrt(var + EPS)
    return (y * g.astype(jnp.float32) + b.astype(jnp.float32)).astype(x.dtype)


def _rmsnorm(x, g):
    xf = x.astype(jnp.float32)
    y = xf * lax.rsqrt(jnp.mean(jnp.square(xf), axis=-1, keepdims=True) + EPS)
    return (y * g.astype(jnp.float32)).astype(x.dtype)


def _rope_tables(n_tokens, dtype):
    rows = n_tokens // GRID_W
    row = jnp.repeat(jnp.arange(rows, dtype=jnp.float32), GRID_W)
    col = jnp.tile(jnp.arange(GRID_W, dtype=jnp.float32), rows)
    half = ROPE_DIM // 2
    inv_freq = ROPE_BASE ** (-jnp.arange(0, half, 2, dtype=jnp.float32) / half)
    ang = jnp.concatenate([row[:, None] * inv_freq[None, :], col[:, None] * inv_freq[None, :]], axis=-1)
    return jnp.cos(ang).astype(dtype), jnp.sin(ang).astype(dtype)


def _apply_axial_rope(x, cos, sin):
    half = ROPE_DIM // 2
    q = half // 2

    def rot(z, c_, s_):
        z1, z2 = z[..., :q], z[..., q:]
        return jnp.concatenate([z1 * c_ - z2 * s_, z1 * s_ + z2 * c_], axis=-1)

    return jnp.concatenate([rot(x[..., :half], cos[..., :q], sin[..., :q]),
                            rot(x[..., half:], cos[..., q:], sin[..., q:])], axis=-1)


def _chunk_mlp(u, v, ln_g, w_s, b_s):
    u = jax.nn.gelu(u)
    v = jax.nn.gelu(v)
    bsz, n, _ = v.shape
    vg = v.reshape(bsz, n // CHUNK, CHUNK, N_CHUNK_GROUPS, CHUNK_GROUP_W).astype(jnp.float32)
    mu = jnp.mean(vg, axis=-1, keepdims=True)
    var = jnp.mean(jnp.square(vg - mu), axis=-1, keepdims=True)
    vg = ((vg - mu) * lax.rsqrt(var + EPS) * ln_g.reshape(N_CHUNK_GROUPS, CHUNK_GROUP_W).astype(jnp.float32)).astype(v.dtype)
    z = jnp.einsum('gpq,bcqgd->bcpgd', w_s, vg) + b_s.T[None, None, :, :, None]
    return u * z.reshape(bsz, n, CHUNK_W)


def _attend(qn, qr, kn, kr, v):
    bsz, n = qn.shape[:2]
    nb = n // Q_BLOCK
    scale = 1.0 / math.sqrt(QK_NOPE + ROPE_DIM)

    def block(args):
        qn_b, qr_b = args
        s = jnp.einsum('bqhd,bkhd->bhqk', qn_b, kn) + jnp.einsum('bqhr,bkr->bhqk', qr_b, kr)
        p = jax.nn.softmax(s.astype(jnp.float32) * scale, axis=-1).astype(v.dtype)
        return jnp.einsum('bhqk,bkhd->bqhd', p, v)

    qn_b = qn.reshape(bsz, nb, Q_BLOCK, N_HEADS, QK_NOPE).swapaxes(0, 1)
    qr_b = qr.reshape(bsz, nb, Q_BLOCK, N_HEADS, ROPE_DIM).swapaxes(0, 1)
    out = lax.map(block, (qn_b, qr_b))
    return out.swapaxes(0, 1).reshape(bsz, n, MLA_W)


def _mixer(h, w_in, q_g, kv_g, w_uq, w_ukv, ln_v_g, w_s, b_s, w_o, ctx_ckv, ctx_krope):
    bsz, n, _ = h.shape
    p = h @ w_in
    u, v, cq, ckv, kr = jnp.split(
        p, [CHUNK_W, 2 * CHUNK_W, 2 * CHUNK_W + Q_RANK, 2 * CHUNK_W + Q_RANK + KV_RANK], axis=-1)
    chunk_out = _chunk_mlp(u, v, ln_v_g, w_s, b_s)
    q = (_rmsnorm(cq, q_g) @ w_uq).reshape(bsz, n, N_HEADS, QK_NOPE + ROPE_DIM)
    qn, qr = q[..., :QK_NOPE], q[..., QK_NOPE:]
    ckv = _rmsnorm(ckv, kv_g)
    if ctx_ckv is None:
        ckv_all, kr_all = ckv, kr
    else:
        cos, sin = _rope_tables(n, h.dtype)
        qr = _apply_axial_rope(qr, cos[:, None, :], sin[:, None, :])
        kr_lat = _apply_axial_rope(kr, cos, sin)
        ckv_all = jnp.concatenate([ctx_ckv, ckv], axis=1)
        kr_all = jnp.concatenate([ctx_krope, kr_lat], axis=1)
    kv = (ckv_all @ w_ukv).reshape(bsz, ckv_all.shape[1], N_HEADS, QK_NOPE + V_DIM)
    kn, vv = kv[..., :QK_NOPE], kv[..., QK_NOPE:]
    mla_out = _attend(qn, qr, kn, kr_all, vv)
    out = jnp.concatenate([chunk_out, mla_out], axis=-1) @ w_o
    return out, ckv, kr


def _swiglu(x, wg, wu, wd):
    return (jax.nn.silu(x @ wg) * (x @ wu)) @ wd


def _moe(h, router, wg, wu, wd):
    bsz, n, d = h.shape
    xt = h.reshape(-1, d)
    logits = (xt @ router).astype(jnp.float32)
    top_v, top_i = lax.top_k(logits, TOP_K)
    gates = jax.nn.softmax(top_v, axis=-1)
    combine = jnp.sum(jax.nn.one_hot(top_i, N_EXPERTS, dtype=jnp.float32) * gates[..., None], axis=1).astype(h.dtype)
    y = jnp.zeros_like(xt)
    for e in range(N_EXPERTS):
        y = y + combine[:, e:e + 1] * _swiglu(xt, wg[e], wu[e], wd[e])
    return y.reshape(bsz, n, d)


def setup_inputs(seed: int = 0) -> dict:
    key = jax.random.key(seed)
    ks = jax.random.split(key, 32)
    f32 = jnp.float32

    def nrm(k, shape, scale=1.0):
        return jax.random.normal(k, shape, dtype=f32) * scale

    D = D_MODEL
    return {
        "x_prompt": nrm(ks[0], (BATCH, SEQ, D)),
        "x_sample": nrm(ks[1], (DEC_BATCH, DEC_SEQ, D)),
        "c": nrm(ks[2], (DEC_BATCH, D)),
        "cache_ckv": nrm(ks[3], (DEC_BATCH, DEPTH, PAST_LEN, KV_RANK)),
        "cache_krope": nrm(ks[4], (DEC_BATCH, DEPTH, PAST_LEN, ROPE_DIM)),
        "c_ctx": nrm(ks[5], (D,)),
        "w_mod": nrm(ks[6], (DEPTH, D, 6 * D), 0.5 * D ** -0.5),
        "b_mod": nrm(ks[7], (DEPTH, 6 * D), 0.02),
        "w_in": nrm(ks[8], (DEPTH, D, IN_COLS), D ** -0.5),
        "q_norm_g": 1.0 + nrm(ks[9], (DEPTH, Q_RANK), 0.02),
        "kv_norm_g": 1.0 + nrm(ks[10], (DEPTH, KV_RANK), 0.02),
        "w_uq": nrm(ks[11], (DEPTH, Q_RANK, N_HEADS * (QK_NOPE + ROPE_DIM)), Q_RANK ** -0.5),
        "w_ukv": nrm(ks[12], (DEPTH, KV_RANK, N_HEADS * (QK_NOPE + V_DIM)), KV_RANK ** -0.5),
        "chunk_ln_g": 1.0 + nrm(ks[13], (DEPTH, CHUNK_W), 0.02),
        "w_spatial": nrm(ks[14], (DEPTH, N_CHUNK_GROUPS, CHUNK, CHUNK), CHUNK ** -0.5),
        "b_spatial": nrm(ks[15], (DEPTH, N_CHUNK_GROUPS, CHUNK), 0.02),
        "w_out": nrm(ks[16], (DEPTH, MIX_W, D), BETA * MIX_W ** -0.5),
        "ln_mix_g": 1.0 + nrm(ks[17], (DEPTH, D), 0.02),
        "ln_mix_b": nrm(ks[18], (DEPTH, D), 0.02),
        "ln_ffn_g": 1.0 + nrm(ks[19], (DEPTH, D), 0.02),
        "ln_ffn_b": nrm(ks[20], (DEPTH, D), 0.02),
        "ffn_w_gate": nrm(ks[21], (N_DENSE, D, D_FF), D ** -0.5),
        "ffn_w_up": nrm(ks[22], (N_DENSE, D, D_FF), D ** -0.5),
        "ffn_w_down": nrm(ks[23], (N_DENSE, D_FF, D), BETA * D_FF ** -0.5),
        "router_w": nrm(ks[24], (N_MOE, D, N_EXPERTS), D ** -0.5),
        "moe_w_gate": nrm(ks[25], (N_MOE, N_EXPERTS, D, D_FF_EXPERT), D ** -0.5),
        "moe_w_up": nrm(ks[26], (N_MOE, N_EXPERTS, D, D_FF_EXPERT), D ** -0.5),
        "moe_w_down": nrm(ks[27], (N_MOE, N_EXPERTS, D_FF_EXPERT, D), BETA * D_FF_EXPERT ** -0.5),
    }


def reference(x_prompt, x_sample, c, cache_ckv, cache_krope, c_ctx, w_mod, b_mod, w_in,
              q_norm_g, kv_norm_g, w_uq, w_ukv, chunk_ln_g, w_spatial, b_spatial, w_out,
              ln_mix_g, ln_mix_b, ln_ffn_g, ln_ffn_b, ffn_w_gate, ffn_w_up, ffn_w_down,
              router_w, moe_w_gate, moe_w_up, moe_w_down):

    def trunk(x, cond, ctx_ckv, ctx_krope):
        cond_act = jax.nn.silu(cond)
        ckvs, krs = [], []
        for l in range(DEPTH):
            mod = cond_act @ w_mod[l] + b_mod[l]
            sh_a, sc_a, g_a, sh_f, sc_f, g_f = jnp.split(mod, 6, axis=-1)
            h = x * (1.0 + sc_a) + sh_a
            mix, ckv, kr = _mixer(
                h, w_in[l], q_norm_g[l], kv_norm_g[l], w_uq[l], w_ukv[l], chunk_ln_g[l],
                w_spatial[l], b_spatial[l], w_out[l],
                None if ctx_ckv is None else ctx_ckv[:, l],
                None if ctx_krope is None else ctx_krope[:, l])
            x = _layernorm(ALPHA * x + g_a * mix, ln_mix_g[l], ln_mix_b[l])
            h = x * (1.0 + sc_f) + sh_f
            if l % 2 == 0:
                f = _swiglu(h, ffn_w_gate[l // 2], ffn_w_up[l // 2], ffn_w_down[l // 2])
            else:
                f = _moe(h, router_w[l // 2], moe_w_gate[l // 2], moe_w_up[l // 2], moe_w_down[l // 2])
            x = _layernorm(ALPHA * x + g_f * f, ln_ffn_g[l], ln_ffn_b[l])
            ckvs.append(ckv)
            krs.append(kr)
        return x, ckvs, krs

    y_prompt, ckvs, krs = trunk(x_prompt, c_ctx[None, None, :], None, None)
    new_ckv = jnp.stack(ckvs, axis=1)
    new_krope = jnp.stack(krs, axis=1)
    y_sample, _, _ = trunk(x_sample, c[:, None, :], cache_ckv, cache_krope)
    return (y_prompt, y_sample, new_ckv, new_krope)
```

```python
import functools
import math

import jax
import jax.numpy as jnp
from jax import lax
from jax.experimental import pallas as pl
from jax.experimental.pallas import tpu as pltpu

F32 = jnp.float32
BF16 = jnp.bfloat16

D_MODEL = 1024
DEPTH = 2
GRID_W = 64
CHUNK = 128
N_GROUPS = 4
GROUP_W = 128
CHUNK_W = N_GROUPS * GROUP_W
N_HEADS = 4
QK_NOPE = 128
ROPE_DIM = 64
V_DIM = 128
Q_RANK = 384
KV_RANK = 256
MLA_W = N_HEADS * V_DIM
HEAD_PAD = 256
QK_W = N_HEADS * HEAD_PAD
ROPE_BASE = 10000.0
N_EXPERTS = 8
ALPHA = (2 * DEPTH) ** 0.25
EPS = 1e-6
ATTN_SCALE = 1.0 / math.sqrt(QK_NOPE + ROPE_DIM)
MOD_ROWS = 16
LANES = 128
VMEM_LIMIT = 56 * 1024 * 1024

SH_A, SC_A, G_A, SH_F, SC_F, G_F = range(6)


def _sigmoid(x):
    return 1.0 / (1.0 + jnp.exp(-x))


def _gelu_tanh(x):
    return 0.5 * x * (1.0 + jnp.tanh(math.sqrt(2.0 / math.pi) * (x + 0.044715 * (x * x * x))))


def _layernorm(y, g, b):
    mu = jnp.mean(y, axis=-1, keepdims=True)
    d = y - mu
    var = jnp.mean(d * d, axis=-1, keepdims=True)
    return d * lax.rsqrt(var + EPS) * g + b


def _rmsnorm(y, g):
    return y * lax.rsqrt(jnp.mean(y * y, axis=-1, keepdims=True) + EPS) * g


def _dot(a, b):
    return jnp.dot(a, b, preferred_element_type=F32)


def _dot_nt(a, b):
    return lax.dot_general(a, b, (((1,), (1,)), ((), ())), preferred_element_type=F32)


def _params(*sem):
    return pltpu.CompilerParams(dimension_semantics=sem, vmem_limit_bytes=VMEM_LIMIT)


def _const_spec(shape):
    nd = len(shape)
    return pl.BlockSpec(shape, lambda *_: (0,) * nd)


class _Stream:
    def __init__(self, batch, seq, mod_row0, per_row_mod, latent):
        self.batch = batch
        self.seq = seq
        self.tokens = batch * seq
        self.mod_row0 = mod_row0
        self.per_row_mod = per_row_mod
        self.latent = latent

    def mod_spec(self, layer, which, tm):
        tiles_per_row = self.seq // tm
        row0, per_row = self.mod_row0, self.per_row_mod

        def index(i):
            r = row0 + (i // tiles_per_row if per_row else 0)
            return (layer, r, which, 0, 0)

        return pl.BlockSpec((None, None, None, 1, D_MODEL), index)


def _mod_body(c_ref, w_ref, b_ref, o_ref):
    a = c_ref[...]
    a = (a * _sigmoid(a)).astype(BF16)
    o_ref[...] = _dot(a, w_ref[...].astype(BF16)) + b_ref[...]


def _modulation(cond_rows, w_mod, b_mod):
    depth, _, width = w_mod.shape
    tn = 1536
    return pl.pallas_call(
        _mod_body,
        grid=(depth, width // tn),
        in_specs=[
            pl.BlockSpec((MOD_ROWS, D_MODEL), lambda l, j: (0, 0)),
            pl.BlockSpec((None, D_MODEL, tn), lambda l, j: (l, 0, j)),
            pl.BlockSpec((None, 1, tn), lambda l, j: (l, 0, j)),
        ],
        out_specs=pl.BlockSpec((None, MOD_ROWS, tn), lambda l, j: (l, 0, j)),
        out_shape=jax.ShapeDtypeStruct((depth, MOD_ROWS, width), F32),
        compiler_params=_params("parallel", "parallel"),
        name="modulation",
    )(cond_rows, w_mod, b_mod.reshape(depth, 1, width))


def _mixer_in_body(latent, tm, *refs):
    it = iter(refs)
    x_ref, sc_ref, sh_ref, win_ref, qg_ref, kvg_ref, wqm_ref = (next(it) for _ in range(7))
    wqs_ref = next(it) if latent else None
    wkv_ref, lng_ref, ws_ref, bs_ref = (next(it) for _ in range(4))
    cos_ref, sin_ref = (next(it), next(it)) if latent else (None, None)
    co_ref, q_ref, kcat_ref, v_ref = (next(it) for _ in range(4))
    ckv_ref, kr_ref = (None, None) if latent else (next(it), next(it))

    h = (x_ref[...] * (1.0 + sc_ref[...]) + sh_ref[...]).astype(BF16)
    p = _dot(h, win_ref[...])

    for g in range(N_GROUPS):
        cols = slice(g * GROUP_W, (g + 1) * GROUP_W)
        vg = _gelu_tanh(p[:, CHUNK_W + g * GROUP_W:CHUNK_W + (g + 1) * GROUP_W])
        mu = jnp.mean(vg, axis=-1, keepdims=True)
        d = vg - mu
        var = jnp.mean(d * d, axis=-1, keepdims=True)
        vn = (d * lax.rsqrt(var + EPS) * lng_ref[:, cols]).astype(BF16)
        ug = _gelu_tanh(p[:, cols])
        for c in range(tm // CHUNK):
            rows = slice(c * CHUNK, (c + 1) * CHUNK)
            z = _dot(ws_ref[g], vn[rows]) + bs_ref[g]
            co_ref[rows, cols] = (ug[rows] * z).astype(BF16)

    o = 2 * CHUNK_W
    cqn = _rmsnorm(p[:, o:o + Q_RANK], qg_ref[...]).astype(BF16)
    qm = _dot(cqn, wqm_ref[...])
    if latent:
        qs = _dot(cqn, wqs_ref[...])
        cos = cos_ref[...]
        sin = sin_ref[...]
    for hd in range(N_HEADS):
        a = hd * HEAD_PAD
        q_ref[:, a:a + QK_NOPE] = (qm[:, a:a + QK_NOPE] * ATTN_SCALE).astype(BF16)
        qr = qm[:, a + QK_NOPE:a + HEAD_PAD]
        if latent:
            qr = qr * cos + qs[:, hd * LANES:(hd + 1) * LANES] * sin
        q_ref[:, a + QK_NOPE:a + HEAD_PAD] = (qr * ATTN_SCALE).astype(BF16)

    o += Q_RANK
    ckvn = _rmsnorm(p[:, o:o + KV_RANK], kvg_ref[...])
    o += KV_RANK
    kr = p[:, o:o + LANES]
    if not latent:
        ckv_ref[...] = ckvn
        kr_ref[...] = kr[:, :ROPE_DIM]
    else:
        kr = kr * cos + p[:, o + LANES:o + 2 * LANES] * sin
    kr = kr.astype(BF16)
    kv = _dot(ckvn.astype(BF16), wkv_ref[...])
    for hd in range(N_HEADS):
        a = hd * HEAD_PAD
        kcat_ref[:, a:a + QK_NOPE] = kv[:, hd * QK_NOPE:(hd + 1) * QK_NOPE].astype(BF16)
        kcat_ref[:, a + QK_NOPE:a + HEAD_PAD] = kr
    v_ref[...] = kv[:, N_HEADS * QK_NOPE:].astype(BF16)


def _mixer_in(stream, layer, x, mod, lw, rope, tm):
    latent = stream.latent
    t = stream.tokens
    row = lambda w: pl.BlockSpec((tm, w), lambda i: (i, 0))
    win = lw["w_in"]
    ins = [x, mod, mod, win, lw["q_g"], lw["kv_g"], lw["wq_main"]]
    specs = [row(D_MODEL), stream.mod_spec(layer, SC_A, tm), stream.mod_spec(layer, SH_A, tm),
             _const_spec(win.shape), _const_spec(lw["q_g"].shape), _const_spec(lw["kv_g"].shape),
             _const_spec(lw["wq_main"].shape)]
    if latent:
        ins.append(lw["wq_swap"])
        specs.append(_const_spec(lw["wq_swap"].shape))
    for name in ("wkv", "ln_v_g", "w_s", "b_s"):
        ins.append(lw[name])
        specs.append(_const_spec(lw[name].shape))
    if latent:
        tiles_per_seq = stream.seq // tm
        pos = pl.BlockSpec((tm, LANES), lambda i: (i % tiles_per_seq, 0))
        ins += [rope[0], rope[1]]
        specs += [pos, pos]
    out_shape = [jax.ShapeDtypeStruct((t, CHUNK_W), BF16), jax.ShapeDtypeStruct((t, QK_W), BF16),
                 jax.ShapeDtypeStruct((t, QK_W), BF16), jax.ShapeDtypeStruct((t, MLA_W), BF16)]
    out_specs = [row(CHUNK_W), row(QK_W), row(QK_W), row(MLA_W)]
    if not latent:
        out_shape += [jax.ShapeDtypeStruct((t, KV_RANK), F32), jax.ShapeDtypeStruct((t, ROPE_DIM), F32)]
        out_specs += [row(KV_RANK), row(ROPE_DIM)]
    return pl.pallas_call(
        functools.partial(_mixer_in_body, latent, tm),
        grid=(t // tm,),
        in_specs=specs,
        out_specs=out_specs,
        out_shape=out_shape,
        compiler_params=_params("parallel"),
        name="mixer_in",
    )(*ins)


def _attend_body(has_ctx, *refs):
    if has_ctx:
        q_ref, k_ref, v_ref, cckv_ref, ckr_ref, wkv_ref, o_ref, kctx_ref, vctx_ref = refs

        @pl.when(pl.program_id(1) == 0)
        def _():
            kv = _dot(cckv_ref[...].astype(BF16), wkv_ref[...])
            kr = ckr_ref[...].astype(BF16)
            for hd in range(N_HEADS):
                a = hd * HEAD_PAD
                kctx_ref[:, a:a + QK_NOPE] = kv[:, hd * QK_NOPE:(hd + 1) * QK_NOPE].astype(BF16)
                kctx_ref[:, a + QK_NOPE:a + HEAD_PAD] = kr
            vctx_ref[...] = kv[:, N_HEADS * QK_NOPE:].astype(BF16)
    else:
        q_ref, k_ref, v_ref, o_ref = refs

    for hd in range(N_HEADS):
        qk = slice(hd * HEAD_PAD, (hd + 1) * HEAD_PAD)
        vv = slice(hd * V_DIM, (hd + 1) * V_DIM)
        qh = q_ref[:, qk]
        s = _dot_nt(qh, k_ref[:, qk])
        m = jnp.max(s, axis=-1, keepdims=True)
        if has_ctx:
            sc = _dot_nt(qh, kctx_ref[:, qk])
            m = jnp.maximum(m, jnp.max(sc, axis=-1, keepdims=True))
            pc = jnp.exp(sc - m)
        p = jnp.exp(s - m)
        den = jnp.sum(p, axis=-1, keepdims=True)
        acc = _dot(p.astype(BF16), v_ref[:, vv])
        if has_ctx:
            den = den + jnp.sum(pc, axis=-1, keepdims=True)
            acc = acc + _dot(pc.astype(BF16), vctx_ref[:, vv])
        o_ref[:, vv] = (acc / den).astype(BF16)


def _attend(stream, layer, q, kcat, v, lw, ctx, tq):
    n = stream.seq
    nq = n // tq
    has_ctx = stream.latent
    ins = [q, kcat, v]
    specs = [pl.BlockSpec((tq, QK_W), lambda b, i: (b * nq + i, 0)),
             pl.BlockSpec((n, QK_W), lambda b, i: (b, 0)),
             pl.BlockSpec((n, MLA_W), lambda b, i: (b, 0))]
    scratch = []
    if has_ctx:
        cache_ckv, cache_kr = ctx
        past = cache_ckv.shape[2]
        ins += [cache_ckv, cache_kr, lw["wkv"]]
        specs += [pl.BlockSpec((None, None, past, KV_RANK), lambda b, i: (b, layer, 0, 0)),
                  pl.BlockSpec((None, None, past, LANES), lambda b, i: (b, layer, 0, 0)),
                  _const_spec(lw["wkv"].shape)]
        scratch = [pltpu.VMEM((past, QK_W), BF16), pltpu.VMEM((past, MLA_W), BF16)]
    return pl.pallas_call(
        functools.partial(_attend_body, has_ctx),
        grid=(stream.batch, nq),
        in_specs=specs,
        out_specs=pl.BlockSpec((tq, MLA_W), lambda b, i: (b * nq + i, 0)),
        out_shape=jax.ShapeDtypeStruct((stream.tokens, MLA_W), BF16),
        scratch_shapes=scratch,
        compiler_params=_params("parallel", "arbitrary"),
        name="attend",
    )(*ins)


def _mix_out_body(route, *refs):
    if route:
        (co_ref, mo_ref, wo1_ref, wo2_ref, x_ref, ga_ref, scf_ref, shf_ref, g_ref, b_ref,
         rh_ref, rl_ref, x1_ref, h2_ref, comb_ref) = refs
    else:
        (co_ref, mo_ref, wo1_ref, wo2_ref, x_ref, ga_ref, scf_ref, shf_ref, g_ref, b_ref,
         x1_ref, h2_ref) = refs
    mix = _dot(co_ref[...], wo1_ref[...]) + _dot(mo_ref[...], wo2_ref[...])
    x1 = _layernorm(ALPHA * x_ref[...] + ga_ref[...] * mix, g_ref[...], b_ref[...])
    x1_ref[...] = x1
    h2 = x1 * (1.0 + scf_ref[...]) + shf_ref[...]
    hh = h2.astype(BF16)
    h2_ref[...] = hh
    if route:
        hl = (h2 - hh.astype(F32)).astype(BF16)
        logits = _dot(hh, rh_ref[...]) + (_dot(hh, rl_ref[...]) + _dot(hl, rh_ref[...]))
        lane = lax.broadcasted_iota(jnp.int32, logits.shape, 1).astype(F32)
        neg = -jnp.inf
        lg = jnp.where(lane < N_EXPERTS, logits, neg)
        m1 = jnp.max(lg, axis=-1, keepdims=True)
        i1 = jnp.min(jnp.where(lg == m1, lane, float(LANES)), axis=-1, keepdims=True)
        lg2 = jnp.where(lane == i1, neg, lg)
        m2 = jnp.max(lg2, axis=-1, keepdims=True)
        i2 = jnp.min(jnp.where(lg2 == m2, lane, float(LANES)), axis=-1, keepdims=True)
        e2 = jnp.exp(m2 - m1)
        den = 1.0 + e2
        comb_ref[...] = jnp.where(lane == i1, 1.0 / den, 0.0) + jnp.where(lane == i2, e2 / den, 0.0)


def _mix_out(stream, layer, co, mo, x, mod, lw, route, tm):
    t = stream.tokens
    row = lambda w: pl.BlockSpec((tm, w), lambda i: (i, 0))
    vec = _const_spec((1, D_MODEL))
    ins = [co, mo, lw["wo_chunk"], lw["wo_mla"], x, mod, mod, mod, lw["ln_mix_g"], lw["ln_mix_b"]]
    specs = [row(CHUNK_W), row(MLA_W), _const_spec(lw["wo_chunk"].shape), _const_spec(lw["wo_mla"].shape),
             row(D_MODEL), stream.mod_spec(layer, G_A, tm), stream.mod_spec(layer, SC_F, tm),
             stream.mod_spec(layer, SH_F, tm), vec, vec]
    out_shape = [jax.ShapeDtypeStruct((t, D_MODEL), F32), jax.ShapeDtypeStruct((t, D_MODEL), BF16)]
    out_specs = [row(D_MODEL), row(D_MODEL)]
    if route:
        ins += [lw["router_hi"], lw["router_lo"]]
        specs += [_const_spec(lw["router_hi"].shape), _const_spec(lw["router_lo"].shape)]
        out_shape.append(jax.ShapeDtypeStruct((t, LANES), F32))
        out_specs.append(row(LANES))
    return pl.pallas_call(
        functools.partial(_mix_out_body, route),
        grid=(t // tm,),
        in_specs=specs,
        out_specs=out_specs,
        out_shape=out_shape,
        compiler_params=_params("parallel"),
        name="mix_out",
    )(*ins)


def _ffn_body(h_ref, x_ref, wg_ref, wu_ref, wd_ref, gf_ref, g_ref, b_ref, o_ref):
    h = h_ref[...]
    gate = _dot(h, wg_ref[...])
    up = _dot(h, wu_ref[...])
    a = (gate * _sigmoid(gate) * up).astype(BF16)
    f = _dot(a, wd_ref[...])
    o_ref[...] = _layernorm(ALPHA * x_ref[...] + gf_ref[...] * f, g_ref[...], b_ref[...])


def _ffn(stream, layer, h2, x1, mod, lw, tm):
    t = stream.tokens
    row = lambda w: pl.BlockSpec((tm, w), lambda i: (i, 0))
    vec = _const_spec((1, D_MODEL))
    resident = lambda a: pl.BlockSpec(a.shape, lambda i: (0, 0), pipeline_mode=pl.Buffered(1))
    return pl.pallas_call(
        _ffn_body,
        grid=(t // tm,),
        in_specs=[row(D_MODEL), row(D_MODEL), resident(lw["ffn_gate"]), resident(lw["ffn_up"]),
                  resident(lw["ffn_down"]), stream.mod_spec(layer, G_F, tm), vec, vec],
        out_specs=row(D_MODEL),
        out_shape=jax.ShapeDtypeStruct((t, D_MODEL), F32),
        compiler_params=_params("parallel"),
        name="ffn_dense",
    )(h2, x1, lw["ffn_gate"], lw["ffn_up"], lw["ffn_down"], mod, lw["ln_ffn_g"], lw["ln_ffn_b"])


def _moe_body(h_ref, comb_ref, x_ref, wg_ref, wu_ref, wd_ref, gf_ref, g_ref, b_ref, o_ref, acc_ref):
    e = pl.program_id(1)

    @pl.when(e == 0)
    def _():
        acc_ref[...] = jnp.zeros_like(acc_ref)

    h = h_ref[...]
    gate = _dot(h, wg_ref[...])
    up = _dot(h, wu_ref[...])
    a = (gate * _sigmoid(gate) * up).astype(BF16)
    f = _dot(a, wd_ref[...])
    comb = comb_ref[...]
    lane = lax.broadcasted_iota(jnp.int32, comb.shape, 1)
    w = jnp.sum(jnp.where(lane == e, comb, 0.0), axis=-1, keepdims=True)
    acc_ref[...] += w * f

    @pl.when(e == N_EXPERTS - 1)
    def _():
        o_ref[...] = _layernorm(ALPHA * x_ref[...] + gf_ref[...] * acc_ref[...], g_ref[...], b_ref[...])


def _moe(stream, layer, h2, comb, x1, mod, lw, tm):
    t = stream.tokens
    row = lambda w: pl.BlockSpec((tm, w), lambda i, e: (i, 0))
    vec = pl.BlockSpec((1, D_MODEL), lambda i, e: (0, 0))
    dff = lw["moe_gate"].shape[2]
    mod_spec = stream.mod_spec(layer, G_F, tm)
    mod_spec = pl.BlockSpec(mod_spec.block_shape, lambda i, e, f=mod_spec.index_map: f(i))
    return pl.pallas_call(
        _moe_body,
        grid=(t // tm, N_EXPERTS),
        in_specs=[row(D_MODEL), row(LANES), row(D_MODEL),
                  pl.BlockSpec((None, D_MODEL, dff), lambda i, e: (e, 0, 0)),
                  pl.BlockSpec((None, D_MODEL, dff), lambda i, e: (e, 0, 0)),
                  pl.BlockSpec((None, dff, D_MODEL), lambda i, e: (e, 0, 0)),
                  mod_spec, vec, vec],
        out_specs=row(D_MODEL),
        out_shape=jax.ShapeDtypeStruct((t, D_MODEL), F32),
        scratch_shapes=[pltpu.VMEM((tm, D_MODEL), F32)],
        compiler_params=_params("parallel", "arbitrary"),
        name="moe_dense",
    )(h2, comb, x1, lw["moe_gate"], lw["moe_up"], lw["moe_down"], mod, lw["ln_ffn_g"], lw["ln_ffn_b"])


_ROPE_SWAP = tuple(list(range(16, 32)) + list(range(0, 16)) + list(range(48, 64)) + list(range(32, 48)))


def _rope_tables(n_tokens):
    rows = n_tokens // GRID_W
    row = jnp.repeat(jnp.arange(rows, dtype=F32), GRID_W)
    col = jnp.tile(jnp.arange(GRID_W, dtype=F32), rows)
    half = ROPE_DIM // 2
    inv_freq = ROPE_BASE ** (-jnp.arange(0, half, 2, dtype=F32) / half)
    ar = row[:, None] * inv_freq[None, :]
    ac = col[:, None] * inv_freq[None, :]
    zeros = jnp.zeros((n_tokens, LANES - ROPE_DIM), F32)
    cos = jnp.concatenate([jnp.cos(ar), jnp.cos(ar), jnp.cos(ac), jnp.cos(ac), zeros], axis=-1)
    sin = jnp.concatenate([-jnp.sin(ar), jnp.sin(ar), -jnp.sin(ac), jnp.sin(ac), zeros], axis=-1)
    return cos, sin


def _layer_weights(l, p):
    swap = jnp.array(_ROPE_SWAP)
    w_in = p["w_in"][l]
    o_kr = 2 * CHUNK_W + Q_RANK + KV_RANK
    kr_cols = w_in[:, o_kr:o_kr + ROPE_DIM]
    zpad = jnp.zeros((D_MODEL, LANES - ROPE_DIM), F32)
    w_in_pad = jnp.concatenate([w_in, zpad], axis=1)
    w_in_lat = jnp.concatenate([w_in_pad, kr_cols[:, swap], zpad], axis=1)

    w_uq = p["w_uq"][l].reshape(Q_RANK, N_HEADS, QK_NOPE + ROPE_DIM)
    q_zero = jnp.zeros((Q_RANK, N_HEADS, HEAD_PAD - QK_NOPE - ROPE_DIM), F32)
    wq_main = jnp.concatenate([w_uq, q_zero], axis=-1).reshape(Q_RANK, QK_W)
    wq_swap = jnp.concatenate([w_uq[:, :, QK_NOPE:][:, :, swap], q_zero], axis=-1).reshape(Q_RANK, N_HEADS * LANES)

    w_ukv = p["w_ukv"][l].reshape(KV_RANK, N_HEADS, QK_NOPE + V_DIM)
    wkv = jnp.concatenate([w_ukv[:, :, :QK_NOPE].reshape(KV_RANK, N_HEADS * QK_NOPE),
                           w_ukv[:, :, QK_NOPE:].reshape(KV_RANK, MLA_W)], axis=1)
    w_out = p["w_out"][l]
    lw = {
        "w_in_ctx": w_in_pad.astype(BF16),
        "w_in_lat": w_in_lat.astype(BF16),
        "q_g": p["q_norm_g"][l].reshape(1, Q_RANK),
        "kv_g": p["kv_norm_g"][l].reshape(1, KV_RANK),
        "wq_main": wq_main.astype(BF16),
        "wq_swap": wq_swap.astype(BF16),
        "wkv": wkv.astype(BF16),
        "ln_v_g": p["chunk_ln_g"][l].reshape(1, CHUNK_W),
        "w_s": p["w_spatial"][l].astype(BF16),
        "b_s": jnp.broadcast_to(p["b_spatial"][l][:, :, None], (N_GROUPS, CHUNK, GROUP_W)),
        "wo_chunk": w_out[:CHUNK_W].astype(BF16),
        "wo_mla": w_out[CHUNK_W:].astype(BF16),
        "ln_mix_g": p["ln_mix_g"][l].reshape(1, D_MODEL),
        "ln_mix_b": p["ln_mix_b"][l].reshape(1, D_MODEL),
        "ln_ffn_g": p["ln_ffn_g"][l].reshape(1, D_MODEL),
        "ln_ffn_b": p["ln_ffn_b"][l].reshape(1, D_MODEL),
    }
    if l % 2 == 0:
        lw["ffn_gate"] = p["ffn_w_gate"][l // 2].astype(BF16)
        lw["ffn_up"] = p["ffn_w_up"][l // 2].astype(BF16)
        lw["ffn_down"] = p["ffn_w_down"][l // 2].astype(BF16)
    else:
        r = jnp.pad(p["router_w"][l // 2], ((0, 0), (0, LANES - N_EXPERTS)))
        r_hi = r.astype(BF16)
        lw["router_hi"] = r_hi
        lw["router_lo"] = (r - r_hi.astype(F32)).astype(BF16)
        lw["moe_gate"] = p["moe_w_gate"][l // 2].astype(BF16)
        lw["moe_up"] = p["moe_w_up"][l // 2].astype(BF16)
        lw["moe_down"] = p["moe_w_down"][l // 2].astype(BF16)
    return lw


def _trunk(stream, x, mod, weights, rope, ctx, tiles):
    caches = []
    for l in range(DEPTH):
        lw = dict(weights[l])
        lw["w_in"] = lw["w_in_lat"] if stream.latent else lw["w_in_ctx"]
        outs = _mixer_in(stream, l, x, mod, lw, rope, tiles["mixer_in"])
        co, q, kcat, v = outs[:4]
        if not stream.latent:
            caches.append(outs[4:])
        mo = _attend(stream, l, q, kcat, v, lw, ctx, tiles["attend"])
        route = l % 2 == 1
        outs = _mix_out(stream, l, co, mo, x, mod, lw, route, tiles["mix_out"])
        if route:
            x = _moe(stream, l, outs[1], outs[2], outs[0], mod, lw, tiles["moe"])
        else:
            x = _ffn(stream, l, outs[1], outs[0], mod, lw, tiles["ffn"])
    return x, caches


def kernel(x_prompt, x_sample, c, cache_ckv, cache_krope, c_ctx, w_mod, b_mod, w_in, q_norm_g, kv_norm_g, w_uq, w_ukv, chunk_ln_g, w_spatial, b_spatial, w_out, ln_mix_g, ln_mix_b, ln_ffn_g, ln_ffn_b, ffn_w_gate, ffn_w_up, ffn_w_down, router_w, moe_w_gate, moe_w_up, moe_w_down):
    p = dict(w_in=w_in, q_norm_g=q_norm_g, kv_norm_g=kv_norm_g, w_uq=w_uq, w_ukv=w_ukv,
             chunk_ln_g=chunk_ln_g, w_spatial=w_spatial, b_spatial=b_spatial, w_out=w_out,
             ln_mix_g=ln_mix_g, ln_mix_b=ln_mix_b, ln_ffn_g=ln_ffn_g, ln_ffn_b=ln_ffn_b,
             ffn_w_gate=ffn_w_gate, ffn_w_up=ffn_w_up, ffn_w_down=ffn_w_down, router_w=router_w,
             moe_w_gate=moe_w_gate, moe_w_up=moe_w_up, moe_w_down=moe_w_down)
    weights = [_layer_weights(l, p) for l in range(DEPTH)]

    batch, seq, _ = x_prompt.shape
    dec_batch, dec_seq, _ = x_sample.shape
    cond_rows = jnp.concatenate(
        [c_ctx[None, :], c, jnp.zeros((MOD_ROWS - 1 - dec_batch, D_MODEL), F32)], axis=0)
    mod = _modulation(cond_rows, w_mod, b_mod).reshape(DEPTH, MOD_ROWS, 6, 1, D_MODEL)

    prompt = _Stream(batch, seq, mod_row0=0, per_row_mod=False, latent=False)
    sample = _Stream(dec_batch, dec_seq, mod_row0=1, per_row_mod=True, latent=True)

    y_prompt, caches = _trunk(
        prompt, x_prompt.reshape(batch * seq, D_MODEL), mod, weights, None, None,
        dict(mixer_in=256, attend=256, mix_out=256, ffn=512, moe=512))
    new_ckv = jnp.stack([ck.reshape(batch, seq, KV_RANK) for ck, _ in caches], axis=1)
    new_krope = jnp.stack([kr.reshape(batch, seq, ROPE_DIM) for _, kr in caches], axis=1)

    rope = _rope_tables(dec_seq)
    ctx = (cache_ckv, jnp.pad(cache_krope, ((0, 0), (0, 0), (0, 0), (0, LANES - ROPE_DIM))))
    y_sample, _ = _trunk(
        sample, x_sample.reshape(dec_batch * dec_seq, D_MODEL), mod, weights, rope, ctx,
        dict(mixer_in=512, attend=512, mix_out=512, ffn=512, moe=512))
    return (y_prompt.reshape(batch, seq, D_MODEL), y_sample.reshape(dec_batch, dec_seq, D_MODEL),
            new_ckv, new_krope)
```

```python
import functools
import math

import jax
import jax.numpy as jnp
from jax import lax
from jax.experimental import pallas as pl
from jax.experimental.pallas import tpu as pltpu

F32 = jnp.float32
BF16 = jnp.bfloat16

D_MODEL = 1024
DEPTH = 2
GRID_W = 64
CHUNK = 128
N_GROUPS = 4
GROUP_W = 128
CHUNK_W = N_GROUPS * GROUP_W
N_HEADS = 4
QK_NOPE = 128
ROPE_DIM = 64
V_DIM = 128
Q_RANK = 384
KV_RANK = 256
MLA_W = N_HEADS * V_DIM
HEAD_PAD = 256
QK_W = N_HEADS * HEAD_PAD
ROPE_BASE = 10000.0
N_EXPERTS = 8
ALPHA = (2 * DEPTH) ** 0.25
EPS = 1e-6
ATTN_SCALE = 1.0 / math.sqrt(QK_NOPE + ROPE_DIM)
MOD_ROWS = 16
LANES = 128
VMEM_LIMIT = 56 * 1024 * 1024

SH_A, SC_A, G_A, SH_F, SC_F, G_F = range(6)


def _sigmoid(x):
    return 1.0 / (1.0 + jnp.exp(-x))


def _gelu_tanh(x):
    return 0.5 * x * (1.0 + jnp.tanh(math.sqrt(2.0 / math.pi) * (x + 0.044715 * (x * x * x))))


def _layernorm(y, g, b):
    mu = jnp.mean(y, axis=-1, keepdims=True)
    d = y - mu
    var = jnp.mean(d * d, axis=-1, keepdims=True)
    return d * lax.rsqrt(var + EPS) * g + b


def _rmsnorm(y, g):
    return y * lax.rsqrt(jnp.mean(y * y, axis=-1, keepdims=True) + EPS) * g


def _dot(a, b):
    return jnp.dot(a, b, preferred_element_type=F32)


def _dot_nt(a, b):
    return lax.dot_general(a, b, (((1,), (1,)), ((), ())), preferred_element_type=F32)


def _params(*sem):
    return pltpu.CompilerParams(dimension_semantics=sem, vmem_limit_bytes=VMEM_LIMIT)


def _const_spec(shape):
    nd = len(shape)
    return pl.BlockSpec(shape, lambda *_: (0,) * nd)


class _Stream:
    def __init__(self, batch, seq, mod_row0, per_row_mod, latent):
        self.batch = batch
        self.seq = seq
        self.tokens = batch * seq
        self.mod_row0 = mod_row0
        self.per_row_mod = per_row_mod
        self.latent = latent

    def mod_spec(self, layer, which, tm):
        tiles_per_row = self.seq // tm
        row0, per_row = self.mod_row0, self.per_row_mod

        def index(i):
            r = row0 + (i // tiles_per_row if per_row else 0)
            return (layer, r, which, 0, 0)

        return pl.BlockSpec((None, None, None, 1, D_MODEL), index)


def _mod_body(c_ref, w_ref, b_ref, o_ref):
    a = c_ref[...]
    a = (a * _sigmoid(a)).astype(BF16)
    o_ref[...] = _dot(a, w_ref[...].astype(BF16)) + b_ref[...]


def _modulation(cond_rows, w_mod, b_mod):
    depth, _, width = w_mod.shape
    tn = 1536
    return pl.pallas_call(
        _mod_body,
        grid=(depth, width // tn),
        in_specs=[
            pl.BlockSpec((MOD_ROWS, D_MODEL), lambda l, j: (0, 0)),
            pl.BlockSpec((None, D_MODEL, tn), lambda l, j: (l, 0, j)),
            pl.BlockSpec((None, 1, tn), lambda l, j: (l, 0, j)),
        ],
        out_specs=pl.BlockSpec((None, MOD_ROWS, tn), lambda l, j: (l, 0, j)),
        out_shape=jax.ShapeDtypeStruct((depth, MOD_ROWS, width), F32),
        compiler_params=_params("parallel", "parallel"),
        name="modulation",
    )(cond_rows, w_mod, b_mod.reshape(depth, 1, width))


def _mixer_in_body(latent, tm, *refs):
    it = iter(refs)
    x_ref, sc_ref, sh_ref, win_ref, qg_ref, kvg_ref, wqm_ref = (next(it) for _ in range(7))
    wqs_ref = next(it) if latent else None
    wkv_ref, lng_ref, ws_ref, bs_ref = (next(it) for _ in range(4))
    cos_ref, sin_ref = (next(it), next(it)) if latent else (None, None)
    co_ref, q_ref, kcat_ref, v_ref = (next(it) for _ in range(4))
    ckv_ref, kr_ref = (None, None) if latent else (next(it), next(it))

    h = (x_ref[...] * (1.0 + sc_ref[...]) + sh_ref[...]).astype(BF16)
    p = _dot(h, win_ref[...])

    for g in range(N_GROUPS):
        cols = slice(g * GROUP_W, (g + 1) * GROUP_W)
        vg = _gelu_tanh(p[:, CHUNK_W + g * GROUP_W:CHUNK_W + (g + 1) * GROUP_W])
        mu = jnp.mean(vg, axis=-1, keepdims=True)
        d = vg - mu
        var = jnp.mean(d * d, axis=-1, keepdims=True)
        vn = (d * lax.rsqrt(var + EPS) * lng_ref[:, cols]).astype(BF16)
        ug = _gelu_tanh(p[:, cols])
        for c in range(tm // CHUNK):
            rows = slice(c * CHUNK, (c + 1) * CHUNK)
            z = _dot(ws_ref[g], vn[rows]) + bs_ref[g]
            co_ref[rows, cols] = (ug[rows] * z).astype(BF16)

    o = 2 * CHUNK_W
    cqn = _rmsnorm(p[:, o:o + Q_RANK], qg_ref[...]).astype(BF16)
    qm = _dot(cqn, wqm_ref[...])
    if latent:
        qs = _dot(cqn, wqs_ref[...])
        cos = cos_ref[...]
        sin = sin_ref[...]
    for hd in range(N_HEADS):
        a = hd * HEAD_PAD
        q_ref[:, a:a + QK_NOPE] = (qm[:, a:a + QK_NOPE] * ATTN_SCALE).astype(BF16)
        qr = qm[:, a + QK_NOPE:a + HEAD_PAD]
        if latent:
            qr = qr * cos + qs[:, hd * LANES:(hd + 1) * LANES] * sin
        q_ref[:, a + QK_NOPE:a + HEAD_PAD] = (qr * ATTN_SCALE).astype(BF16)

    o += Q_RANK
    ckvn = _rmsnorm(p[:, o:o + KV_RANK], kvg_ref[...])
    o += KV_RANK
    kr = p[:, o:o + LANES]
    if not latent:
        ckv_ref[...] = ckvn
        kr_ref[...] = kr[:, :ROPE_DIM]
    else:
        kr = kr * cos + p[:, o + LANES:o + 2 * LANES] * sin
    kr = kr.astype(BF16)
    kv = _dot(ckvn.astype(BF16), wkv_ref[...])
    for hd in range(N_HEADS):
        a = hd * HEAD_PAD
        kcat_ref[:, a:a + QK_NOPE] = kv[:, hd * QK_NOPE:(hd + 1) * QK_NOPE].astype(BF16)
        kcat_ref[:, a + QK_NOPE:a + HEAD_PAD] = kr
    v_ref[...] = kv[:, N_HEADS * QK_NOPE:].astype(BF16)


def _mixer_in(stream, layer, x, mod, lw, rope, tm):
    latent = stream.latent
    t = stream.tokens
    row = lambda w: pl.BlockSpec((tm, w), lambda i: (i, 0))
    win = lw["w_in"]
    ins = [x, mod, mod, win, lw["q_g"], lw["kv_g"], lw["wq_main"]]
    specs = [row(D_MODEL), stream.mod_spec(layer, SC_A, tm), stream.mod_spec(layer, SH_A, tm),
             _const_spec(win.shape), _const_spec(lw["q_g"].shape), _const_spec(lw["kv_g"].shape),
             _const_spec(lw["wq_main"].shape)]
    if latent:
        ins.append(lw["wq_swap"])
        specs.append(_const_spec(lw["wq_swap"].shape))
    for name in ("wkv", "ln_v_g", "w_s", "b_s"):
        ins.append(lw[name])
        specs.append(_const_spec(lw[name].shape))
    if latent:
        tiles_per_seq = stream.seq // tm
        pos = pl.BlockSpec((tm, LANES), lambda i: (i % tiles_per_seq, 0))
        ins += [rope[0], rope[1]]
        specs += [pos, pos]
    out_shape = [jax.ShapeDtypeStruct((t, CHUNK_W), BF16), jax.ShapeDtypeStruct((t, QK_W), BF16),
                 jax.ShapeDtypeStruct((t, QK_W), BF16), jax.ShapeDtypeStruct((t, MLA_W), BF16)]
    out_specs = [row(CHUNK_W), row(QK_W), row(QK_W), row(MLA_W)]
    if not latent:
        out_shape += [jax.ShapeDtypeStruct((t, KV_RANK), F32), jax.ShapeDtypeStruct((t, ROPE_DIM), F32)]
        out_specs += [row(KV_RANK), row(ROPE_DIM)]
    return pl.pallas_call(
        functools.partial(_mixer_in_body, latent, tm),
        grid=(t // tm,),
        in_specs=specs,
        out_specs=out_specs,
        out_shape=out_shape,
        compiler_params=_params("parallel"),
        name="mixer_in",
    )(*ins)


def _attend_body(has_ctx, *refs):
    if has_ctx:
        q_ref, k_ref, v_ref, cckv_ref, ckr_ref, wkv_ref, o_ref, kctx_ref, vctx_ref = refs

        @pl.when(pl.program_id(1) == 0)
        def _():
            kv = _dot(cckv_ref[...].astype(BF16), wkv_ref[...])
            kr = ckr_ref[...].astype(BF16)
            for hd in range(N_HEADS):
                a = hd * HEAD_PAD
                kctx_ref[:, a:a + QK_NOPE] = kv[:, hd * QK_NOPE:(hd + 1) * QK_NOPE].astype(BF16)
                kctx_ref[:, a + QK_NOPE:a + HEAD_PAD] = kr
            vctx_ref[...] = kv[:, N_HEADS * QK_NOPE:].astype(BF16)
    else:
        q_ref, k_ref, v_ref, o_ref = refs

    for hd in range(N_HEADS):
        qk = slice(hd * HEAD_PAD, (hd + 1) * HEAD_PAD)
        vv = slice(hd * V_DIM, (hd + 1) * V_DIM)
        qh = q_ref[:, qk]
        s = _dot_nt(qh, k_ref[:, qk])
        m = jnp.max(s, axis=-1, keepdims=True)
        if has_ctx:
            sc = _dot_nt(qh, kctx_ref[:, qk])
            m = jnp.maximum(m, jnp.max(sc, axis=-1, keepdims=True))
            pc = jnp.exp(sc - m)
        p = jnp.exp(s - m)
        den = jnp.sum(p, axis=-1, keepdims=True)
        acc = _dot(p.astype(BF16), v_ref[:, vv])
        if has_ctx:
            den = den + jnp.sum(pc, axis=-1, keepdims=True)
            acc = acc + _dot(pc.astype(BF16), vctx_ref[:, vv])
        o_ref[:, vv] = (acc / den).astype(BF16)


def _attend(stream, layer, q, kcat, v, lw, ctx, tq):
    n = stream.seq
    nq = n // tq
    has_ctx = stream.latent
    ins = [q, kcat, v]
    specs = [pl.BlockSpec((tq, QK_W), lambda b, i: (b * nq + i, 0)),
             pl.BlockSpec((n, QK_W), lambda b, i: (b, 0)),
             pl.BlockSpec((n, MLA_W), lambda b, i: (b, 0))]
    scratch = []
    if has_ctx:
        cache_ckv, cache_kr = ctx
        past = cache_ckv.shape[2]
        ins += [cache_ckv, cache_kr, lw["wkv"]]
        specs += [pl.BlockSpec((None, None, past, KV_RANK), lambda b, i: (b, layer, 0, 0)),
                  pl.BlockSpec((None, None, past, LANES), lambda b, i: (b, layer, 0, 0)),
                  _const_spec(lw["wkv"].shape)]
        scratch = [pltpu.VMEM((past, QK_W), BF16), pltpu.VMEM((past, MLA_W), BF16)]
    return pl.pallas_call(
        functools.partial(_attend_body, has_ctx),
        grid=(stream.batch, nq),
        in_specs=specs,
        out_specs=pl.BlockSpec((tq, MLA_W), lambda b, i: (b * nq + i, 0)),
        out_shape=jax.ShapeDtypeStruct((stream.tokens, MLA_W), BF16),
        scratch_shapes=scratch,
        compiler_params=_params("parallel", "arbitrary"),
        name="attend",
    )(*ins)


def _mix_out_body(route, *refs):
    if route:
        (co_ref, mo_ref, wo1_ref, wo2_ref, x_ref, ga_ref, scf_ref, shf_ref, g_ref, b_ref,
         rh_ref, rl_ref, x1_ref, h2_ref, meta_ref, cnt_ref) = refs
    else:
        (co_ref, mo_ref, wo1_ref, wo2_ref, x_ref, ga_ref, scf_ref, shf_ref, g_ref, b_ref,
         x1_ref, h2_ref) = refs
    mix = _dot(co_ref[...], wo1_ref[...]) + _dot(mo_ref[...], wo2_ref[...])
    x1 = _layernorm(ALPHA * x_ref[...] + ga_ref[...] * mix, g_ref[...], b_ref[...])
    x1_ref[...] = x1
    h2 = x1 * (1.0 + scf_ref[...]) + shf_ref[...]
    hh = h2.astype(BF16)
    h2_ref[...] = hh
    if route:
        hl = (h2 - hh.astype(F32)).astype(BF16)
        logits = _dot(hh, rh_ref[...]) + (_dot(hh, rl_ref[...]) + _dot(hl, rh_ref[...]))
        lane = lax.broadcasted_iota(jnp.int32, logits.shape, 1).astype(F32)
        neg = -jnp.inf
        lg = jnp.where(lane < N_EXPERTS, logits, neg)
        m1 = jnp.max(lg, axis=-1, keepdims=True)
        i1 = jnp.min(jnp.where(lg == m1, lane, float(LANES)), axis=-1, keepdims=True)
        lg2 = jnp.where(lane == i1, neg, lg)
        m2 = jnp.max(lg2, axis=-1, keepdims=True)
        i2 = jnp.min(jnp.where(lg2 == m2, lane, float(LANES)), axis=-1, keepdims=True)
        e2 = jnp.exp(m2 - m1)
        den = 1.0 + e2
        tm = logits.shape[0]
        picked = jnp.where(lane == i1, 1.0, jnp.where(lane == i2, 1.0, 0.0))
        earlier = jnp.where(lax.broadcasted_iota(jnp.int32, (tm, tm), 0)
                            > lax.broadcasted_iota(jnp.int32, (tm, tm), 1), 1.0, 0.0).astype(BF16)
        rank = _dot(earlier, picked.astype(BF16))
        rank1 = jnp.sum(jnp.where(lane == i1, rank, 0.0), axis=-1, keepdims=True)
        rank2 = jnp.sum(jnp.where(lane == i2, rank, 0.0), axis=-1, keepdims=True)
        meta = jnp.zeros_like(logits)
        for k, val in enumerate((i1, i2, 1.0 / den, e2 / den, rank1, rank2)):
            meta = jnp.where(lane == float(k), val, meta)
        meta_ref[...] = meta
        cnt_ref[...] = jnp.sum(picked, axis=0, keepdims=True)


def _mix_out(stream, layer, co, mo, x, mod, lw, route, tm):
    t = stream.tokens
    row = lambda w: pl.BlockSpec((tm, w), lambda i: (i, 0))
    vec = _const_spec((1, D_MODEL))
    ins = [co, mo, lw["wo_chunk"], lw["wo_mla"], x, mod, mod, mod, lw["ln_mix_g"], lw["ln_mix_b"]]
    specs = [row(CHUNK_W), row(MLA_W), _const_spec(lw["wo_chunk"].shape), _const_spec(lw["wo_mla"].shape),
             row(D_MODEL), stream.mod_spec(layer, G_A, tm), stream.mod_spec(layer, SC_F, tm),
             stream.mod_spec(layer, SH_F, tm), vec, vec]
    out_shape = [jax.ShapeDtypeStruct((t, D_MODEL), F32), jax.ShapeDtypeStruct((t, D_MODEL), BF16)]
    out_specs = [row(D_MODEL), row(D_MODEL)]
    if route:
        ins += [lw["router_hi"], lw["router_lo"]]
        specs += [_const_spec(lw["router_hi"].shape), _const_spec(lw["router_lo"].shape)]
        out_shape += [jax.ShapeDtypeStruct((t, LANES), F32), jax.ShapeDtypeStruct((t // tm, 1, LANES), F32)]
        out_specs += [row(LANES), pl.BlockSpec((None, 1, LANES), lambda i: (i, 0, 0))]
    return pl.pallas_call(
        functools.partial(_mix_out_body, route),
        grid=(t // tm,),
        in_specs=specs,
        out_specs=out_specs,
        out_shape=out_shape,
        compiler_params=_params("parallel"),
        name="mix_out",
    )(*ins)


def _ffn_body(h_ref, x_ref, wg_ref, wu_ref, wd_ref, gf_ref, g_ref, b_ref, o_ref):
    h = h_ref[...]
    gate = _dot(h, wg_ref[...])
    up = _dot(h, wu_ref[...])
    a = (gate * _sigmoid(gate) * up).astype(BF16)
    f = _dot(a, wd_ref[...])
    o_ref[...] = _layernorm(ALPHA * x_ref[...] + gf_ref[...] * f, g_ref[...], b_ref[...])


def _ffn(stream, layer, h2, x1, mod, lw, tm):
    t = stream.tokens
    row = lambda w: pl.BlockSpec((tm, w), lambda i: (i, 0))
    vec = _const_spec((1, D_MODEL))
    resident = lambda a: pl.BlockSpec(a.shape, lambda i: (0, 0), pipeline_mode=pl.Buffered(1))
    return pl.pallas_call(
        _ffn_body,
        grid=(t // tm,),
        in_specs=[row(D_MODEL), row(D_MODEL), resident(lw["ffn_gate"]), resident(lw["ffn_up"]),
                  resident(lw["ffn_down"]), stream.mod_spec(layer, G_F, tm), vec, vec],
        out_specs=row(D_MODEL),
        out_shape=jax.ShapeDtypeStruct((t, D_MODEL), F32),
        compiler_params=_params("parallel"),
        name="ffn_dense",
    )(h2, x1, lw["ffn_gate"], lw["ffn_up"], lw["ffn_down"], mod, lw["ln_ffn_g"], lw["ln_ffn_b"])


RUN_ALIGN = 16
EXPERT_TILE = 256


def _route_plan(cnt, tokens, tm):
    nw = tokens // tm
    worst_pad = (RUN_ALIGN - 1) * nw
    cap = pl.cdiv(tokens + worst_pad, EXPERT_TILE) * EXPERT_TILE
    n_tiles = (2 * tokens + N_EXPERTS * worst_pad) // EXPERT_TILE + N_EXPERTS
    cnt = cnt[:, 0, :N_EXPERTS].astype(jnp.int32)
    run = (cnt + RUN_ALIGN - 1) // RUN_ALIGN * RUN_ALIGN
    off = jnp.cumsum(run, axis=1) - run
    base = jnp.arange(N_EXPERTS, dtype=jnp.int32) * cap + jnp.cumsum(run, axis=0) - run
    totals = jnp.sum(run, axis=0)
    tiles = (totals + EXPERT_TILE - 1) // EXPERT_TILE
    cum = jnp.cumsum(tiles)
    n_valid = cum[-1]
    i = jnp.minimum(jnp.arange(n_tiles, dtype=jnp.int32), n_valid - 1)
    te = jnp.sum((cum[None, :] <= i[:, None]).astype(jnp.int32), axis=1)
    tj = i - (cum - tiles)[te]
    return dict(
        tm=tm, nw=nw, cap=cap, n_tiles=n_tiles, buf_rows=2 * tm + N_EXPERTS * RUN_ALIGN,
        off=off.reshape(-1), run=run.reshape(-1), base=base.reshape(-1),
        tile_expert=te, tile_block=te * (cap // EXPERT_TILE) + tj,
        tile_rows=jnp.minimum(EXPERT_TILE, totals[te] - tj * EXPERT_TILE),
        n_valid=n_valid.reshape(1))


def _run_piece_sizes(tm):
    sizes, b = [], RUN_ALIGN
    while b <= tm:
        sizes.append(b)
        b *= 2
    return sizes


def _row_run_copies(src, dst, src_off, dst_off, n_rows, sem, sizes, action):
    for b in sizes:
        done = n_rows & (-2 * b)
        cp = pltpu.make_async_copy(
            src.at[pl.ds(pl.multiple_of(src_off + done, RUN_ALIGN), b)],
            dst.at[pl.ds(pl.multiple_of(dst_off + done, RUN_ALIGN), b)], sem)

        @pl.when((n_rows & b) != 0)
        def _():
            action(cp)


def _local_rows(meta, off_ref, w):
    i1, i2 = meta[:, 0:1], meta[:, 1:2]
    o1, o2 = jnp.zeros_like(i1), jnp.zeros_like(i2)
    for e in range(N_EXPERTS):
        off_e = off_ref[w * N_EXPERTS + e].astype(F32)
        o1 = jnp.where(i1 == float(e), off_e, o1)
        o2 = jnp.where(i2 == float(e), off_e, o2)
    return (o1 + meta[:, 4:5]).astype(jnp.int32), (o2 + meta[:, 5:6]).astype(jnp.int32)


def _dispatch_body(tm, off_ref, run_ref, base_ref, h_ref, meta_ref, xs_ref, xw_ref, sem):
    w = pl.program_id(0)
    r1, r2 = _local_rows(meta_ref[...], off_ref, w)
    col = lax.broadcasted_iota(jnp.int32, (tm, xw_ref.shape[0]), 1)
    sel = jnp.where(col == r1, 1.0, jnp.where(col == r2, 1.0, 0.0)).astype(BF16)
    xw = lax.dot_general(sel, h_ref[...], (((0,), (0,)), ((), ())), preferred_element_type=F32)
    xw_ref[...] = xw.astype(BF16)
    sizes = _run_piece_sizes(tm)
    for action in (lambda cp: cp.start(), lambda cp: cp.wait()):
        for e in range(N_EXPERTS):
            k = w * N_EXPERTS + e
            _row_run_copies(xw_ref, xs_ref, off_ref[k], base_ref[k], run_ref[k], sem, sizes, action)


def _dispatch(h2, meta, plan):
    tm = plan["tm"]
    return pl.pallas_call(
        functools.partial(_dispatch_body, tm),
        grid_spec=pltpu.PrefetchScalarGridSpec(
            num_scalar_prefetch=3,
            grid=(plan["nw"],),
            in_specs=[pl.BlockSpec((tm, D_MODEL), lambda i, *_: (i, 0)),
                      pl.BlockSpec((tm, LANES), lambda i, *_: (i, 0))],
            out_specs=pl.BlockSpec(memory_space=pl.ANY),
            scratch_shapes=[pltpu.VMEM((plan["buf_rows"], D_MODEL), BF16), pltpu.SemaphoreType.DMA(())]),
        out_shape=jax.ShapeDtypeStruct((N_EXPERTS * plan["cap"], D_MODEL), BF16),
        compiler_params=_params("arbitrary"),
        name="moe_dispatch",
    )(plan["off"], plan["run"], plan["base"], h2, meta)


def _experts_body(te_ref, tb_ref, rows_ref, nv_ref, x_ref, wg_ref, wu_ref, wd_ref, o_ref):
    i = pl.program_id(0)

    @pl.when(i < nv_ref[0])
    def _():
        x = x_ref[...]
        live = lax.broadcasted_iota(jnp.int32, (x.shape[0], 1), 0) < rows_ref[i]
        x = jnp.where(live, x, jnp.zeros_like(x))
        gate = _dot(x, wg_ref[...])
        up = _dot(x, wu_ref[...])
        a = (gate * _sigmoid(gate) * up).astype(BF16)
        o_ref[...] = _dot(a, wd_ref[...]).astype(BF16)


def _experts(xs, plan, lw):
    dff = lw["moe_gate"].shape[2]
    rows = pl.BlockSpec((EXPERT_TILE, D_MODEL), lambda i, te, tb, *_: (tb[i], 0))
    return pl.pallas_call(
        _experts_body,
        grid_spec=pltpu.PrefetchScalarGridSpec(
            num_scalar_prefetch=4,
            grid=(plan["n_tiles"],),
            in_specs=[rows,
                      pl.BlockSpec((None, D_MODEL, dff), lambda i, te, *_: (te[i], 0, 0)),
                      pl.BlockSpec((None, D_MODEL, dff), lambda i, te, *_: (te[i], 0, 0)),
                      pl.BlockSpec((None, dff, D_MODEL), lambda i, te, *_: (te[i], 0, 0))],
            out_specs=rows),
        out_shape=jax.ShapeDtypeStruct(xs.shape, BF16),
        compiler_params=_params("arbitrary"),
        name="moe_experts",
    )(plan["tile_expert"], plan["tile_block"], plan["tile_rows"], plan["n_valid"],
      xs, lw["moe_gate"], lw["moe_up"], lw["moe_down"])


def _combine_body(tm, off_ref, run_ref, base_ref, ys_ref, meta_ref, x_ref, gf_ref, g_ref, b_ref,
                  o_ref, yw_ref, sem):
    w = pl.program_id(0)

    @pl.when(w == 0)
    def _():
        yw_ref[...] = jnp.zeros_like(yw_ref)

    sizes = _run_piece_sizes(tm)
    for action in (lambda cp: cp.start(), lambda cp: cp.wait()):
        for e in range(N_EXPERTS):
            k = w * N_EXPERTS + e
            _row_run_copies(ys_ref, yw_ref, base_ref[k], off_ref[k], run_ref[k], sem, sizes, action)
    meta = meta_ref[...]
    r1, r2 = _local_rows(meta, off_ref, w)
    col = lax.broadcasted_iota(jnp.int32, (tm, yw_ref.shape[0]), 1)
    yw = yw_ref[...]
    f = (meta[:, 2:3] * _dot(jnp.where(col == r1, 1.0, 0.0).astype(BF16), yw)
         + meta[:, 3:4] * _dot(jnp.where(col == r2, 1.0, 0.0).astype(BF16), yw))
    o_ref[...] = _layernorm(ALPHA * x_ref[...] + gf_ref[...] * f, g_ref[...], b_ref[...])


def _combine(stream, layer, ys, meta, x1, mod, plan, lw):
    tm = plan["tm"]
    row = lambda width: pl.BlockSpec((tm, width), lambda i, *_: (i, 0))
    vec = pl.BlockSpec((1, D_MODEL), lambda i, *_: (0, 0))
    mod_spec = stream.mod_spec(layer, G_F, tm)
    mod_spec = pl.BlockSpec(mod_spec.block_shape, lambda i, *_, f=mod_spec.index_map: f(i))
    return pl.pallas_call(
        functools.partial(_combine_body, tm),
        grid_spec=pltpu.PrefetchScalarGridSpec(
            num_scalar_prefetch=3,
            grid=(plan["nw"],),
            in_specs=[pl.BlockSpec(memory_space=pl.ANY), row(LANES), row(D_MODEL), mod_spec, vec, vec],
            out_specs=row(D_MODEL),
            scratch_shapes=[pltpu.VMEM((plan["buf_rows"], D_MODEL), BF16), pltpu.SemaphoreType.DMA(())]),
        out_shape=jax.ShapeDtypeStruct((stream.tokens, D_MODEL), F32),
        compiler_params=_params("arbitrary"),
        name="moe_combine",
    )(plan["off"], plan["run"], plan["base"], ys, meta, x1, mod, lw["ln_ffn_g"], lw["ln_ffn_b"])


_ROPE_SWAP = tuple(list(range(16, 32)) + list(range(0, 16)) + list(range(48, 64)) + list(range(32, 48)))


def _rope_tables(n_tokens):
    rows = n_tokens // GRID_W
    row = jnp.repeat(jnp.arange(rows, dtype=F32), GRID_W)
    col = jnp.tile(jnp.arange(GRID_W, dtype=F32), rows)
    half = ROPE_DIM // 2
    inv_freq = ROPE_BASE ** (-jnp.arange(0, half, 2, dtype=F32) / half)
    ar = row[:, None] * inv_freq[None, :]
    ac = col[:, None] * inv_freq[None, :]
    zeros = jnp.zeros((n_tokens, LANES - ROPE_DIM), F32)
    cos = jnp.concatenate([jnp.cos(ar), jnp.cos(ar), jnp.cos(ac), jnp.cos(ac), zeros], axis=-1)
    sin = jnp.concatenate([-jnp.sin(ar), jnp.sin(ar), -jnp.sin(ac), jnp.sin(ac), zeros], axis=-1)
    return cos, sin


def _layer_weights(l, p):
    swap = jnp.array(_ROPE_SWAP)
    w_in = p["w_in"][l]
    o_kr = 2 * CHUNK_W + Q_RANK + KV_RANK
    kr_cols = w_in[:, o_kr:o_kr + ROPE_DIM]
    zpad = jnp.zeros((D_MODEL, LANES - ROPE_DIM), F32)
    w_in_pad = jnp.concatenate([w_in, zpad], axis=1)
    w_in_lat = jnp.concatenate([w_in_pad, kr_cols[:, swap], zpad], axis=1)

    w_uq = p["w_uq"][l].reshape(Q_RANK, N_HEADS, QK_NOPE + ROPE_DIM)
    q_zero = jnp.zeros((Q_RANK, N_HEADS, HEAD_PAD - QK_NOPE - ROPE_DIM), F32)
    wq_main = jnp.concatenate([w_uq, q_zero], axis=-1).reshape(Q_RANK, QK_W)
    wq_swap = jnp.concatenate([w_uq[:, :, QK_NOPE:][:, :, swap], q_zero], axis=-1).reshape(Q_RANK, N_HEADS * LANES)

    w_ukv = p["w_ukv"][l].reshape(KV_RANK, N_HEADS, QK_NOPE + V_DIM)
    wkv = jnp.concatenate([w_ukv[:, :, :QK_NOPE].reshape(KV_RANK, N_HEADS * QK_NOPE),
                           w_ukv[:, :, QK_NOPE:].reshape(KV_RANK, MLA_W)], axis=1)
    w_out = p["w_out"][l]
    lw = {
        "w_in_ctx": w_in_pad.astype(BF16),
        "w_in_lat": w_in_lat.astype(BF16),
        "q_g": p["q_norm_g"][l].reshape(1, Q_RANK),
        "kv_g": p["kv_norm_g"][l].reshape(1, KV_RANK),
        "wq_main": wq_main.astype(BF16),
        "wq_swap": wq_swap.astype(BF16),
        "wkv": wkv.astype(BF16),
        "ln_v_g": p["chunk_ln_g"][l].reshape(1, CHUNK_W),
        "w_s": p["w_spatial"][l].astype(BF16),
        "b_s": jnp.broadcast_to(p["b_spatial"][l][:, :, None], (N_GROUPS, CHUNK, GROUP_W)),
        "wo_chunk": w_out[:CHUNK_W].astype(BF16),
        "wo_mla": w_out[CHUNK_W:].astype(BF16),
        "ln_mix_g": p["ln_mix_g"][l].reshape(1, D_MODEL),
        "ln_mix_b": p["ln_mix_b"][l].reshape(1, D_MODEL),
        "ln_ffn_g": p["ln_ffn_g"][l].reshape(1, D_MODEL),
        "ln_ffn_b": p["ln_ffn_b"][l].reshape(1, D_MODEL),
    }
    if l % 2 == 0:
        lw["ffn_gate"] = p["ffn_w_gate"][l // 2].astype(BF16)
        lw["ffn_up"] = p["ffn_w_up"][l // 2].astype(BF16)
        lw["ffn_down"] = p["ffn_w_down"][l // 2].astype(BF16)
    else:
        r = jnp.pad(p["router_w"][l // 2], ((0, 0), (0, LANES - N_EXPERTS)))
        r_hi = r.astype(BF16)
        lw["router_hi"] = r_hi
        lw["router_lo"] = (r - r_hi.astype(F32)).astype(BF16)
        lw["moe_gate"] = p["moe_w_gate"][l // 2].astype(BF16)
        lw["moe_up"] = p["moe_w_up"][l // 2].astype(BF16)
        lw["moe_down"] = p["moe_w_down"][l // 2].astype(BF16)
    return lw


def _trunk(stream, x, mod, weights, rope, ctx, tiles):
    caches = []
    for l in range(DEPTH):
        lw = dict(weights[l])
        lw["w_in"] = lw["w_in_lat"] if stream.latent else lw["w_in_ctx"]
        outs = _mixer_in(stream, l, x, mod, lw, rope, tiles["mixer_in"])
        co, q, kcat, v = outs[:4]
        if not stream.latent:
            caches.append(outs[4:])
        mo = _attend(stream, l, q, kcat, v, lw, ctx, tiles["attend"])
        route = l % 2 == 1
        outs = _mix_out(stream, l, co, mo, x, mod, lw, route, tiles["mix_out"])
        if route:
            x1, h2, meta, cnt = outs
            plan = _route_plan(cnt, stream.tokens, tiles["mix_out"])
            ys = _experts(_dispatch(h2, meta, plan), plan, lw)
            x = _combine(stream, l, ys, meta, x1, mod, plan, lw)
        else:
            x = _ffn(stream, l, outs[1], outs[0], mod, lw, tiles["ffn"])
    return x, caches


def kernel(x_prompt, x_sample, c, cache_ckv, cache_krope, c_ctx, w_mod, b_mod, w_in, q_norm_g, kv_norm_g, w_uq, w_ukv, chunk_ln_g, w_spatial, b_spatial, w_out, ln_mix_g, ln_mix_b, ln_ffn_g, ln_ffn_b, ffn_w_gate, ffn_w_up, ffn_w_down, router_w, moe_w_gate, moe_w_up, moe_w_down):
    p = dict(w_in=w_in, q_norm_g=q_norm_g, kv_norm_g=kv_norm_g, w_uq=w_uq, w_ukv=w_ukv,
             chunk_ln_g=chunk_ln_g, w_spatial=w_spatial, b_spatial=b_spatial, w_out=w_out,
             ln_mix_g=ln_mix_g, ln_mix_b=ln_mix_b, ln_ffn_g=ln_ffn_g, ln_ffn_b=ln_ffn_b,
             ffn_w_gate=ffn_w_gate, ffn_w_up=ffn_w_up, ffn_w_down=ffn_w_down, router_w=router_w,
             moe_w_gate=moe_w_gate, moe_w_up=moe_w_up, moe_w_down=moe_w_down)
    weights = [_layer_weights(l, p) for l in range(DEPTH)]

    batch, seq, _ = x_prompt.shape
    dec_batch, dec_seq, _ = x_sample.shape
    cond_rows = jnp.concatenate(
        [c_ctx[None, :], c, jnp.zeros((MOD_ROWS - 1 - dec_batch, D_MODEL), F32)], axis=0)
    mod = _modulation(cond_rows, w_mod, b_mod).reshape(DEPTH, MOD_ROWS, 6, 1, D_MODEL)

    prompt = _Stream(batch, seq, mod_row0=0, per_row_mod=False, latent=False)
    sample = _Stream(dec_batch, dec_seq, mod_row0=1, per_row_mod=True, latent=True)

    y_prompt, caches = _trunk(
        prompt, x_prompt.reshape(batch * seq, D_MODEL), mod, weights, None, None,
        dict(mixer_in=256, attend=256, mix_out=512, ffn=512))
    new_ckv = jnp.stack([ck.reshape(batch, seq, KV_RANK) for ck, _ in caches], axis=1)
    new_krope = jnp.stack([kr.reshape(batch, seq, ROPE_DIM) for _, kr in caches], axis=1)

    rope = _rope_tables(dec_seq)
    ctx = (cache_ckv, jnp.pad(cache_krope, ((0, 0), (0, 0), (0, 0), (0, LANES - ROPE_DIM))))
    y_sample, _ = _trunk(
        sample, x_sample.reshape(dec_batch * dec_seq, D_MODEL), mod, weights, rope, ctx,
        dict(mixer_in=512, attend=512, mix_out=512, ffn=512))
    return (y_prompt.reshape(batch, seq, D_MODEL), y_sample.reshape(dec_batch, dec_seq, D_MODEL),
            new_ckv, new_krope)
```

```python
import functools
import math

import jax
import jax.numpy as jnp
from jax import lax
from jax.experimental import pallas as pl
from jax.experimental.pallas import tpu as pltpu

F32 = jnp.float32
BF16 = jnp.bfloat16

D_MODEL = 1024
DEPTH = 2
GRID_W = 64
CHUNK = 128
N_GROUPS = 4
GROUP_W = 128
CHUNK_W = N_GROUPS * GROUP_W
N_HEADS = 4
QK_NOPE = 128
ROPE_DIM = 64
V_DIM = 128
Q_RANK = 384
KV_RANK = 256
MLA_W = N_HEADS * V_DIM
HEAD_PAD = 256
QK_W = N_HEADS * HEAD_PAD
V_EXT = 2 * V_DIM
VEXT_W = N_HEADS * V_EXT
ROPE_BASE = 10000.0
N_EXPERTS = 8
ALPHA = (2 * DEPTH) ** 0.25
EPS = 1e-6
ATTN_SCALE = math.log2(math.e) / math.sqrt(QK_NOPE + ROPE_DIM)
MOD_ROWS = 16
LANES = 128
VMEM_LIMIT = 56 * 1024 * 1024

SH_A, SC_A, G_A, SH_F, SC_F, G_F = range(6)


def _sigmoid(x):
    return 1.0 / (1.0 + jnp.exp(-x))


def _gelu_tanh(x):
    return 0.5 * x * (1.0 + jnp.tanh(math.sqrt(2.0 / math.pi) * (x + 0.044715 * (x * x * x))))


def _layernorm(y, g, b):
    mu = jnp.mean(y, axis=-1, keepdims=True)
    d = y - mu
    var = jnp.mean(d * d, axis=-1, keepdims=True)
    return d * lax.rsqrt(var + EPS) * g + b


def _rmsnorm(y, g):
    return y * lax.rsqrt(jnp.mean(y * y, axis=-1, keepdims=True) + EPS) * g


def _dot(a, b):
    return jnp.dot(a, b, preferred_element_type=F32)


def _dot_nt(a, b):
    return lax.dot_general(a, b, (((1,), (1,)), ((), ())), preferred_element_type=F32)


def _params(*sem):
    return pltpu.CompilerParams(dimension_semantics=sem, vmem_limit_bytes=VMEM_LIMIT)


def _const_spec(shape):
    nd = len(shape)
    return pl.BlockSpec(shape, lambda *_: (0,) * nd)


class _Stream:
    def __init__(self, batch, seq, mod_row0, per_row_mod, latent):
        self.batch = batch
        self.seq = seq
        self.tokens = batch * seq
        self.mod_row0 = mod_row0
        self.per_row_mod = per_row_mod
        self.latent = latent

    def mod_spec(self, layer, which, tm):
        tiles_per_row = self.seq // tm
        row0, per_row = self.mod_row0, self.per_row_mod

        def index(i):
            r = row0 + (i // tiles_per_row if per_row else 0)
            return (layer, r, which, 0, 0)

        return pl.BlockSpec((None, None, None, 1, D_MODEL), index)


def _mod_body(c_ref, w_ref, b_ref, o_ref):
    a = c_ref[...]
    a = (a * _sigmoid(a)).astype(BF16)
    o_ref[...] = _dot(a, w_ref[...].astype(BF16)) + b_ref[...]


def _modulation(cond_rows, w_mod, b_mod):
    depth, _, width = w_mod.shape
    tn = 1536
    return pl.pallas_call(
        _mod_body,
        grid=(depth, width // tn),
        in_specs=[
            pl.BlockSpec((MOD_ROWS, D_MODEL), lambda l, j: (0, 0)),
            pl.BlockSpec((None, D_MODEL, tn), lambda l, j: (l, 0, j)),
            pl.BlockSpec((None, 1, tn), lambda l, j: (l, 0, j)),
        ],
        out_specs=pl.BlockSpec((None, MOD_ROWS, tn), lambda l, j: (l, 0, j)),
        out_shape=jax.ShapeDtypeStruct((depth, MOD_ROWS, width), F32),
        compiler_params=_params("parallel", "parallel"),
        name="modulation",
    )(cond_rows, w_mod, b_mod.reshape(depth, 1, width))


def _store_keys_values(kv, kr, kcat_ref, vext_ref):
    ones = jnp.ones((kv.shape[0], V_EXT - V_DIM), BF16)
    for hd in range(N_HEADS):
        a = hd * HEAD_PAD
        kcat_ref[:, a:a + QK_NOPE] = kv[:, hd * QK_NOPE:(hd + 1) * QK_NOPE].astype(BF16)
        kcat_ref[:, a + QK_NOPE:a + HEAD_PAD] = kr
        b = hd * V_EXT
        vext_ref[:, b:b + V_DIM] = kv[:, (N_HEADS + hd) * V_DIM:(N_HEADS + hd + 1) * V_DIM].astype(BF16)
        vext_ref[:, b + V_DIM:b + V_EXT] = ones


def _mixer_in_body(latent, tm, *refs):
    it = iter(refs)
    x_ref, sc_ref, sh_ref, win_ref, qg_ref, kvg_ref, wqm_ref = (next(it) for _ in range(7))
    wqs_ref = next(it) if latent else None
    wkv_ref, lng_ref, ws_ref, bs_ref = (next(it) for _ in range(4))
    cos_ref, sin_ref = (next(it), next(it)) if latent else (None, None)
    co_ref, q_ref, kcat_ref, vext_ref = (next(it) for _ in range(4))
    ckv_ref, kr_ref = (None, None) if latent else (next(it), next(it))

    h = (x_ref[...] * (1.0 + sc_ref[...]) + sh_ref[...]).astype(BF16)
    p = _dot(h, win_ref[...])

    for g in range(N_GROUPS):
        cols = slice(g * GROUP_W, (g + 1) * GROUP_W)
        vg = _gelu_tanh(p[:, CHUNK_W + g * GROUP_W:CHUNK_W + (g + 1) * GROUP_W])
        mu = jnp.mean(vg, axis=-1, keepdims=True)
        d = vg - mu
        var = jnp.mean(d * d, axis=-1, keepdims=True)
        vn = (d * lax.rsqrt(var + EPS) * lng_ref[:, cols]).astype(BF16)
        ug = _gelu_tanh(p[:, cols])
        for c in range(tm // CHUNK):
            rows = slice(c * CHUNK, (c + 1) * CHUNK)
            z = _dot(ws_ref[g], vn[rows]) + bs_ref[g]
            co_ref[rows, cols] = (ug[rows] * z).astype(BF16)

    o = 2 * CHUNK_W
    cqn = _rmsnorm(p[:, o:o + Q_RANK], qg_ref[...]).astype(BF16)
    qm = _dot(cqn, wqm_ref[...])
    if latent:
        qs = _dot(cqn, wqs_ref[...])
        cos = cos_ref[...]
        sin = sin_ref[...]
    for hd in range(N_HEADS):
        a = hd * HEAD_PAD
        q_ref[:, a:a + QK_NOPE] = (qm[:, a:a + QK_NOPE] * ATTN_SCALE).astype(BF16)
        qr = qm[:, a + QK_NOPE:a + HEAD_PAD]
        if latent:
            qr = qr * cos + qs[:, hd * LANES:(hd + 1) * LANES] * sin
        q_ref[:, a + QK_NOPE:a + HEAD_PAD] = (qr * ATTN_SCALE).astype(BF16)

    o += Q_RANK
    ckvn = _rmsnorm(p[:, o:o + KV_RANK], kvg_ref[...])
    o += KV_RANK
    kr = p[:, o:o + LANES]
    if not latent:
        ckv_ref[...] = ckvn
        kr_ref[...] = kr[:, :ROPE_DIM]
    else:
        kr = kr * cos + p[:, o + LANES:o + 2 * LANES] * sin
    kr = kr.astype(BF16)
    kv = _dot(ckvn.astype(BF16), wkv_ref[...])
    _store_keys_values(kv, kr, kcat_ref, vext_ref)


def _mixer_in(stream, layer, x, mod, lw, rope, tm):
    latent = stream.latent
    t = stream.tokens
    row = lambda w: pl.BlockSpec((tm, w), lambda i: (i, 0))
    win = lw["w_in"]
    ins = [x, mod, mod, win, lw["q_g"], lw["kv_g"], lw["wq_main"]]
    specs = [row(D_MODEL), stream.mod_spec(layer, SC_A, tm), stream.mod_spec(layer, SH_A, tm),
             _const_spec(win.shape), _const_spec(lw["q_g"].shape), _const_spec(lw["kv_g"].shape),
             _const_spec(lw["wq_main"].shape)]
    if latent:
        ins.append(lw["wq_swap"])
        specs.append(_const_spec(lw["wq_swap"].shape))
    for name in ("wkv", "ln_v_g", "w_s", "b_s"):
        ins.append(lw[name])
        specs.append(_const_spec(lw[name].shape))
    if latent:
        tiles_per_seq = stream.seq // tm
        pos = pl.BlockSpec((tm, LANES), lambda i: (i % tiles_per_seq, 0))
        ins += [rope[0], rope[1]]
        specs += [pos, pos]
    out_shape = [jax.ShapeDtypeStruct((t, CHUNK_W), BF16), jax.ShapeDtypeStruct((t, QK_W), BF16),
                 jax.ShapeDtypeStruct((t, QK_W), BF16), jax.ShapeDtypeStruct((t, VEXT_W), BF16)]
    out_specs = [row(CHUNK_W), row(QK_W), row(QK_W), row(VEXT_W)]
    if not latent:
        out_shape += [jax.ShapeDtypeStruct((t, KV_RANK), F32), jax.ShapeDtypeStruct((t, ROPE_DIM), F32)]
        out_specs += [row(KV_RANK), row(ROPE_DIM)]
    return pl.pallas_call(
        functools.partial(_mixer_in_body, latent, tm),
        grid=(t // tm,),
        in_specs=specs,
        out_specs=out_specs,
        out_shape=out_shape,
        compiler_params=_params("parallel"),
        name="mixer_in",
    )(*ins)


KEY_CHUNK = 256


def _attend_body(has_ctx, *refs):
    if has_ctx:
        q_ref, k_ref, vext_ref, cckv_ref, ckr_ref, wkv_ref, o_ref, kctx_ref, vctx_ref = refs

        @pl.when(pl.program_id(1) == 0)
        def _():
            kv = _dot(cckv_ref[...].astype(BF16), wkv_ref[...])
            _store_keys_values(kv, ckr_ref[...].astype(BF16), kctx_ref, vctx_ref)

        sources = [(kctx_ref, vctx_ref), (k_ref, vext_ref)]
    else:
        q_ref, k_ref, vext_ref, o_ref = refs
        sources = [(k_ref, vext_ref)]
    chunks = [(kr, vr, slice(c * KEY_CHUNK, (c + 1) * KEY_CHUNK))
              for kr, vr in sources for c in range(kr.shape[0] // KEY_CHUNK)]

    def scores(hd):
        qk = slice(hd * HEAD_PAD, (hd + 1) * HEAD_PAD)
        return [_dot_nt(q_ref[:, qk], kr[rows, qk]) for kr, _, rows in chunks]

    s = scores(0)
    for hd in range(N_HEADS):
        m = jnp.max(functools.reduce(jnp.maximum, s), axis=-1, keepdims=True)
        s_next, acc = [], None
        for c, (kr, vr, rows) in enumerate(chunks):
            if hd + 1 < N_HEADS:
                qk = slice((hd + 1) * HEAD_PAD, (hd + 2) * HEAD_PAD)
                s_next.append(_dot_nt(q_ref[:, qk], kr[rows, qk]))
            p = jnp.exp2(s[c] - m).astype(BF16)
            part = _dot(p, vr[rows, hd * V_EXT:(hd + 1) * V_EXT])
            acc = part if acc is None else acc + part
        s = s_next
        o_ref[:, hd * V_DIM:(hd + 1) * V_DIM] = (acc[:, :V_DIM] / acc[:, V_DIM:]).astype(BF16)


def _attend(stream, layer, q, kcat, vext, lw, ctx, tq):
    n = stream.seq
    nq = n // tq
    has_ctx = stream.latent
    ins = [q, kcat, vext]
    specs = [pl.BlockSpec((tq, QK_W), lambda b, i: (b * nq + i, 0)),
             pl.BlockSpec((n, QK_W), lambda b, i: (b, 0)),
             pl.BlockSpec((n, VEXT_W), lambda b, i: (b, 0))]
    scratch = []
    if has_ctx:
        cache_ckv, cache_kr = ctx
        past = cache_ckv.shape[2]
        ins += [cache_ckv, cache_kr, lw["wkv"]]
        specs += [pl.BlockSpec((None, None, past, KV_RANK), lambda b, i: (b, layer, 0, 0)),
                  pl.BlockSpec((None, None, past, LANES), lambda b, i: (b, layer, 0, 0)),
                  _const_spec(lw["wkv"].shape)]
        scratch = [pltpu.VMEM((past, QK_W), BF16), pltpu.VMEM((past, VEXT_W), BF16)]
    return pl.pallas_call(
        functools.partial(_attend_body, has_ctx),
        grid=(stream.batch, nq),
        in_specs=specs,
        out_specs=pl.BlockSpec((tq, MLA_W), lambda b, i: (b * nq + i, 0)),
        out_shape=jax.ShapeDtypeStruct((stream.tokens, MLA_W), BF16),
        scratch_shapes=scratch,
        compiler_params=_params("parallel", "arbitrary"),
        name="attend",
    )(*ins)


def _mix_out_body(route, *refs):
    if route:
        (co_ref, mo_ref, wo1_ref, wo2_ref, x_ref, ga_ref, scf_ref, shf_ref, g_ref, b_ref,
         rh_ref, rl_ref, x1_ref, h2_ref, meta_ref, cnt_ref) = refs
    else:
        (co_ref, mo_ref, wo1_ref, wo2_ref, x_ref, ga_ref, scf_ref, shf_ref, g_ref, b_ref,
         x1_ref, h2_ref) = refs
    mix = _dot(co_ref[...], wo1_ref[...]) + _dot(mo_ref[...], wo2_ref[...])
    x1 = _layernorm(ALPHA * x_ref[...] + ga_ref[...] * mix, g_ref[...], b_ref[...])
    x1_ref[...] = x1
    h2 = x1 * (1.0 + scf_ref[...]) + shf_ref[...]
    hh = h2.astype(BF16)
    h2_ref[...] = hh
    if route:
        hl = (h2 - hh.astype(F32)).astype(BF16)
        logits = _dot(hh, rh_ref[...]) + (_dot(hh, rl_ref[...]) + _dot(hl, rh_ref[...]))
        lane = lax.broadcasted_iota(jnp.int32, logits.shape, 1).astype(F32)
        neg = -jnp.inf
        lg = jnp.where(lane < N_EXPERTS, logits, neg)
        m1 = jnp.max(lg, axis=-1, keepdims=True)
        i1 = jnp.min(jnp.where(lg == m1, lane, float(LANES)), axis=-1, keepdims=True)
        lg2 = jnp.where(lane == i1, neg, lg)
        m2 = jnp.max(lg2, axis=-1, keepdims=True)
        i2 = jnp.min(jnp.where(lg2 == m2, lane, float(LANES)), axis=-1, keepdims=True)
        e2 = jnp.exp(m2 - m1)
        den = 1.0 + e2
        tm = logits.shape[0]
        picked = jnp.where(lane == i1, 1.0, jnp.where(lane == i2, 1.0, 0.0))
        earlier = jnp.where(lax.broadcasted_iota(jnp.int32, (tm, tm), 0)
                            > lax.broadcasted_iota(jnp.int32, (tm, tm), 1), 1.0, 0.0).astype(BF16)
        rank = _dot(earlier, picked.astype(BF16))
        rank1 = jnp.sum(jnp.where(lane == i1, rank, 0.0), axis=-1, keepdims=True)
        rank2 = jnp.sum(jnp.where(lane == i2, rank, 0.0), axis=-1, keepdims=True)
        meta = jnp.zeros_like(logits)
        for k, val in enumerate((i1, i2, 1.0 / den, e2 / den, rank1, rank2)):
            meta = jnp.where(lane == float(k), val, meta)
        meta_ref[...] = meta
        cnt_ref[...] = jnp.sum(picked, axis=0, keepdims=True)


def _mix_out(stream, layer, co, mo, x, mod, lw, route, tm):
    t = stream.tokens
    row = lambda w: pl.BlockSpec((tm, w), lambda i: (i, 0))
    vec = _const_spec((1, D_MODEL))
    ins = [co, mo, lw["wo_chunk"], lw["wo_mla"], x, mod, mod, mod, lw["ln_mix_g"], lw["ln_mix_b"]]
    specs = [row(CHUNK_W), row(MLA_W), _const_spec(lw["wo_chunk"].shape), _const_spec(lw["wo_mla"].shape),
             row(D_MODEL), stream.mod_spec(layer, G_A, tm), stream.mod_spec(layer, SC_F, tm),
             stream.mod_spec(layer, SH_F, tm), vec, vec]
    out_shape = [jax.ShapeDtypeStruct((t, D_MODEL), F32), jax.ShapeDtypeStruct((t, D_MODEL), BF16)]
    out_specs = [row(D_MODEL), row(D_MODEL)]
    if route:
        ins += [lw["router_hi"], lw["router_lo"]]
        specs += [_const_spec(lw["router_hi"].shape), _const_spec(lw["router_lo"].shape)]
        out_shape += [jax.ShapeDtypeStruct((t, LANES), F32), jax.ShapeDtypeStruct((t // tm, 1, LANES), F32)]
        out_specs += [row(LANES), pl.BlockSpec((None, 1, LANES), lambda i: (i, 0, 0))]
    return pl.pallas_call(
        functools.partial(_mix_out_body, route),
        grid=(t // tm,),
        in_specs=specs,
        out_specs=out_specs,
        out_shape=out_shape,
        compiler_params=_params("parallel"),
        name="mix_out",
    )(*ins)


def _ffn_body(h_ref, x_ref, wg_ref, wu_ref, wd_ref, gf_ref, g_ref, b_ref, o_ref):
    h = h_ref[...]
    gate = _dot(h, wg_ref[...])
    up = _dot(h, wu_ref[...])
    a = (gate * _sigmoid(gate) * up).astype(BF16)
    f = _dot(a, wd_ref[...])
    o_ref[...] = _layernorm(ALPHA * x_ref[...] + gf_ref[...] * f, g_ref[...], b_ref[...])


def _ffn(stream, layer, h2, x1, mod, lw, tm):
    t = stream.tokens
    row = lambda w: pl.BlockSpec((tm, w), lambda i: (i, 0))
    vec = _const_spec((1, D_MODEL))
    resident = lambda a: pl.BlockSpec(a.shape, lambda i: (0, 0), pipeline_mode=pl.Buffered(1))
    return pl.pallas_call(
        _ffn_body,
        grid=(t // tm,),
        in_specs=[row(D_MODEL), row(D_MODEL), resident(lw["ffn_gate"]), resident(lw["ffn_up"]),
                  resident(lw["ffn_down"]), stream.mod_spec(layer, G_F, tm), vec, vec],
        out_specs=row(D_MODEL),
        out_shape=jax.ShapeDtypeStruct((t, D_MODEL), F32),
        compiler_params=_params("parallel"),
        name="ffn_dense",
    )(h2, x1, lw["ffn_gate"], lw["ffn_up"], lw["ffn_down"], mod, lw["ln_ffn_g"], lw["ln_ffn_b"])


RUN_ALIGN = 16
EXPERT_TILE = 256


def _route_plan(cnt, tokens, tm):
    nw = tokens // tm
    worst_pad = (RUN_ALIGN - 1) * nw
    cap = pl.cdiv(tokens + worst_pad, EXPERT_TILE) * EXPERT_TILE
    n_tiles = (2 * tokens + N_EXPERTS * worst_pad) // EXPERT_TILE + N_EXPERTS
    cnt = cnt[:, 0, :N_EXPERTS].astype(jnp.int32)
    run = (cnt + RUN_ALIGN - 1) // RUN_ALIGN * RUN_ALIGN
    off = jnp.cumsum(run, axis=1) - run
    base = jnp.arange(N_EXPERTS, dtype=jnp.int32) * cap + jnp.cumsum(run, axis=0) - run
    totals = jnp.sum(run, axis=0)
    tiles = (totals + EXPERT_TILE - 1) // EXPERT_TILE
    cum = jnp.cumsum(tiles)
    n_valid = cum[-1]
    i = jnp.minimum(jnp.arange(n_tiles, dtype=jnp.int32), n_valid - 1)
    te = jnp.sum((cum[None, :] <= i[:, None]).astype(jnp.int32), axis=1)
    tj = i - (cum - tiles)[te]
    return dict(
        tm=tm, nw=nw, cap=cap, n_tiles=n_tiles, buf_rows=2 * tm + N_EXPERTS * RUN_ALIGN,
        off=off.reshape(-1), run=run.reshape(-1), base=base.reshape(-1),
        tile_expert=te, tile_block=te * (cap // EXPERT_TILE) + tj,
        tile_rows=jnp.minimum(EXPERT_TILE, totals[te] - tj * EXPERT_TILE),
        n_valid=n_valid.reshape(1))


def _run_piece_sizes(tm):
    sizes, b = [], RUN_ALIGN
    while b <= tm:
        sizes.append(b)
        b *= 2
    return sizes


def _row_run_copies(src, dst, src_off, dst_off, n_rows, sem, sizes, action):
    for b in sizes:
        done = n_rows & (-2 * b)
        cp = pltpu.make_async_copy(
            src.at[pl.ds(pl.multiple_of(src_off + done, RUN_ALIGN), b)],
            dst.at[pl.ds(pl.multiple_of(dst_off + done, RUN_ALIGN), b)], sem)

        @pl.when((n_rows & b) != 0)
        def _():
            action(cp)


def _local_rows(meta, off_ref, w):
    i1, i2 = meta[:, 0:1], meta[:, 1:2]
    o1, o2 = jnp.zeros_like(i1), jnp.zeros_like(i2)
    for e in range(N_EXPERTS):
        off_e = off_ref[w * N_EXPERTS + e].astype(F32)
        o1 = jnp.where(i1 == float(e), off_e, o1)
        o2 = jnp.where(i2 == float(e), off_e, o2)
    return (o1 + meta[:, 4:5]).astype(jnp.int32), (o2 + meta[:, 5:6]).astype(jnp.int32)


def _start(cp):
    cp.start()


def _wait(cp):
    cp.wait()


def _dispatch_body(tm, nw, off_ref, run_ref, base_ref, h_ref, meta_ref, xs_ref, xw_ref, sems):
    w = pl.program_id(0)
    slot = w % 2
    sizes = _run_piece_sizes(tm)

    def push(win, buf, action):
        for e in range(N_EXPERTS):
            k = win * N_EXPERTS + e
            _row_run_copies(xw_ref.at[buf], xs_ref, off_ref[k], base_ref[k], run_ref[k],
                            sems.at[buf], sizes, action)

    r1, r2 = _local_rows(meta_ref[...], off_ref, w)
    col = lax.broadcasted_iota(jnp.int32, (tm, xw_ref.shape[1]), 1)
    sel = jnp.where(col == r1, 1.0, jnp.where(col == r2, 1.0, 0.0)).astype(BF16)
    xw = lax.dot_general(sel, h_ref[...], (((0,), (0,)), ((), ())), preferred_element_type=F32)
    xw_ref[slot] = xw.astype(BF16)
    push(w, slot, _start)

    @pl.when(w > 0)
    def _():
        push(w - 1, 1 - slot, _wait)

    @pl.when(w == nw - 1)
    def _():
        push(w, slot, _wait)


def _dispatch(h2, meta, plan):
    tm = plan["tm"]
    return pl.pallas_call(
        functools.partial(_dispatch_body, tm, plan["nw"]),
        grid_spec=pltpu.PrefetchScalarGridSpec(
            num_scalar_prefetch=3,
            grid=(plan["nw"],),
            in_specs=[pl.BlockSpec((tm, D_MODEL), lambda i, *_: (i, 0)),
                      pl.BlockSpec((tm, LANES), lambda i, *_: (i, 0))],
            out_specs=pl.BlockSpec(memory_space=pl.ANY),
            scratch_shapes=[pltpu.VMEM((2, plan["buf_rows"], D_MODEL), BF16),
                            pltpu.SemaphoreType.DMA((2,))]),
        out_shape=jax.ShapeDtypeStruct((N_EXPERTS * plan["cap"], D_MODEL), BF16),
        compiler_params=_params("arbitrary"),
        name="moe_dispatch",
    )(plan["off"], plan["run"], plan["base"], h2, meta)


def _experts_body(te_ref, tb_ref, rows_ref, nv_ref, x_ref, wg_ref, wu_ref, wd_ref, o_ref):
    i = pl.program_id(0)

    @pl.when(i < nv_ref[0])
    def _():
        x = x_ref[...]
        live = lax.broadcasted_iota(jnp.int32, (x.shape[0], 1), 0) < rows_ref[i]
        x = jnp.where(live, x, jnp.zeros_like(x))
        gate = _dot(x, wg_ref[...])
        up = _dot(x, wu_ref[...])
        a = (gate * _sigmoid(gate) * up).astype(BF16)
        o_ref[...] = _dot(a, wd_ref[...]).astype(BF16)


def _experts(xs, plan, lw):
    dff = lw["moe_gate"].shape[2]
    rows = pl.BlockSpec((EXPERT_TILE, D_MODEL), lambda i, te, tb, *_: (tb[i], 0))
    return pl.pallas_call(
        _experts_body,
        grid_spec=pltpu.PrefetchScalarGridSpec(
            num_scalar_prefetch=4,
            grid=(plan["n_tiles"],),
            in_specs=[rows,
                      pl.BlockSpec((None, D_MODEL, dff), lambda i, te, *_: (te[i], 0, 0)),
                      pl.BlockSpec((None, D_MODEL, dff), lambda i, te, *_: (te[i], 0, 0)),
                      pl.BlockSpec((None, dff, D_MODEL), lambda i, te, *_: (te[i], 0, 0))],
            out_specs=rows),
        out_shape=jax.ShapeDtypeStruct(xs.shape, BF16),
        compiler_params=_params("arbitrary"),
        name="moe_experts",
    )(plan["tile_expert"], plan["tile_block"], plan["tile_rows"], plan["n_valid"],
      xs, lw["moe_gate"], lw["moe_up"], lw["moe_down"])


def _combine_body(tm, nw, off_ref, run_ref, base_ref, ys_ref, meta_ref, x_ref, gf_ref, g_ref, b_ref,
                  o_ref, yw_ref, sems):
    w = pl.program_id(0)
    slot = w % 2
    sizes = _run_piece_sizes(tm)

    def fetch(win, buf, action):
        for e in range(N_EXPERTS):
            k = win * N_EXPERTS + e
            _row_run_copies(ys_ref, yw_ref.at[buf], base_ref[k], off_ref[k], run_ref[k],
                            sems.at[buf], sizes, action)

    @pl.when(w == 0)
    def _():
        yw_ref[...] = jnp.zeros_like(yw_ref)
        fetch(0, 0, _start)

    @pl.when(w + 1 < nw)
    def _():
        fetch(w + 1, 1 - slot, _start)

    fetch(w, slot, _wait)
    meta = meta_ref[...]
    r1, r2 = _local_rows(meta, off_ref, w)
    col = lax.broadcasted_iota(jnp.int32, (tm, yw_ref.shape[1]), 1)
    yw = yw_ref[slot]
    f = (meta[:, 2:3] * _dot(jnp.where(col == r1, 1.0, 0.0).astype(BF16), yw)
         + meta[:, 3:4] * _dot(jnp.where(col == r2, 1.0, 0.0).astype(BF16), yw))
    o_ref[...] = _layernorm(ALPHA * x_ref[...] + gf_ref[...] * f, g_ref[...], b_ref[...])


def _combine(stream, layer, ys, meta, x1, mod, plan, lw):
    tm = plan["tm"]
    row = lambda width: pl.BlockSpec((tm, width), lambda i, *_: (i, 0))
    vec = pl.BlockSpec((1, D_MODEL), lambda i, *_: (0, 0))
    mod_spec = stream.mod_spec(layer, G_F, tm)
    mod_spec = pl.BlockSpec(mod_spec.block_shape, lambda i, *_, f=mod_spec.index_map: f(i))
    return pl.pallas_call(
        functools.partial(_combine_body, tm, plan["nw"]),
        grid_spec=pltpu.PrefetchScalarGridSpec(
            num_scalar_prefetch=3,
            grid=(plan["nw"],),
            in_specs=[pl.BlockSpec(memory_space=pl.ANY), row(LANES), row(D_MODEL), mod_spec, vec, vec],
            out_specs=row(D_MODEL),
            scratch_shapes=[pltpu.VMEM((2, plan["buf_rows"], D_MODEL), BF16),
                            pltpu.SemaphoreType.DMA((2,))]),
        out_shape=jax.ShapeDtypeStruct((stream.tokens, D_MODEL), F32),
        compiler_params=_params("arbitrary"),
        name="moe_combine",
    )(plan["off"], plan["run"], plan["base"], ys, meta, x1, mod, lw["ln_ffn_g"], lw["ln_ffn_b"])


_ROPE_SWAP = tuple(list(range(16, 32)) + list(range(0, 16)) + list(range(48, 64)) + list(range(32, 48)))


def _rope_tables(n_tokens):
    rows = n_tokens // GRID_W
    row = jnp.repeat(jnp.arange(rows, dtype=F32), GRID_W)
    col = jnp.tile(jnp.arange(GRID_W, dtype=F32), rows)
    half = ROPE_DIM // 2
    inv_freq = ROPE_BASE ** (-jnp.arange(0, half, 2, dtype=F32) / half)
    ar = row[:, None] * inv_freq[None, :]
    ac = col[:, None] * inv_freq[None, :]
    zeros = jnp.zeros((n_tokens, LANES - ROPE_DIM), F32)
    cos = jnp.concatenate([jnp.cos(ar), jnp.cos(ar), jnp.cos(ac), jnp.cos(ac), zeros], axis=-1)
    sin = jnp.concatenate([-jnp.sin(ar), jnp.sin(ar), -jnp.sin(ac), jnp.sin(ac), zeros], axis=-1)
    return cos, sin


def _layer_weights(l, p):
    swap = jnp.array(_ROPE_SWAP)
    w_in = p["w_in"][l]
    o_kr = 2 * CHUNK_W + Q_RANK + KV_RANK
    kr_cols = w_in[:, o_kr:o_kr + ROPE_DIM]
    zpad = jnp.zeros((D_MODEL, LANES - ROPE_DIM), F32)
    w_in_pad = jnp.concatenate([w_in, zpad], axis=1)
    w_in_lat = jnp.concatenate([w_in_pad, kr_cols[:, swap], zpad], axis=1)

    w_uq = p["w_uq"][l].reshape(Q_RANK, N_HEADS, QK_NOPE + ROPE_DIM)
    q_zero = jnp.zeros((Q_RANK, N_HEADS, HEAD_PAD - QK_NOPE - ROPE_DIM), F32)
    wq_main = jnp.concatenate([w_uq, q_zero], axis=-1).reshape(Q_RANK, QK_W)
    wq_swap = jnp.concatenate([w_uq[:, :, QK_NOPE:][:, :, swap], q_zero], axis=-1).reshape(Q_RANK, N_HEADS * LANES)

    w_ukv = p["w_ukv"][l].reshape(KV_RANK, N_HEADS, QK_NOPE + V_DIM)
    wkv = jnp.concatenate([w_ukv[:, :, :QK_NOPE].reshape(KV_RANK, N_HEADS * QK_NOPE),
                           w_ukv[:, :, QK_NOPE:].reshape(KV_RANK, MLA_W)], axis=1)
    w_out = p["w_out"][l]
    lw = {
        "w_in_ctx": w_in_pad.astype(BF16),
        "w_in_lat": w_in_lat.astype(BF16),
        "q_g": p["q_norm_g"][l].reshape(1, Q_RANK),
        "kv_g": p["kv_norm_g"][l].reshape(1, KV_RANK),
        "wq_main": wq_main.astype(BF16),
        "wq_swap": wq_swap.astype(BF16),
        "wkv": wkv.astype(BF16),
        "ln_v_g": p["chunk_ln_g"][l].reshape(1, CHUNK_W),
        "w_s": p["w_spatial"][l].astype(BF16),
        "b_s": jnp.broadcast_to(p["b_spatial"][l][:, :, None], (N_GROUPS, CHUNK, GROUP_W)),
        "wo_chunk": w_out[:CHUNK_W].astype(BF16),
        "wo_mla": w_out[CHUNK_W:].astype(BF16),
        "ln_mix_g": p["ln_mix_g"][l].reshape(1, D_MODEL),
        "ln_mix_b": p["ln_mix_b"][l].reshape(1, D_MODEL),
        "ln_ffn_g": p["ln_ffn_g"][l].reshape(1, D_MODEL),
        "ln_ffn_b": p["ln_ffn_b"][l].reshape(1, D_MODEL),
    }
    if l % 2 == 0:
        lw["ffn_gate"] = p["ffn_w_gate"][l // 2].astype(BF16)
        lw["ffn_up"] = p["ffn_w_up"][l // 2].astype(BF16)
        lw["ffn_down"] = p["ffn_w_down"][l // 2].astype(BF16)
    else:
        r = jnp.pad(p["router_w"][l // 2], ((0, 0), (0, LANES - N_EXPERTS)))
        r_hi = r.astype(BF16)
        lw["router_hi"] = r_hi
        lw["router_lo"] = (r - r_hi.astype(F32)).astype(BF16)
        lw["moe_gate"] = p["moe_w_gate"][l // 2].astype(BF16)
        lw["moe_up"] = p["moe_w_up"][l // 2].astype(BF16)
        lw["moe_down"] = p["moe_w_down"][l // 2].astype(BF16)
    return lw


def _trunk(stream, x, mod, weights, rope, ctx, tiles):
    caches = []
    for l in range(DEPTH):
        lw = dict(weights[l])
        lw["w_in"] = lw["w_in_lat"] if stream.latent else lw["w_in_ctx"]
        outs = _mixer_in(stream, l, x, mod, lw, rope, tiles["mixer_in"])
        co, q, kcat, v = outs[:4]
        if not stream.latent:
            caches.append(outs[4:])
        mo = _attend(stream, l, q, kcat, v, lw, ctx, tiles["attend"])
        route = l % 2 == 1
        outs = _mix_out(stream, l, co, mo, x, mod, lw, route, tiles["mix_out"])
        if route:
            x1, h2, meta, cnt = outs
            plan = _route_plan(cnt, stream.tokens, tiles["mix_out"])
            ys = _experts(_dispatch(h2, meta, plan), plan, lw)
            x = _combine(stream, l, ys, meta, x1, mod, plan, lw)
        else:
            x = _ffn(stream, l, outs[1], outs[0], mod, lw, tiles["ffn"])
    return x, caches


def kernel(x_prompt, x_sample, c, cache_ckv, cache_krope, c_ctx, w_mod, b_mod, w_in, q_norm_g, kv_norm_g, w_uq, w_ukv, chunk_ln_g, w_spatial, b_spatial, w_out, ln_mix_g, ln_mix_b, ln_ffn_g, ln_ffn_b, ffn_w_gate, ffn_w_up, ffn_w_down, router_w, moe_w_gate, moe_w_up, moe_w_down):
    p = dict(w_in=w_in, q_norm_g=q_norm_g, kv_norm_g=kv_norm_g, w_uq=w_uq, w_ukv=w_ukv,
             chunk_ln_g=chunk_ln_g, w_spatial=w_spatial, b_spatial=b_spatial, w_out=w_out,
             ln_mix_g=ln_mix_g, ln_mix_b=ln_mix_b, ln_ffn_g=ln_ffn_g, ln_ffn_b=ln_ffn_b,
             ffn_w_gate=ffn_w_gate, ffn_w_up=ffn_w_up, ffn_w_down=ffn_w_down, router_w=router_w,
             moe_w_gate=moe_w_gate, moe_w_up=moe_w_up, moe_w_down=moe_w_down)
    weights = [_layer_weights(l, p) for l in range(DEPTH)]

    batch, seq, _ = x_prompt.shape
    dec_batch, dec_seq, _ = x_sample.shape
    cond_rows = jnp.concatenate(
        [c_ctx[None, :], c, jnp.zeros((MOD_ROWS - 1 - dec_batch, D_MODEL), F32)], axis=0)
    mod = _modulation(cond_rows, w_mod, b_mod).reshape(DEPTH, MOD_ROWS, 6, 1, D_MODEL)

    prompt = _Stream(batch, seq, mod_row0=0, per_row_mod=False, latent=False)
    sample = _Stream(dec_batch, dec_seq, mod_row0=1, per_row_mod=True, latent=True)

    y_prompt, caches = _trunk(
        prompt, x_prompt.reshape(batch * seq, D_MODEL), mod, weights, None, None,
        dict(mixer_in=256, attend=256, mix_out=512, ffn=512))
    new_ckv = jnp.stack([ck.reshape(batch, seq, KV_RANK) for ck, _ in caches], axis=1)
    new_krope = jnp.stack([kr.reshape(batch, seq, ROPE_DIM) for _, kr in caches], axis=1)

    rope = _rope_tables(dec_seq)
    ctx = (cache_ckv, jnp.pad(cache_krope, ((0, 0), (0, 0), (0, 0), (0, LANES - ROPE_DIM))))
    y_sample, _ = _trunk(
        sample, x_sample.reshape(dec_batch * dec_seq, D_MODEL), mod, weights, rope, ctx,
        dict(mixer_in=512, attend=512, mix_out=512, ffn=512))
    return (y_prompt.reshape(batch, seq, D_MODEL), y_sample.reshape(dec_batch, dec_seq, D_MODEL),
            new_ckv, new_krope)
```

```python
import functools
import math

import jax
import jax.numpy as jnp
import numpy as np
from jax import lax
from jax.experimental import pallas as pl
from jax.experimental.pallas import tpu as pltpu

F32 = jnp.float32
BF16 = jnp.bfloat16

D_MODEL = 1024
DEPTH = 2
GRID_W = 64
CHUNK = 128
N_GROUPS = 4
GROUP_W = 128
CHUNK_W = N_GROUPS * GROUP_W
N_HEADS = 4
QK_NOPE = 128
ROPE_DIM = 64
V_DIM = 128
Q_RANK = 384
KV_RANK = 256
MLA_W = N_HEADS * V_DIM
HEAD_PAD = 256
QK_W = N_HEADS * HEAD_PAD
V_EXT = 2 * V_DIM
VEXT_W = N_HEADS * V_EXT
ROPE_BASE = 10000.0
N_EXPERTS = 8
ALPHA = (2 * DEPTH) ** 0.25
EPS = 1e-6
ATTN_SCALE = math.log2(math.e) / math.sqrt(QK_NOPE + ROPE_DIM)
MOD_ROWS = 16
LANES = 128
VMEM_LIMIT = 56 * 1024 * 1024

SH_A, SC_A, G_A, SH_F, SC_F, G_F = range(6)


def _sigmoid(x):
    return 1.0 / (1.0 + jnp.exp(-x))


def _gelu_tanh(x):
    return 0.5 * x * (1.0 + jnp.tanh(math.sqrt(2.0 / math.pi) * (x + 0.044715 * (x * x * x))))


def _layernorm(y, g, b):
    mu = jnp.mean(y, axis=-1, keepdims=True)
    d = y - mu
    var = jnp.mean(d * d, axis=-1, keepdims=True)
    return d * lax.rsqrt(var + EPS) * g + b


def _rmsnorm(y, g):
    return y * lax.rsqrt(jnp.mean(y * y, axis=-1, keepdims=True) + EPS) * g


def _dot(a, b):
    return jnp.dot(a, b, preferred_element_type=F32)


def _dot_nt(a, b):
    return lax.dot_general(a, b, (((1,), (1,)), ((), ())), preferred_element_type=F32)


def _params(*sem):
    return pltpu.CompilerParams(dimension_semantics=sem, vmem_limit_bytes=VMEM_LIMIT)


def _const_spec(shape):
    nd = len(shape)
    return pl.BlockSpec(shape, lambda *_: (0,) * nd)


class _Stream:
    def __init__(self, batch, seq, mod_row0, per_row_mod, latent):
        self.batch = batch
        self.seq = seq
        self.tokens = batch * seq
        self.mod_row0 = mod_row0
        self.per_row_mod = per_row_mod
        self.latent = latent

    def mod_spec(self, layer, which, tm):
        tiles_per_row = self.seq // tm
        row0, per_row = self.mod_row0, self.per_row_mod

        def index(i):
            r = row0 + (i // tiles_per_row if per_row else 0)
            return (layer, r, which, 0, 0)

        return pl.BlockSpec((None, None, None, 1, D_MODEL), index)


def _mod_body(c_ref, w_ref, b_ref, o_ref):
    a = c_ref[...]
    a = (a * _sigmoid(a)).astype(BF16)
    o_ref[...] = _dot(a, w_ref[...].astype(BF16)) + b_ref[...]


def _modulation(cond_rows, w_mod, b_mod):
    depth, _, width = w_mod.shape
    tn = 1536
    return pl.pallas_call(
        _mod_body,
        grid=(depth, width // tn),
        in_specs=[
            pl.BlockSpec((MOD_ROWS, D_MODEL), lambda l, j: (0, 0)),
            pl.BlockSpec((None, D_MODEL, tn), lambda l, j: (l, 0, j)),
            pl.BlockSpec((None, 1, tn), lambda l, j: (l, 0, j)),
        ],
        out_specs=pl.BlockSpec((None, MOD_ROWS, tn), lambda l, j: (l, 0, j)),
        out_shape=jax.ShapeDtypeStruct((depth, MOD_ROWS, width), F32),
        compiler_params=_params("parallel", "parallel"),
        name="modulation",
    )(cond_rows, w_mod, b_mod.reshape(depth, 1, width))


def _store_keys_values(kv, kr, kcat_ref, vext_ref):
    ones = jnp.ones((kv.shape[0], V_EXT - V_DIM), BF16)
    for hd in range(N_HEADS):
        a = hd * HEAD_PAD
        kcat_ref[:, a:a + QK_NOPE] = kv[:, hd * QK_NOPE:(hd + 1) * QK_NOPE].astype(BF16)
        kcat_ref[:, a + QK_NOPE:a + HEAD_PAD] = kr
        b = hd * V_EXT
        vext_ref[:, b:b + V_DIM] = kv[:, (N_HEADS + hd) * V_DIM:(N_HEADS + hd + 1) * V_DIM].astype(BF16)
        vext_ref[:, b + V_DIM:b + V_EXT] = ones


def _mixer_in_body(latent, tm, *refs):
    it = iter(refs)
    x_ref, sc_ref, sh_ref, win_ref, qg_ref, kvg_ref, wqm_ref = (next(it) for _ in range(7))
    wqs_ref = next(it) if latent else None
    wkv_ref, lng_ref, ws_ref, bs_ref = (next(it) for _ in range(4))
    cos_ref, sin_ref = (next(it), next(it)) if latent else (None, None)
    co_ref, q_ref, kcat_ref, vext_ref = (next(it) for _ in range(4))
    ckv_ref, kr_ref = (None, None) if latent else (next(it), next(it))

    h = (x_ref[...] * (1.0 + sc_ref[...]) + sh_ref[...]).astype(BF16)
    p = _dot(h, win_ref[...])

    for g in range(N_GROUPS):
        cols = slice(g * GROUP_W, (g + 1) * GROUP_W)
        vg = _gelu_tanh(p[:, CHUNK_W + g * GROUP_W:CHUNK_W + (g + 1) * GROUP_W])
        mu = jnp.mean(vg, axis=-1, keepdims=True)
        d = vg - mu
        var = jnp.mean(d * d, axis=-1, keepdims=True)
        vn = (d * lax.rsqrt(var + EPS) * lng_ref[:, cols]).astype(BF16)
        ug = _gelu_tanh(p[:, cols])
        for c in range(tm // CHUNK):
            rows = slice(c * CHUNK, (c + 1) * CHUNK)
            z = _dot(ws_ref[g], vn[rows]) + bs_ref[g]
            co_ref[rows, cols] = (ug[rows] * z).astype(BF16)

    o = 2 * CHUNK_W
    cqn = _rmsnorm(p[:, o:o + Q_RANK], qg_ref[...]).astype(BF16)
    qm = _dot(cqn, wqm_ref[...])
    if latent:
        qs = _dot(cqn, wqs_ref[...])
        cos = cos_ref[...]
        sin = sin_ref[...]
    for hd in range(N_HEADS):
        a = hd * HEAD_PAD
        q_ref[:, a:a + QK_NOPE] = (qm[:, a:a + QK_NOPE] * ATTN_SCALE).astype(BF16)
        qr = qm[:, a + QK_NOPE:a + HEAD_PAD]
        if latent:
            qr = qr * cos + qs[:, hd * LANES:(hd + 1) * LANES] * sin
        q_ref[:, a + QK_NOPE:a + HEAD_PAD] = (qr * ATTN_SCALE).astype(BF16)

    o += Q_RANK
    ckvn = _rmsnorm(p[:, o:o + KV_RANK], kvg_ref[...])
    o += KV_RANK
    kr = p[:, o:o + LANES]
    if not latent:
        ckv_ref[...] = ckvn
        kr_ref[...] = kr[:, :ROPE_DIM]
    else:
        kr = kr * cos + p[:, o + LANES:o + 2 * LANES] * sin
    kr = kr.astype(BF16)
    kv = _dot(ckvn.astype(BF16), wkv_ref[...])
    _store_keys_values(kv, kr, kcat_ref, vext_ref)


def _mixer_in(stream, layer, x, mod, lw, rope, tm):
    latent = stream.latent
    t = stream.tokens
    row = lambda w: pl.BlockSpec((tm, w), lambda i: (i, 0))
    win = lw["w_in"]
    ins = [x, mod, mod, win, lw["q_g"], lw["kv_g"], lw["wq_main"]]
    specs = [row(D_MODEL), stream.mod_spec(layer, SC_A, tm), stream.mod_spec(layer, SH_A, tm),
             _const_spec(win.shape), _const_spec(lw["q_g"].shape), _const_spec(lw["kv_g"].shape),
             _const_spec(lw["wq_main"].shape)]
    if latent:
        ins.append(lw["wq_swap"])
        specs.append(_const_spec(lw["wq_swap"].shape))
    for name in ("wkv", "ln_v_g", "w_s", "b_s"):
        ins.append(lw[name])
        specs.append(_const_spec(lw[name].shape))
    if latent:
        tiles_per_seq = stream.seq // tm
        pos = pl.BlockSpec((tm, LANES), lambda i: (i % tiles_per_seq, 0))
        ins += [rope[0], rope[1]]
        specs += [pos, pos]
    out_shape = [jax.ShapeDtypeStruct((t, CHUNK_W), BF16), jax.ShapeDtypeStruct((t, QK_W), BF16),
                 jax.ShapeDtypeStruct((t, QK_W), BF16), jax.ShapeDtypeStruct((t, VEXT_W), BF16)]
    out_specs = [row(CHUNK_W), row(QK_W), row(QK_W), row(VEXT_W)]
    if not latent:
        out_shape += [jax.ShapeDtypeStruct((t, KV_RANK), F32), jax.ShapeDtypeStruct((t, ROPE_DIM), F32)]
        out_specs += [row(KV_RANK), row(ROPE_DIM)]
    return pl.pallas_call(
        functools.partial(_mixer_in_body, latent, tm),
        grid=(t // tm,),
        in_specs=specs,
        out_specs=out_specs,
        out_shape=out_shape,
        compiler_params=_params("parallel"),
        name="mixer_in",
    )(*ins)


KEY_CHUNK = 256


def _attend_body(has_ctx, *refs):
    if has_ctx:
        q_ref, k_ref, vext_ref, cckv_ref, ckr_ref, wkv_ref, o_ref, kctx_ref, vctx_ref = refs

        @pl.when(pl.program_id(1) == 0)
        def _():
            kv = _dot(cckv_ref[...].astype(BF16), wkv_ref[...])
            _store_keys_values(kv, ckr_ref[...].astype(BF16), kctx_ref, vctx_ref)

        sources = [(kctx_ref, vctx_ref), (k_ref, vext_ref)]
    else:
        q_ref, k_ref, vext_ref, o_ref = refs
        sources = [(k_ref, vext_ref)]
    chunks = [(kr, vr, slice(c * KEY_CHUNK, (c + 1) * KEY_CHUNK))
              for kr, vr in sources for c in range(kr.shape[0] // KEY_CHUNK)]

    def scores(hd):
        qk = slice(hd * HEAD_PAD, (hd + 1) * HEAD_PAD)
        return [_dot_nt(q_ref[:, qk], kr[rows, qk]) for kr, _, rows in chunks]

    s = scores(0)
    for hd in range(N_HEADS):
        m = jnp.max(functools.reduce(jnp.maximum, s), axis=-1, keepdims=True)
        s_next, acc = [], None
        for c, (kr, vr, rows) in enumerate(chunks):
            if hd + 1 < N_HEADS:
                qk = slice((hd + 1) * HEAD_PAD, (hd + 2) * HEAD_PAD)
                s_next.append(_dot_nt(q_ref[:, qk], kr[rows, qk]))
            p = jnp.exp2(s[c] - m).astype(BF16)
            part = _dot(p, vr[rows, hd * V_EXT:(hd + 1) * V_EXT])
            acc = part if acc is None else acc + part
        s = s_next
        o_ref[:, hd * V_DIM:(hd + 1) * V_DIM] = (acc[:, :V_DIM] / acc[:, V_DIM:]).astype(BF16)


def _attend(stream, layer, q, kcat, vext, lw, ctx, tq):
    n = stream.seq
    nq = n // tq
    has_ctx = stream.latent
    ins = [q, kcat, vext]
    specs = [pl.BlockSpec((tq, QK_W), lambda b, i: (b * nq + i, 0)),
             pl.BlockSpec((n, QK_W), lambda b, i: (b, 0)),
             pl.BlockSpec((n, VEXT_W), lambda b, i: (b, 0))]
    scratch = []
    if has_ctx:
        cache_ckv, cache_kr = ctx
        past = cache_ckv.shape[2]
        ins += [cache_ckv, cache_kr, lw["wkv"]]
        specs += [pl.BlockSpec((None, None, past, KV_RANK), lambda b, i: (b, layer, 0, 0)),
                  pl.BlockSpec((None, None, past, LANES), lambda b, i: (b, layer, 0, 0)),
                  _const_spec(lw["wkv"].shape)]
        scratch = [pltpu.VMEM((past, QK_W), BF16), pltpu.VMEM((past, VEXT_W), BF16)]
    return pl.pallas_call(
        functools.partial(_attend_body, has_ctx),
        grid=(stream.batch, nq),
        in_specs=specs,
        out_specs=pl.BlockSpec((tq, MLA_W), lambda b, i: (b * nq + i, 0)),
        out_shape=jax.ShapeDtypeStruct((stream.tokens, MLA_W), BF16),
        scratch_shapes=scratch,
        compiler_params=_params("parallel", "arbitrary"),
        name="attend",
    )(*ins)


MIX_PARTS = 4


def _mix_out_body(route, *refs):
    if route:
        (co_ref, mo_ref, wo1_ref, wo2_ref, x_ref, ga_ref, scf_ref, shf_ref, g_ref, b_ref,
         rcat_ref, x1_ref, h2_ref, meta_ref, cnt_ref) = refs
    else:
        (co_ref, mo_ref, wo1_ref, wo2_ref, x_ref, ga_ref, scf_ref, shf_ref, g_ref, b_ref,
         x1_ref, h2_ref) = refs
    rp = x_ref.shape[0] // MIX_PARTS
    parts = [slice(k * rp, (k + 1) * rp) for k in range(MIX_PARTS)]
    mix = [_dot(co_ref[r], wo1_ref[...]) + _dot(mo_ref[r], wo2_ref[...]) for r in parts]
    x1 = [_layernorm(ALPHA * x_ref[r] + ga_ref[...] * mix[k], g_ref[...], b_ref[...])
          for k, r in enumerate(parts)]
    h2 = [v * (1.0 + scf_ref[...]) + shf_ref[...] for v in x1]
    hh = [v.astype(BF16) for v in h2]
    for k, r in enumerate(parts):
        x1_ref[r] = x1[k]
        h2_ref[r] = hh[k]
    if not route:
        return
    hl = [(h2[k] - hh[k].astype(F32)).astype(BF16) for k in range(MIX_PARTS)]
    ra = [_dot(v, rcat_ref[...]) for v in hh]
    rb = [_dot(v, rcat_ref[...]) for v in hl]
    logits = [ra[k][:, :LANES] + (ra[k][:, LANES:] + (rb[k][:, :LANES] + rb[k][:, LANES:]))
              for k in range(MIX_PARTS)]
    lane = lax.broadcasted_iota(jnp.int32, (rp, LANES), 1).astype(F32)
    neg = -jnp.inf
    lg = [jnp.where(lane < N_EXPERTS, v, neg) for v in logits]
    m1 = [jnp.max(v, axis=-1, keepdims=True) for v in lg]
    i1 = [jnp.min(jnp.where(lg[k] == m1[k], lane, float(LANES)), axis=-1, keepdims=True)
          for k in range(MIX_PARTS)]
    lg2 = [jnp.where(lane == i1[k], neg, lg[k]) for k in range(MIX_PARTS)]
    m2 = [jnp.max(v, axis=-1, keepdims=True) for v in lg2]
    i2 = [jnp.min(jnp.where(lg2[k] == m2[k], lane, float(LANES)), axis=-1, keepdims=True)
          for k in range(MIX_PARTS)]
    picked = [jnp.where(lane == i1[k], 1.0, jnp.where(lane == i2[k], 1.0, 0.0)) for k in range(MIX_PARTS)]
    earlier = jnp.where(lax.broadcasted_iota(jnp.int32, (rp, rp), 0)
                        > lax.broadcasted_iota(jnp.int32, (rp, rp), 1), 1.0, 0.0).astype(BF16)
    before = jnp.zeros((1, LANES), F32)
    for k, r in enumerate(parts):
        rank = _dot(earlier, picked[k].astype(BF16)) + before
        before = before + jnp.sum(picked[k], axis=0, keepdims=True)
        rank1 = jnp.sum(jnp.where(lane == i1[k], rank, 0.0), axis=-1, keepdims=True)
        rank2 = jnp.sum(jnp.where(lane == i2[k], rank, 0.0), axis=-1, keepdims=True)
        e2 = jnp.exp(m2[k] - m1[k])
        den = 1.0 + e2
        meta = jnp.zeros((rp, LANES), F32)
        for j, val in enumerate((i1[k], i2[k], 1.0 / den, e2 / den, rank1, rank2)):
            meta = jnp.where(lane == float(j), val, meta)
        meta_ref[r] = meta
    cnt_ref[...] = before


def _mix_out(stream, layer, co, mo, x, mod, lw, route, tm):
    t = stream.tokens
    row = lambda w: pl.BlockSpec((tm, w), lambda i: (i, 0))
    vec = _const_spec((1, D_MODEL))
    ins = [co, mo, lw["wo_chunk"], lw["wo_mla"], x, mod, mod, mod, lw["ln_mix_g"], lw["ln_mix_b"]]
    specs = [row(CHUNK_W), row(MLA_W), _const_spec(lw["wo_chunk"].shape), _const_spec(lw["wo_mla"].shape),
             row(D_MODEL), stream.mod_spec(layer, G_A, tm), stream.mod_spec(layer, SC_F, tm),
             stream.mod_spec(layer, SH_F, tm), vec, vec]
    out_shape = [jax.ShapeDtypeStruct((t, D_MODEL), F32), jax.ShapeDtypeStruct((t, D_MODEL), BF16)]
    out_specs = [row(D_MODEL), row(D_MODEL)]
    if route:
        ins.append(lw["router"])
        specs.append(_const_spec(lw["router"].shape))
        out_shape += [jax.ShapeDtypeStruct((t, LANES), F32), jax.ShapeDtypeStruct((t // tm, 1, LANES), F32)]
        out_specs += [row(LANES), pl.BlockSpec((None, 1, LANES), lambda i: (i, 0, 0))]
    return pl.pallas_call(
        functools.partial(_mix_out_body, route),
        grid=(t // tm,),
        in_specs=specs,
        out_specs=out_specs,
        out_shape=out_shape,
        compiler_params=_params("parallel"),
        name="mix_out",
    )(*ins)


def _ffn_body(h_ref, x_ref, wg_ref, wu_ref, wd_ref, gf_ref, g_ref, b_ref, o_ref):
    h = h_ref[...]
    gate = _dot(h, wg_ref[...])
    up = _dot(h, wu_ref[...])
    a = (gate * _sigmoid(gate) * up).astype(BF16)
    f = _dot(a, wd_ref[...])
    o_ref[...] = _layernorm(ALPHA * x_ref[...] + gf_ref[...] * f, g_ref[...], b_ref[...])


def _ffn(stream, layer, h2, x1, mod, lw, tm):
    t = stream.tokens
    row = lambda w: pl.BlockSpec((tm, w), lambda i: (i, 0))
    vec = _const_spec((1, D_MODEL))
    resident = lambda a: pl.BlockSpec(a.shape, lambda i: (0, 0), pipeline_mode=pl.Buffered(1))
    return pl.pallas_call(
        _ffn_body,
        grid=(t // tm,),
        in_specs=[row(D_MODEL), row(D_MODEL), resident(lw["ffn_gate"]), resident(lw["ffn_up"]),
                  resident(lw["ffn_down"]), stream.mod_spec(layer, G_F, tm), vec, vec],
        out_specs=row(D_MODEL),
        out_shape=jax.ShapeDtypeStruct((t, D_MODEL), F32),
        compiler_params=_params("parallel"),
        name="ffn_dense",
    )(h2, x1, lw["ffn_gate"], lw["ffn_up"], lw["ffn_down"], mod, lw["ln_ffn_g"], lw["ln_ffn_b"])


RUN_ALIGN = 16
EXPERT_TILE = 256


def _route_plan(cnt, tokens, tm):
    nw = tokens // tm
    worst_pad = (RUN_ALIGN - 1) * nw
    cap = pl.cdiv(tokens + worst_pad, EXPERT_TILE) * EXPERT_TILE
    n_tiles = (2 * tokens + N_EXPERTS * worst_pad) // EXPERT_TILE + N_EXPERTS
    cnt = cnt[:, 0, :N_EXPERTS].astype(jnp.int32)
    run = (cnt + RUN_ALIGN - 1) // RUN_ALIGN * RUN_ALIGN
    off = jnp.cumsum(run, axis=1) - run
    base = jnp.arange(N_EXPERTS, dtype=jnp.int32) * cap + jnp.cumsum(run, axis=0) - run
    totals = jnp.sum(run, axis=0)
    tiles = (totals + EXPERT_TILE - 1) // EXPERT_TILE
    cum = jnp.cumsum(tiles)
    n_valid = cum[-1]
    i = jnp.minimum(jnp.arange(n_tiles, dtype=jnp.int32), n_valid - 1)
    te = jnp.sum((cum[None, :] <= i[:, None]).astype(jnp.int32), axis=1)
    tj = i - (cum - tiles)[te]
    return dict(
        tm=tm, nw=nw, cap=cap, n_tiles=n_tiles, buf_rows=2 * tm + N_EXPERTS * RUN_ALIGN,
        off=off.reshape(-1), run=run.reshape(-1), base=base.reshape(-1),
        tile_expert=te, tile_block=te * (cap // EXPERT_TILE) + tj,
        tile_rows=jnp.minimum(EXPERT_TILE, totals[te] - tj * EXPERT_TILE),
        n_valid=n_valid.reshape(1))


def _run_piece_sizes(tm):
    sizes, b = [], RUN_ALIGN
    while b <= tm:
        sizes.append(b)
        b *= 2
    return sizes


def _row_run_copies(src, dst, src_off, dst_off, n_rows, sem, sizes, action):
    for b in sizes:
        done = n_rows & (-2 * b)
        cp = pltpu.make_async_copy(
            src.at[pl.ds(pl.multiple_of(src_off + done, RUN_ALIGN), b)],
            dst.at[pl.ds(pl.multiple_of(dst_off + done, RUN_ALIGN), b)], sem)

        @pl.when((n_rows & b) != 0)
        def _():
            action(cp)


def _local_rows(meta, off_ref, w):
    i1, i2 = meta[:, 0:1], meta[:, 1:2]
    o1, o2 = jnp.zeros_like(i1), jnp.zeros_like(i2)
    for e in range(N_EXPERTS):
        off_e = off_ref[w * N_EXPERTS + e].astype(F32)
        o1 = jnp.where(i1 == float(e), off_e, o1)
        o2 = jnp.where(i2 == float(e), off_e, o2)
    return (o1 + meta[:, 4:5]).astype(jnp.int32), (o2 + meta[:, 5:6]).astype(jnp.int32)


def _start(cp):
    cp.start()


def _wait(cp):
    cp.wait()


def _dispatch_body(tm, nw, off_ref, run_ref, base_ref, h_ref, meta_ref, xs_ref, xw_ref, sems):
    w = pl.program_id(0)
    slot = w % 2
    sizes = _run_piece_sizes(tm)

    def push(win, buf, action):
        for e in range(N_EXPERTS):
            k = win * N_EXPERTS + e
            _row_run_copies(xw_ref.at[buf], xs_ref, off_ref[k], base_ref[k], run_ref[k],
                            sems.at[buf], sizes, action)

    r1, r2 = _local_rows(meta_ref[...], off_ref, w)
    col = lax.broadcasted_iota(jnp.int32, (tm, xw_ref.shape[1]), 1)
    sel = jnp.where(col == r1, 1.0, jnp.where(col == r2, 1.0, 0.0)).astype(BF16)
    xw = lax.dot_general(sel, h_ref[...], (((0,), (0,)), ((), ())), preferred_element_type=F32)
    xw_ref[slot] = xw.astype(BF16)
    push(w, slot, _start)

    @pl.when(w > 0)
    def _():
        push(w - 1, 1 - slot, _wait)

    @pl.when(w == nw - 1)
    def _():
        push(w, slot, _wait)


def _dispatch(h2, meta, plan):
    tm = plan["tm"]
    return pl.pallas_call(
        functools.partial(_dispatch_body, tm, plan["nw"]),
        grid_spec=pltpu.PrefetchScalarGridSpec(
            num_scalar_prefetch=3,
            grid=(plan["nw"],),
            in_specs=[pl.BlockSpec((tm, D_MODEL), lambda i, *_: (i, 0)),
                      pl.BlockSpec((tm, LANES), lambda i, *_: (i, 0))],
            out_specs=pl.BlockSpec(memory_space=pl.ANY),
            scratch_shapes=[pltpu.VMEM((2, plan["buf_rows"], D_MODEL), BF16),
                            pltpu.SemaphoreType.DMA((2,))]),
        out_shape=jax.ShapeDtypeStruct((N_EXPERTS * plan["cap"], D_MODEL), BF16),
        compiler_params=_params("arbitrary"),
        name="moe_dispatch",
    )(plan["off"], plan["run"], plan["base"], h2, meta)


def _experts_body(te_ref, tb_ref, rows_ref, nv_ref, x_ref, wg_ref, wu_ref, wd_ref, o_ref):
    i = pl.program_id(0)

    @pl.when(i < nv_ref[0])
    def _():
        x = x_ref[...]
        live = lax.broadcasted_iota(jnp.int32, (x.shape[0], 1), 0) < rows_ref[i]
        x = jnp.where(live, x, jnp.zeros_like(x))
        gate = _dot(x, wg_ref[...])
        up = _dot(x, wu_ref[...])
        a = (gate * _sigmoid(gate) * up).astype(BF16)
        o_ref[...] = _dot(a, wd_ref[...]).astype(BF16)


def _experts(xs, plan, lw):
    dff = lw["moe_gate"].shape[2]
    rows = pl.BlockSpec((EXPERT_TILE, D_MODEL), lambda i, te, tb, *_: (tb[i], 0))
    return pl.pallas_call(
        _experts_body,
        grid_spec=pltpu.PrefetchScalarGridSpec(
            num_scalar_prefetch=4,
            grid=(plan["n_tiles"],),
            in_specs=[rows,
                      pl.BlockSpec((None, D_MODEL, dff), lambda i, te, *_: (te[i], 0, 0)),
                      pl.BlockSpec((None, D_MODEL, dff), lambda i, te, *_: (te[i], 0, 0)),
                      pl.BlockSpec((None, dff, D_MODEL), lambda i, te, *_: (te[i], 0, 0))],
            out_specs=rows),
        out_shape=jax.ShapeDtypeStruct(xs.shape, BF16),
        compiler_params=_params("arbitrary"),
        name="moe_experts",
    )(plan["tile_expert"], plan["tile_block"], plan["tile_rows"], plan["n_valid"],
      xs, lw["moe_gate"], lw["moe_up"], lw["moe_down"])


def _combine_body(tm, nw, off_ref, run_ref, base_ref, ys_ref, meta_ref, x_ref, gf_ref, g_ref, b_ref,
                  o_ref, yw_ref, sems):
    w = pl.program_id(0)
    slot = w % 2
    sizes = _run_piece_sizes(tm)

    def fetch(win, buf, action):
        for e in range(N_EXPERTS):
            k = win * N_EXPERTS + e
            _row_run_copies(ys_ref, yw_ref.at[buf], base_ref[k], off_ref[k], run_ref[k],
                            sems.at[buf], sizes, action)

    @pl.when(w == 0)
    def _():
        yw_ref[...] = jnp.zeros_like(yw_ref)
        fetch(0, 0, _start)

    @pl.when(w + 1 < nw)
    def _():
        fetch(w + 1, 1 - slot, _start)

    fetch(w, slot, _wait)
    rp = tm // MIX_PARTS
    parts = [slice(k * rp, (k + 1) * rp) for k in range(MIX_PARTS)]
    meta = [meta_ref[r] for r in parts]
    picks = [_local_rows(v, off_ref, w) for v in meta]
    col = lax.broadcasted_iota(jnp.int32, (rp, yw_ref.shape[1]), 1)
    yw = yw_ref[slot]
    gate = [jnp.where(col == r1, v[:, 2:3], jnp.where(col == r2, v[:, 3:4], 0.0)).astype(BF16)
            for v, (r1, r2) in zip(meta, picks)]
    f = [_dot(g, yw) for g in gate]
    for k, r in enumerate(parts):
        o_ref[r] = _layernorm(ALPHA * x_ref[r] + gf_ref[...] * f[k], g_ref[...], b_ref[...])


def _combine(stream, layer, ys, meta, x1, mod, plan, lw):
    tm = plan["tm"]
    row = lambda width: pl.BlockSpec((tm, width), lambda i, *_: (i, 0))
    vec = pl.BlockSpec((1, D_MODEL), lambda i, *_: (0, 0))
    mod_spec = stream.mod_spec(layer, G_F, tm)
    mod_spec = pl.BlockSpec(mod_spec.block_shape, lambda i, *_, f=mod_spec.index_map: f(i))
    return pl.pallas_call(
        functools.partial(_combine_body, tm, plan["nw"]),
        grid_spec=pltpu.PrefetchScalarGridSpec(
            num_scalar_prefetch=3,
            grid=(plan["nw"],),
            in_specs=[pl.BlockSpec(memory_space=pl.ANY), row(LANES), row(D_MODEL), mod_spec, vec, vec],
            out_specs=row(D_MODEL),
            scratch_shapes=[pltpu.VMEM((2, plan["buf_rows"], D_MODEL), BF16),
                            pltpu.SemaphoreType.DMA((2,))]),
        out_shape=jax.ShapeDtypeStruct((stream.tokens, D_MODEL), F32),
        compiler_params=_params("arbitrary"),
        name="moe_combine",
    )(plan["off"], plan["run"], plan["base"], ys, meta, x1, mod, lw["ln_ffn_g"], lw["ln_ffn_b"])


_ROPE_SWAP = tuple(list(range(16, 32)) + list(range(0, 16)) + list(range(48, 64)) + list(range(32, 48)))


def _rope_tables(n_tokens):
    rows = n_tokens // GRID_W
    row = np.repeat(np.arange(rows, dtype=np.float64), GRID_W)
    col = np.tile(np.arange(GRID_W, dtype=np.float64), rows)
    half = ROPE_DIM // 2
    inv_freq = ROPE_BASE ** (-np.arange(0, half, 2, dtype=np.float64) / half)
    ar = row[:, None] * inv_freq[None, :]
    ac = col[:, None] * inv_freq[None, :]
    zeros = np.zeros((n_tokens, LANES - ROPE_DIM))
    cos = np.concatenate([np.cos(ar), np.cos(ar), np.cos(ac), np.cos(ac), zeros], axis=-1)
    sin = np.concatenate([-np.sin(ar), np.sin(ar), -np.sin(ac), np.sin(ac), zeros], axis=-1)
    return jnp.asarray(cos, F32), jnp.asarray(sin, F32)


def _layer_weights(l, p):
    swap = jnp.array(_ROPE_SWAP)
    w_in = p["w_in"][l]
    o_kr = 2 * CHUNK_W + Q_RANK + KV_RANK
    kr_cols = w_in[:, o_kr:o_kr + ROPE_DIM]
    zpad = jnp.zeros((D_MODEL, LANES - ROPE_DIM), F32)
    w_in_pad = jnp.concatenate([w_in, zpad], axis=1)
    w_in_lat = jnp.concatenate([w_in_pad, kr_cols[:, swap], zpad], axis=1)

    w_uq = p["w_uq"][l].reshape(Q_RANK, N_HEADS, QK_NOPE + ROPE_DIM)
    q_zero = jnp.zeros((Q_RANK, N_HEADS, HEAD_PAD - QK_NOPE - ROPE_DIM), F32)
    wq_main = jnp.concatenate([w_uq, q_zero], axis=-1).reshape(Q_RANK, QK_W)
    wq_swap = jnp.concatenate([w_uq[:, :, QK_NOPE:][:, :, swap], q_zero], axis=-1).reshape(Q_RANK, N_HEADS * LANES)

    w_ukv = p["w_ukv"][l].reshape(KV_RANK, N_HEADS, QK_NOPE + V_DIM)
    wkv = jnp.concatenate([w_ukv[:, :, :QK_NOPE].reshape(KV_RANK, N_HEADS * QK_NOPE),
                           w_ukv[:, :, QK_NOPE:].reshape(KV_RANK, MLA_W)], axis=1)
    w_out = p["w_out"][l]
    lw = {
        "w_in_ctx": w_in_pad.astype(BF16),
        "w_in_lat": w_in_lat.astype(BF16),
        "q_g": p["q_norm_g"][l].reshape(1, Q_RANK),
        "kv_g": p["kv_norm_g"][l].reshape(1, KV_RANK),
        "wq_main": wq_main.astype(BF16),
        "wq_swap": wq_swap.astype(BF16),
        "wkv": wkv.astype(BF16),
        "ln_v_g": p["chunk_ln_g"][l].reshape(1, CHUNK_W),
        "w_s": p["w_spatial"][l].astype(BF16),
        "b_s": jnp.broadcast_to(p["b_spatial"][l][:, :, None], (N_GROUPS, CHUNK, GROUP_W)),
        "wo_chunk": w_out[:CHUNK_W].astype(BF16),
        "wo_mla": w_out[CHUNK_W:].astype(BF16),
        "ln_mix_g": p["ln_mix_g"][l].reshape(1, D_MODEL),
        "ln_mix_b": p["ln_mix_b"][l].reshape(1, D_MODEL),
        "ln_ffn_g": p["ln_ffn_g"][l].reshape(1, D_MODEL),
        "ln_ffn_b": p["ln_ffn_b"][l].reshape(1, D_MODEL),
    }
    if l % 2 == 0:
        lw["ffn_gate"] = p["ffn_w_gate"][l // 2].astype(BF16)
        lw["ffn_up"] = p["ffn_w_up"][l // 2].astype(BF16)
        lw["ffn_down"] = p["ffn_w_down"][l // 2].astype(BF16)
    else:
        r = jnp.pad(p["router_w"][l // 2], ((0, 0), (0, LANES - N_EXPERTS)))
        r_hi = r.astype(BF16)
        lw["router"] = jnp.concatenate([r_hi, (r - r_hi.astype(F32)).astype(BF16)], axis=1)
        lw["moe_gate"] = p["moe_w_gate"][l // 2].astype(BF16)
        lw["moe_up"] = p["moe_w_up"][l // 2].astype(BF16)
        lw["moe_down"] = p["moe_w_down"][l // 2].astype(BF16)
    return lw


def _trunk(stream, x, mod, weights, rope, ctx, tiles):
    caches = []
    for l in range(DEPTH):
        lw = dict(weights[l])
        lw["w_in"] = lw["w_in_lat"] if stream.latent else lw["w_in_ctx"]
        outs = _mixer_in(stream, l, x, mod, lw, rope, tiles["mixer_in"])
        co, q, kcat, v = outs[:4]
        if not stream.latent:
            caches.append(outs[4:])
        mo = _attend(stream, l, q, kcat, v, lw, ctx, tiles["attend"])
        route = l % 2 == 1
        outs = _mix_out(stream, l, co, mo, x, mod, lw, route, tiles["mix_out"])
        if route:
            x1, h2, meta, cnt = outs
            plan = _route_plan(cnt, stream.tokens, tiles["mix_out"])
            ys = _experts(_dispatch(h2, meta, plan), plan, lw)
            x = _combine(stream, l, ys, meta, x1, mod, plan, lw)
        else:
            x = _ffn(stream, l, outs[1], outs[0], mod, lw, tiles["ffn"])
    return x, caches


def kernel(x_prompt, x_sample, c, cache_ckv, cache_krope, c_ctx, w_mod, b_mod, w_in, q_norm_g, kv_norm_g, w_uq, w_ukv, chunk_ln_g, w_spatial, b_spatial, w_out, ln_mix_g, ln_mix_b, ln_ffn_g, ln_ffn_b, ffn_w_gate, ffn_w_up, ffn_w_down, router_w, moe_w_gate, moe_w_up, moe_w_down):
    p = dict(w_in=w_in, q_norm_g=q_norm_g, kv_norm_g=kv_norm_g, w_uq=w_uq, w_ukv=w_ukv,
             chunk_ln_g=chunk_ln_g, w_spatial=w_spatial, b_spatial=b_spatial, w_out=w_out,
             ln_mix_g=ln_mix_g, ln_mix_b=ln_mix_b, ln_ffn_g=ln_ffn_g, ln_ffn_b=ln_ffn_b,
             ffn_w_gate=ffn_w_gate, ffn_w_up=ffn_w_up, ffn_w_down=ffn_w_down, router_w=router_w,
             moe_w_gate=moe_w_gate, moe_w_up=moe_w_up, moe_w_down=moe_w_down)
    weights = [_layer_weights(l, p) for l in range(DEPTH)]

    batch, seq, _ = x_prompt.shape
    dec_batch, dec_seq, _ = x_sample.shape
    cond_rows = jnp.concatenate(
        [c_ctx[None, :], c, jnp.zeros((MOD_ROWS - 1 - dec_batch, D_MODEL), F32)], axis=0)
    mod = _modulation(cond_rows, w_mod, b_mod).reshape(DEPTH, MOD_ROWS, 6, 1, D_MODEL)

    prompt = _Stream(batch, seq, mod_row0=0, per_row_mod=False, latent=False)
    sample = _Stream(dec_batch, dec_seq, mod_row0=1, per_row_mod=True, latent=True)

    y_prompt, caches = _trunk(
        prompt, x_prompt.reshape(batch * seq, D_MODEL), mod, weights, None, None,
        dict(mixer_in=256, attend=256, mix_out=512, ffn=512))
    new_ckv = jnp.stack([ck.reshape(batch, seq, KV_RANK) for ck, _ in caches], axis=1)
    new_krope = jnp.stack([kr.reshape(batch, seq, ROPE_DIM) for _, kr in caches], axis=1)

    rope = _rope_tables(dec_seq)
    ctx = (cache_ckv, jnp.pad(cache_krope, ((0, 0), (0, 0), (0, 0), (0, LANES - ROPE_DIM))))
    y_sample, _ = _trunk(
        sample, x_sample.reshape(dec_batch * dec_seq, D_MODEL), mod, weights, rope, ctx,
        dict(mixer_in=512, attend=512, mix_out=512, ffn=512))
    return (y_prompt.reshape(batch, seq, D_MODEL), y_sample.reshape(dec_batch, dec_seq, D_MODEL),
            new_ckv, new_krope)
```

```python
import functools
import math

import jax
import jax.numpy as jnp
import numpy as np
from jax import lax
from jax.experimental import pallas as pl
from jax.experimental.pallas import tpu as pltpu

F32 = jnp.float32
BF16 = jnp.bfloat16

D_MODEL = 1024
DEPTH = 2
GRID_W = 64
CHUNK = 128
N_GROUPS = 4
GROUP_W = 128
CHUNK_W = N_GROUPS * GROUP_W
N_HEADS = 4
QK_NOPE = 128
ROPE_DIM = 64
V_DIM = 128
Q_RANK = 384
KV_RANK = 256
MLA_W = N_HEADS * V_DIM
HEAD_PAD = 256
QK_W = N_HEADS * HEAD_PAD
V_EXT = 2 * V_DIM
VEXT_W = N_HEADS * V_EXT
ROPE_BASE = 10000.0
N_EXPERTS = 8
ALPHA = (2 * DEPTH) ** 0.25
EPS = 1e-6
ATTN_SCALE = math.log2(math.e) / math.sqrt(QK_NOPE + ROPE_DIM)
MOD_ROWS = 16
LANES = 128
VMEM_LIMIT = 56 * 1024 * 1024

SH_A, SC_A, G_A, SH_F, SC_F, G_F = range(6)


def _sigmoid(x):
    return 1.0 / (1.0 + jnp.exp(-x))


def _gelu_tanh(x):
    return 0.5 * x * (1.0 + jnp.tanh(math.sqrt(2.0 / math.pi) * (x + 0.044715 * (x * x * x))))


def _layernorm(y, g, b):
    mu = jnp.mean(y, axis=-1, keepdims=True)
    d = y - mu
    var = jnp.mean(d * d, axis=-1, keepdims=True)
    return d * lax.rsqrt(var + EPS) * g + b


def _rmsnorm(y, g):
    return y * lax.rsqrt(jnp.mean(y * y, axis=-1, keepdims=True) + EPS) * g


def _dot(a, b):
    return jnp.dot(a, b, preferred_element_type=F32)


def _dot_nt(a, b):
    return lax.dot_general(a, b, (((1,), (1,)), ((), ())), preferred_element_type=F32)


def _params(*sem):
    return pltpu.CompilerParams(dimension_semantics=sem, vmem_limit_bytes=VMEM_LIMIT)


def _const_spec(shape):
    nd = len(shape)
    return pl.BlockSpec(shape, lambda *_: (0,) * nd)


class _Stream:
    def __init__(self, batch, seq, mod_row0, per_row_mod, latent):
        self.batch = batch
        self.seq = seq
        self.tokens = batch * seq
        self.mod_row0 = mod_row0
        self.per_row_mod = per_row_mod
        self.latent = latent

    def mod_spec(self, layer, which, tm):
        tiles_per_row = self.seq // tm
        row0, per_row = self.mod_row0, self.per_row_mod

        def index(i):
            r = row0 + (i // tiles_per_row if per_row else 0)
            return (layer, r, which, 0, 0)

        return pl.BlockSpec((None, None, None, 1, D_MODEL), index)


def _mod_body(c_ref, w_ref, b_ref, o_ref):
    a = c_ref[...]
    a = (a * _sigmoid(a)).astype(BF16)
    o_ref[...] = _dot(a, w_ref[...].astype(BF16)) + b_ref[...]


def _modulation(cond_rows, w_mod, b_mod):
    depth, _, width = w_mod.shape
    tn = 1536
    return pl.pallas_call(
        _mod_body,
        grid=(depth, width // tn),
        in_specs=[
            pl.BlockSpec((MOD_ROWS, D_MODEL), lambda l, j: (0, 0)),
            pl.BlockSpec((None, D_MODEL, tn), lambda l, j: (l, 0, j)),
            pl.BlockSpec((None, 1, tn), lambda l, j: (l, 0, j)),
        ],
        out_specs=pl.BlockSpec((None, MOD_ROWS, tn), lambda l, j: (l, 0, j)),
        out_shape=jax.ShapeDtypeStruct((depth, MOD_ROWS, width), F32),
        compiler_params=_params("parallel", "parallel"),
        name="modulation",
    )(cond_rows, w_mod, b_mod.reshape(depth, 1, width))


def _store_keys_values(kv, kr, kcat_ref, vext_ref):
    ones = jnp.ones((kv.shape[0], V_EXT - V_DIM), BF16)
    for hd in range(N_HEADS):
        a = hd * HEAD_PAD
        kcat_ref[:, a:a + QK_NOPE] = kv[:, hd * QK_NOPE:(hd + 1) * QK_NOPE].astype(BF16)
        kcat_ref[:, a + QK_NOPE:a + HEAD_PAD] = kr
        b = hd * V_EXT
        vext_ref[:, b:b + V_DIM] = kv[:, (N_HEADS + hd) * V_DIM:(N_HEADS + hd + 1) * V_DIM].astype(BF16)
        vext_ref[:, b + V_DIM:b + V_EXT] = ones


def _mixer_in_body(latent, tm, *refs):
    it = iter(refs)
    x_ref, sc_ref, sh_ref, win_ref, qg_ref, kvg_ref, wqm_ref = (next(it) for _ in range(7))
    wqs_ref = next(it) if latent else None
    wkv_ref, lng_ref, ws_ref, bs_ref = (next(it) for _ in range(4))
    cos_ref, sin_ref = (next(it), next(it)) if latent else (None, None)
    co_ref, q_ref, kcat_ref, vext_ref = (next(it) for _ in range(4))
    ckv_ref, kr_ref = (None, None) if latent else (next(it), next(it))

    h = (x_ref[...] * (1.0 + sc_ref[...]) + sh_ref[...]).astype(BF16)
    p = _dot(h, win_ref[...])

    for g in range(N_GROUPS):
        cols = slice(g * GROUP_W, (g + 1) * GROUP_W)
        vg = _gelu_tanh(p[:, CHUNK_W + g * GROUP_W:CHUNK_W + (g + 1) * GROUP_W])
        mu = jnp.mean(vg, axis=-1, keepdims=True)
        d = vg - mu
        var = jnp.mean(d * d, axis=-1, keepdims=True)
        vn = (d * lax.rsqrt(var + EPS) * lng_ref[:, cols]).astype(BF16)
        ug = _gelu_tanh(p[:, cols])
        for c in range(tm // CHUNK):
            rows = slice(c * CHUNK, (c + 1) * CHUNK)
            z = _dot(ws_ref[g], vn[rows]) + bs_ref[g]
            co_ref[rows, cols] = (ug[rows] * z).astype(BF16)

    o = 2 * CHUNK_W
    cqn = _rmsnorm(p[:, o:o + Q_RANK], qg_ref[...]).astype(BF16)
    qm = _dot(cqn, wqm_ref[...])
    if latent:
        qs = _dot(cqn, wqs_ref[...])
        cos = cos_ref[...]
        sin = sin_ref[...]
    for hd in range(N_HEADS):
        a = hd * HEAD_PAD
        q_ref[:, a:a + QK_NOPE] = (qm[:, a:a + QK_NOPE] * ATTN_SCALE).astype(BF16)
        qr = qm[:, a + QK_NOPE:a + HEAD_PAD]
        if latent:
            qr = qr * cos + qs[:, hd * LANES:(hd + 1) * LANES] * sin
        q_ref[:, a + QK_NOPE:a + HEAD_PAD] = (qr * ATTN_SCALE).astype(BF16)

    o += Q_RANK
    ckvn = _rmsnorm(p[:, o:o + KV_RANK], kvg_ref[...])
    o += KV_RANK
    kr = p[:, o:o + LANES]
    if not latent:
        ckv_ref[...] = ckvn
        kr_ref[...] = kr[:, :ROPE_DIM]
    else:
        kr = kr * cos + p[:, o + LANES:o + 2 * LANES] * sin
    kr = kr.astype(BF16)
    kv = _dot(ckvn.astype(BF16), wkv_ref[...])
    _store_keys_values(kv, kr, kcat_ref, vext_ref)


def _mixer_in(stream, layer, x, mod, lw, rope, tm):
    latent = stream.latent
    t = stream.tokens
    row = lambda w: pl.BlockSpec((tm, w), lambda i: (i, 0))
    win = lw["w_in"]
    win_cols = win.shape[1] if latent else win.shape[1] - LANES
    ins = [x, mod, mod, win, lw["q_g"], lw["kv_g"], lw["wq_main"]]
    specs = [row(D_MODEL), stream.mod_spec(layer, SC_A, tm), stream.mod_spec(layer, SH_A, tm),
             _const_spec((D_MODEL, win_cols)), _const_spec(lw["q_g"].shape), _const_spec(lw["kv_g"].shape),
             _const_spec(lw["wq_main"].shape)]
    if latent:
        ins.append(lw["wq_swap"])
        specs.append(_const_spec(lw["wq_swap"].shape))
    for name in ("wkv", "ln_v_g", "w_s", "b_s"):
        ins.append(lw[name])
        specs.append(_const_spec(lw[name].shape))
    if latent:
        tiles_per_seq = stream.seq // tm
        pos = pl.BlockSpec((tm, LANES), lambda i: (i % tiles_per_seq, 0))
        ins += [rope[0], rope[1]]
        specs += [pos, pos]
    out_shape = [jax.ShapeDtypeStruct((t, CHUNK_W), BF16), jax.ShapeDtypeStruct((t, QK_W), BF16),
                 jax.ShapeDtypeStruct((t, QK_W), BF16), jax.ShapeDtypeStruct((t, VEXT_W), BF16)]
    out_specs = [row(CHUNK_W), row(QK_W), row(QK_W), row(VEXT_W)]
    if not latent:
        out_shape += [jax.ShapeDtypeStruct((t, KV_RANK), F32), jax.ShapeDtypeStruct((t, ROPE_DIM), F32)]
        out_specs += [row(KV_RANK), row(ROPE_DIM)]
    return pl.pallas_call(
        functools.partial(_mixer_in_body, latent, tm),
        grid=(t // tm,),
        in_specs=specs,
        out_specs=out_specs,
        out_shape=out_shape,
        compiler_params=_params("parallel"),
        name="mixer_in",
    )(*ins)


KEY_CHUNK = 256


def _attend_body(has_ctx, *refs):
    if has_ctx:
        q_ref, k_ref, vext_ref, cckv_ref, ckr_ref, wkv_ref, o_ref, kctx_ref, vctx_ref = refs

        @pl.when(pl.program_id(1) == 0)
        def _():
            kv = _dot(cckv_ref[...].astype(BF16), wkv_ref[...])
            _store_keys_values(kv, ckr_ref[...].astype(BF16), kctx_ref, vctx_ref)

        sources = [(kctx_ref, vctx_ref), (k_ref, vext_ref)]
    else:
        q_ref, k_ref, vext_ref, o_ref = refs
        sources = [(k_ref, vext_ref)]
    chunks = [(kr, vr, slice(c * KEY_CHUNK, (c + 1) * KEY_CHUNK))
              for kr, vr in sources for c in range(kr.shape[0] // KEY_CHUNK)]

    def scores(hd):
        qk = slice(hd * HEAD_PAD, (hd + 1) * HEAD_PAD)
        return [_dot_nt(q_ref[:, qk], kr[rows, qk]) for kr, _, rows in chunks]

    s = scores(0)
    for hd in range(N_HEADS):
        m = jnp.max(functools.reduce(jnp.maximum, s), axis=-1, keepdims=True)
        s_next, acc = [], None
        for c, (kr, vr, rows) in enumerate(chunks):
            if hd + 1 < N_HEADS:
                qk = slice((hd + 1) * HEAD_PAD, (hd + 2) * HEAD_PAD)
                s_next.append(_dot_nt(q_ref[:, qk], kr[rows, qk]))
            p = jnp.exp2(s[c] - m).astype(BF16)
            part = _dot(p, vr[rows, hd * V_EXT:(hd + 1) * V_EXT])
            acc = part if acc is None else acc + part
        s = s_next
        o_ref[:, hd * V_DIM:(hd + 1) * V_DIM] = (acc[:, :V_DIM] / acc[:, V_DIM:]).astype(BF16)


def _attend(stream, layer, q, kcat, vext, lw, ctx, tq):
    n = stream.seq
    nq = n // tq
    has_ctx = stream.latent
    ins = [q, kcat, vext]
    specs = [pl.BlockSpec((tq, QK_W), lambda b, i: (b * nq + i, 0)),
             pl.BlockSpec((n, QK_W), lambda b, i: (b, 0)),
             pl.BlockSpec((n, VEXT_W), lambda b, i: (b, 0))]
    scratch = []
    if has_ctx:
        cache_ckv, cache_kr = ctx
        past = cache_ckv.shape[2]
        ins += [cache_ckv, cache_kr, lw["wkv"]]
        specs += [pl.BlockSpec((None, None, past, KV_RANK), lambda b, i: (b, layer, 0, 0)),
                  pl.BlockSpec((None, None, past, LANES), lambda b, i: (b, layer, 0, 0)),
                  _const_spec(lw["wkv"].shape)]
        scratch = [pltpu.VMEM((past, QK_W), BF16), pltpu.VMEM((past, VEXT_W), BF16)]
    return pl.pallas_call(
        functools.partial(_attend_body, has_ctx),
        grid=(stream.batch, nq),
        in_specs=specs,
        out_specs=pl.BlockSpec((tq, MLA_W), lambda b, i: (b * nq + i, 0)),
        out_shape=jax.ShapeDtypeStruct((stream.tokens, MLA_W), BF16),
        scratch_shapes=scratch,
        compiler_params=_params("parallel", "arbitrary"),
        name="attend",
    )(*ins)


MIX_PARTS = 4


def _mix_out_body(route, *refs):
    if route:
        (co_ref, mo_ref, wo1_ref, wo2_ref, x_ref, ga_ref, scf_ref, shf_ref, g_ref, b_ref,
         rcat_ref, x1_ref, h2_ref, meta_ref, cnt_ref) = refs
    else:
        (co_ref, mo_ref, wo1_ref, wo2_ref, x_ref, ga_ref, scf_ref, shf_ref, g_ref, b_ref,
         x1_ref, h2_ref) = refs
    rp = x_ref.shape[0] // MIX_PARTS
    parts = [slice(k * rp, (k + 1) * rp) for k in range(MIX_PARTS)]
    mix = [_dot(co_ref[r], wo1_ref[...]) + _dot(mo_ref[r], wo2_ref[...]) for r in parts]
    x1 = [_layernorm(ALPHA * x_ref[r] + ga_ref[...] * mix[k], g_ref[...], b_ref[...])
          for k, r in enumerate(parts)]
    h2 = [v * (1.0 + scf_ref[...]) + shf_ref[...] for v in x1]
    hh = [v.astype(BF16) for v in h2]
    for k, r in enumerate(parts):
        x1_ref[r] = x1[k]
        h2_ref[r] = hh[k]
    if not route:
        return
    hl = [(h2[k] - hh[k].astype(F32)).astype(BF16) for k in range(MIX_PARTS)]
    ra = [_dot(v, rcat_ref[...]) for v in hh]
    rb = [_dot(v, rcat_ref[...]) for v in hl]
    logits = [ra[k][:, :LANES] + (ra[k][:, LANES:] + (rb[k][:, :LANES] + rb[k][:, LANES:]))
              for k in range(MIX_PARTS)]
    lane = lax.broadcasted_iota(jnp.int32, (rp, LANES), 1).astype(F32)
    neg = -jnp.inf
    lg = [jnp.where(lane < N_EXPERTS, v, neg) for v in logits]
    m1 = [jnp.max(v, axis=-1, keepdims=True) for v in lg]
    i1 = [jnp.min(jnp.where(lg[k] == m1[k], lane, float(LANES)), axis=-1, keepdims=True)
          for k in range(MIX_PARTS)]
    lg2 = [jnp.where(lane == i1[k], neg, lg[k]) for k in range(MIX_PARTS)]
    m2 = [jnp.max(v, axis=-1, keepdims=True) for v in lg2]
    i2 = [jnp.min(jnp.where(lg2[k] == m2[k], lane, float(LANES)), axis=-1, keepdims=True)
          for k in range(MIX_PARTS)]
    picked = [jnp.where(lane == i1[k], 1.0, jnp.where(lane == i2[k], 1.0, 0.0)) for k in range(MIX_PARTS)]
    earlier = jnp.where(lax.broadcasted_iota(jnp.int32, (rp, rp), 0)
                        > lax.broadcasted_iota(jnp.int32, (rp, rp), 1), 1.0, 0.0).astype(BF16)
    before = jnp.zeros((1, LANES), F32)
    for k, r in enumerate(parts):
        rank = _dot(earlier, picked[k].astype(BF16)) + before
        before = before + jnp.sum(picked[k], axis=0, keepdims=True)
        rank1 = jnp.sum(jnp.where(lane == i1[k], rank, 0.0), axis=-1, keepdims=True)
        rank2 = jnp.sum(jnp.where(lane == i2[k], rank, 0.0), axis=-1, keepdims=True)
        e2 = jnp.exp(m2[k] - m1[k])
        den = 1.0 + e2
        meta = jnp.zeros((rp, LANES), F32)
        for j, val in enumerate((i1[k], i2[k], 1.0 / den, e2 / den, rank1, rank2)):
            meta = jnp.where(lane == float(j), val, meta)
        meta_ref[r] = meta
    cnt_ref[...] = before


def _mix_out(stream, layer, co, mo, x, mod, lw, route, tm):
    t = stream.tokens
    row = lambda w: pl.BlockSpec((tm, w), lambda i: (i, 0))
    vec = _const_spec((1, D_MODEL))
    ins = [co, mo, lw["wo_chunk"], lw["wo_mla"], x, mod, mod, mod, lw["ln_mix_g"], lw["ln_mix_b"]]
    specs = [row(CHUNK_W), row(MLA_W), _const_spec(lw["wo_chunk"].shape), _const_spec(lw["wo_mla"].shape),
             row(D_MODEL), stream.mod_spec(layer, G_A, tm), stream.mod_spec(layer, SC_F, tm),
             stream.mod_spec(layer, SH_F, tm), vec, vec]
    out_shape = [jax.ShapeDtypeStruct((t, D_MODEL), F32), jax.ShapeDtypeStruct((t, D_MODEL), BF16)]
    out_specs = [row(D_MODEL), row(D_MODEL)]
    if route:
        ins.append(lw["router"])
        specs.append(_const_spec(lw["router"].shape))
        out_shape += [jax.ShapeDtypeStruct((t, LANES), F32), jax.ShapeDtypeStruct((t // tm, 1, LANES), F32)]
        out_specs += [row(LANES), pl.BlockSpec((None, 1, LANES), lambda i: (i, 0, 0))]
    return pl.pallas_call(
        functools.partial(_mix_out_body, route),
        grid=(t // tm,),
        in_specs=specs,
        out_specs=out_specs,
        out_shape=out_shape,
        compiler_params=_params("parallel"),
        name="mix_out",
    )(*ins)


def _ffn_body(h_ref, x_ref, wg_ref, wu_ref, wd_ref, gf_ref, g_ref, b_ref, o_ref):
    h = h_ref[...]
    gate = _dot(h, wg_ref[...])
    up = _dot(h, wu_ref[...])
    a = (gate * _sigmoid(gate) * up).astype(BF16)
    f = _dot(a, wd_ref[...])
    o_ref[...] = _layernorm(ALPHA * x_ref[...] + gf_ref[...] * f, g_ref[...], b_ref[...])


def _ffn(stream, layer, h2, x1, mod, lw, tm):
    t = stream.tokens
    row = lambda w: pl.BlockSpec((tm, w), lambda i: (i, 0))
    vec = _const_spec((1, D_MODEL))
    resident = lambda a: pl.BlockSpec(a.shape, lambda i: (0, 0), pipeline_mode=pl.Buffered(1))
    return pl.pallas_call(
        _ffn_body,
        grid=(t // tm,),
        in_specs=[row(D_MODEL), row(D_MODEL), resident(lw["ffn_gate"]), resident(lw["ffn_up"]),
                  resident(lw["ffn_down"]), stream.mod_spec(layer, G_F, tm), vec, vec],
        out_specs=row(D_MODEL),
        out_shape=jax.ShapeDtypeStruct((t, D_MODEL), F32),
        compiler_params=_params("parallel"),
        name="ffn_dense",
    )(h2, x1, lw["ffn_gate"], lw["ffn_up"], lw["ffn_down"], mod, lw["ln_ffn_g"], lw["ln_ffn_b"])


RUN_ALIGN = 16


def _route_plan(cnt, tokens, tm, tile):
    nw = tokens // tm
    worst_pad = (RUN_ALIGN - 1) * nw
    cap = pl.cdiv(tokens + worst_pad, tile) * tile
    n_tiles = (2 * tokens + N_EXPERTS * worst_pad) // tile + N_EXPERTS
    cnt = cnt[:, 0, :N_EXPERTS].astype(jnp.int32)
    run = (cnt + RUN_ALIGN - 1) // RUN_ALIGN * RUN_ALIGN
    off = jnp.cumsum(run, axis=1) - run
    base = jnp.arange(N_EXPERTS, dtype=jnp.int32) * cap + jnp.cumsum(run, axis=0) - run
    totals = jnp.sum(run, axis=0)
    tiles = (totals + tile - 1) // tile
    cum = jnp.cumsum(tiles)
    n_valid = cum[-1]
    i = jnp.minimum(jnp.arange(n_tiles, dtype=jnp.int32), n_valid - 1)
    te = jnp.sum((cum[None, :] <= i[:, None]).astype(jnp.int32), axis=1)
    tj = i - (cum - tiles)[te]
    return dict(
        tm=tm, nw=nw, cap=cap, tile=tile, n_tiles=n_tiles, buf_rows=2 * tm + N_EXPERTS * RUN_ALIGN,
        off=off.reshape(-1), run=run.reshape(-1), base=base.reshape(-1),
        tile_expert=te, tile_block=te * (cap // tile) + tj,
        tile_rows=jnp.minimum(tile, totals[te] - tj * tile),
        n_valid=n_valid.reshape(1))


def _run_piece_sizes(tm):
    sizes, b = [], RUN_ALIGN
    while b <= tm:
        sizes.append(b)
        b *= 2
    return sizes


def _row_run_copies(src, dst, src_off, dst_off, n_rows, sem, sizes, action):
    for b in sizes:
        done = n_rows & (-2 * b)
        cp = pltpu.make_async_copy(
            src.at[pl.ds(pl.multiple_of(src_off + done, RUN_ALIGN), b)],
            dst.at[pl.ds(pl.multiple_of(dst_off + done, RUN_ALIGN), b)], sem)

        @pl.when((n_rows & b) != 0)
        def _():
            action(cp)


def _local_rows(meta, off_ref, w):
    i1, i2 = meta[:, 0:1], meta[:, 1:2]
    o1, o2 = jnp.zeros_like(i1), jnp.zeros_like(i2)
    for e in range(N_EXPERTS):
        off_e = off_ref[w * N_EXPERTS + e].astype(F32)
        o1 = jnp.where(i1 == float(e), off_e, o1)
        o2 = jnp.where(i2 == float(e), off_e, o2)
    return (o1 + meta[:, 4:5]).astype(jnp.int32), (o2 + meta[:, 5:6]).astype(jnp.int32)


def _start(cp):
    cp.start()


def _wait(cp):
    cp.wait()


def _dispatch_body(tm, nw, off_ref, run_ref, base_ref, h_ref, meta_ref, xs_ref, xw_ref, sems):
    w = pl.program_id(0)
    slot = w % 2
    sizes = _run_piece_sizes(tm)

    def push(win, buf, action):
        for e in range(N_EXPERTS):
            k = win * N_EXPERTS + e
            _row_run_copies(xw_ref.at[buf], xs_ref, off_ref[k], base_ref[k], run_ref[k],
                            sems.at[buf], sizes, action)

    r1, r2 = _local_rows(meta_ref[...], off_ref, w)
    col = lax.broadcasted_iota(jnp.int32, (tm, xw_ref.shape[1]), 1)
    sel = jnp.where(col == r1, 1.0, jnp.where(col == r2, 1.0, 0.0)).astype(BF16)
    xw = lax.dot_general(sel, h_ref[...], (((0,), (0,)), ((), ())), preferred_element_type=F32)
    xw_ref[slot] = xw.astype(BF16)
    push(w, slot, _start)

    @pl.when(w > 0)
    def _():
        push(w - 1, 1 - slot, _wait)

    @pl.when(w == nw - 1)
    def _():
        push(w, slot, _wait)


def _dispatch(h2, meta, plan):
    tm = plan["tm"]
    return pl.pallas_call(
        functools.partial(_dispatch_body, tm, plan["nw"]),
        grid_spec=pltpu.PrefetchScalarGridSpec(
            num_scalar_prefetch=3,
            grid=(plan["nw"],),
            in_specs=[pl.BlockSpec((tm, D_MODEL), lambda i, *_: (i, 0)),
                      pl.BlockSpec((tm, LANES), lambda i, *_: (i, 0))],
            out_specs=pl.BlockSpec(memory_space=pl.ANY),
            scratch_shapes=[pltpu.VMEM((2, plan["buf_rows"], D_MODEL), BF16),
                            pltpu.SemaphoreType.DMA((2,))]),
        out_shape=jax.ShapeDtypeStruct((N_EXPERTS * plan["cap"], D_MODEL), BF16),
        compiler_params=_params("arbitrary"),
        name="moe_dispatch",
    )(plan["off"], plan["run"], plan["base"], h2, meta)


def _experts_body(te_ref, tb_ref, rows_ref, nv_ref, x_ref, wg_ref, wu_ref, wd_ref, o_ref):
    i = pl.program_id(0)

    @pl.when(i < nv_ref[0])
    def _():
        x = x_ref[...]
        live = lax.broadcasted_iota(jnp.int32, (x.shape[0], 1), 0) < rows_ref[i]
        x = jnp.where(live, x, jnp.zeros_like(x))
        gate = _dot(x, wg_ref[...])
        up = _dot(x, wu_ref[...])
        a = (gate * _sigmoid(gate) * up).astype(BF16)
        o_ref[...] = _dot(a, wd_ref[...]).astype(BF16)


def _experts(xs, plan, lw):
    dff = lw["moe_gate"].shape[2]
    rows = pl.BlockSpec((plan["tile"], D_MODEL), lambda i, te, tb, *_: (tb[i], 0))
    return pl.pallas_call(
        _experts_body,
        grid_spec=pltpu.PrefetchScalarGridSpec(
            num_scalar_prefetch=4,
            grid=(plan["n_tiles"],),
            in_specs=[rows,
                      pl.BlockSpec((None, D_MODEL, dff), lambda i, te, *_: (te[i], 0, 0)),
                      pl.BlockSpec((None, D_MODEL, dff), lambda i, te, *_: (te[i], 0, 0)),
                      pl.BlockSpec((None, dff, D_MODEL), lambda i, te, *_: (te[i], 0, 0))],
            out_specs=rows),
        out_shape=jax.ShapeDtypeStruct(xs.shape, BF16),
        compiler_params=_params("arbitrary"),
        name="moe_experts",
    )(plan["tile_expert"], plan["tile_block"], plan["tile_rows"], plan["n_valid"],
      xs, lw["moe_gate"], lw["moe_up"], lw["moe_down"])


def _combine_body(tm, nw, off_ref, run_ref, base_ref, ys_ref, meta_ref, x_ref, gf_ref, g_ref, b_ref,
                  o_ref, yw_ref, sems):
    w = pl.program_id(0)
    slot = w % 2
    sizes = _run_piece_sizes(tm)

    def fetch(win, buf, action):
        for e in range(N_EXPERTS):
            k = win * N_EXPERTS + e
            _row_run_copies(ys_ref, yw_ref.at[buf], base_ref[k], off_ref[k], run_ref[k],
                            sems.at[buf], sizes, action)

    @pl.when(w == 0)
    def _():
        yw_ref[...] = jnp.zeros_like(yw_ref)
        fetch(0, 0, _start)

    @pl.when(w + 1 < nw)
    def _():
        fetch(w + 1, 1 - slot, _start)

    fetch(w, slot, _wait)
    rp = tm // MIX_PARTS
    parts = [slice(k * rp, (k + 1) * rp) for k in range(MIX_PARTS)]
    meta = [meta_ref[r] for r in parts]
    picks = [_local_rows(v, off_ref, w) for v in meta]
    col = lax.broadcasted_iota(jnp.int32, (rp, yw_ref.shape[1]), 1)
    yw = yw_ref[slot]
    gate = [jnp.where(col == r1, v[:, 2:3], jnp.where(col == r2, v[:, 3:4], 0.0)).astype(BF16)
            for v, (r1, r2) in zip(meta, picks)]
    f = [_dot(g, yw) for g in gate]
    for k, r in enumerate(parts):
        o_ref[r] = _layernorm(ALPHA * x_ref[r] + gf_ref[...] * f[k], g_ref[...], b_ref[...])


def _combine(stream, layer, ys, meta, x1, mod, plan, lw):
    tm = plan["tm"]
    row = lambda width: pl.BlockSpec((tm, width), lambda i, *_: (i, 0))
    vec = pl.BlockSpec((1, D_MODEL), lambda i, *_: (0, 0))
    mod_spec = stream.mod_spec(layer, G_F, tm)
    mod_spec = pl.BlockSpec(mod_spec.block_shape, lambda i, *_, f=mod_spec.index_map: f(i))
    return pl.pallas_call(
        functools.partial(_combine_body, tm, plan["nw"]),
        grid_spec=pltpu.PrefetchScalarGridSpec(
            num_scalar_prefetch=3,
            grid=(plan["nw"],),
            in_specs=[pl.BlockSpec(memory_space=pl.ANY), row(LANES), row(D_MODEL), mod_spec, vec, vec],
            out_specs=row(D_MODEL),
            scratch_shapes=[pltpu.VMEM((2, plan["buf_rows"], D_MODEL), BF16),
                            pltpu.SemaphoreType.DMA((2,))]),
        out_shape=jax.ShapeDtypeStruct((stream.tokens, D_MODEL), F32),
        compiler_params=_params("arbitrary"),
        name="moe_combine",
    )(plan["off"], plan["run"], plan["base"], ys, meta, x1, mod, lw["ln_ffn_g"], lw["ln_ffn_b"])


def _rope_tables(n_tokens):
    rows = n_tokens // GRID_W
    row = np.repeat(np.arange(rows, dtype=np.float64), GRID_W)
    col = np.tile(np.arange(GRID_W, dtype=np.float64), rows)
    half = ROPE_DIM // 2
    inv_freq = ROPE_BASE ** (-np.arange(0, half, 2, dtype=np.float64) / half)
    ar = row[:, None] * inv_freq[None, :]
    ac = col[:, None] * inv_freq[None, :]
    zeros = np.zeros((n_tokens, LANES - ROPE_DIM))
    cos = np.concatenate([np.cos(ar), np.cos(ar), np.cos(ac), np.cos(ac), zeros], axis=-1)
    sin = np.concatenate([-np.sin(ar), np.sin(ar), -np.sin(ac), np.sin(ac), zeros], axis=-1)
    return jnp.asarray(cos, F32), jnp.asarray(sin, F32)


def _swap_rope_halves(a):
    q = ROPE_DIM // 4
    return jnp.concatenate([a[..., q:2 * q], a[..., :q], a[..., 3 * q:], a[..., 2 * q:3 * q]], axis=-1)


def _layer_weights(l, p):
    w_in = p["w_in"][l].astype(BF16)
    o_kr = 2 * CHUNK_W + Q_RANK + KV_RANK
    zpad = jnp.zeros((D_MODEL, LANES - ROPE_DIM), BF16)
    w_in_ext = jnp.concatenate(
        [w_in, zpad, _swap_rope_halves(w_in[:, o_kr:o_kr + ROPE_DIM]), zpad], axis=1)

    w_uq = p["w_uq"][l].astype(BF16).reshape(Q_RANK, N_HEADS, QK_NOPE + ROPE_DIM)
    q_zero = jnp.zeros((Q_RANK, N_HEADS, HEAD_PAD - QK_NOPE - ROPE_DIM), BF16)
    wq_main = jnp.concatenate([w_uq, q_zero], axis=-1).reshape(Q_RANK, QK_W)
    wq_swap = jnp.concatenate([_swap_rope_halves(w_uq[:, :, QK_NOPE:]), q_zero], axis=-1)
    wq_swap = wq_swap.reshape(Q_RANK, N_HEADS * LANES)

    w_ukv = p["w_ukv"][l].astype(BF16).reshape(KV_RANK, N_HEADS, QK_NOPE + V_DIM)
    wkv = jnp.concatenate([w_ukv[:, :, :QK_NOPE].reshape(KV_RANK, N_HEADS * QK_NOPE),
                           w_ukv[:, :, QK_NOPE:].reshape(KV_RANK, MLA_W)], axis=1)
    w_out = p["w_out"][l]
    lw = {
        "w_in": w_in_ext,
        "q_g": p["q_norm_g"][l].reshape(1, Q_RANK),
        "kv_g": p["kv_norm_g"][l].reshape(1, KV_RANK),
        "wq_main": wq_main,
        "wq_swap": wq_swap,
        "wkv": wkv,
        "ln_v_g": p["chunk_ln_g"][l].reshape(1, CHUNK_W),
        "w_s": p["w_spatial"][l].astype(BF16),
        "b_s": jnp.broadcast_to(p["b_spatial"][l][:, :, None], (N_GROUPS, CHUNK, GROUP_W)),
        "wo_chunk": w_out[:CHUNK_W].astype(BF16),
        "wo_mla": w_out[CHUNK_W:].astype(BF16),
        "ln_mix_g": p["ln_mix_g"][l].reshape(1, D_MODEL),
        "ln_mix_b": p["ln_mix_b"][l].reshape(1, D_MODEL),
        "ln_ffn_g": p["ln_ffn_g"][l].reshape(1, D_MODEL),
        "ln_ffn_b": p["ln_ffn_b"][l].reshape(1, D_MODEL),
    }
    if l % 2 == 0:
        lw["ffn_gate"] = p["ffn_w_gate"][l // 2].astype(BF16)
        lw["ffn_up"] = p["ffn_w_up"][l // 2].astype(BF16)
        lw["ffn_down"] = p["ffn_w_down"][l // 2].astype(BF16)
    else:
        r = jnp.pad(p["router_w"][l // 2], ((0, 0), (0, LANES - N_EXPERTS)))
        r_hi = r.astype(BF16)
        lw["router"] = jnp.concatenate([r_hi, (r - r_hi.astype(F32)).astype(BF16)], axis=1)
        lw["moe_gate"] = p["moe_w_gate"][l // 2].astype(BF16)
        lw["moe_up"] = p["moe_w_up"][l // 2].astype(BF16)
        lw["moe_down"] = p["moe_w_down"][l // 2].astype(BF16)
    return lw


def _trunk(stream, x, mod, weights, rope, ctx, tiles):
    caches = []
    for l in range(DEPTH):
        lw = weights[l]
        outs = _mixer_in(stream, l, x, mod, lw, rope, tiles["mixer_in"])
        co, q, kcat, v = outs[:4]
        if not stream.latent:
            caches.append(outs[4:])
        mo = _attend(stream, l, q, kcat, v, lw, ctx, tiles["attend"])
        route = l % 2 == 1
        outs = _mix_out(stream, l, co, mo, x, mod, lw, route, tiles["route" if route else "mix_out"])
        if route:
            x1, h2, meta, cnt = outs
            plan = _route_plan(cnt, stream.tokens, tiles["route"], tiles["experts"])
            ys = _experts(_dispatch(h2, meta, plan), plan, lw)
            x = _combine(stream, l, ys, meta, x1, mod, plan, lw)
        else:
            x = _ffn(stream, l, outs[1], outs[0], mod, lw, tiles["ffn"])
    return x, caches


def kernel(x_prompt, x_sample, c, cache_ckv, cache_krope, c_ctx, w_mod, b_mod, w_in, q_norm_g, kv_norm_g, w_uq, w_ukv, chunk_ln_g, w_spatial, b_spatial, w_out, ln_mix_g, ln_mix_b, ln_ffn_g, ln_ffn_b, ffn_w_gate, ffn_w_up, ffn_w_down, router_w, moe_w_gate, moe_w_up, moe_w_down):
    p = dict(w_in=w_in, q_norm_g=q_norm_g, kv_norm_g=kv_norm_g, w_uq=w_uq, w_ukv=w_ukv,
             chunk_ln_g=chunk_ln_g, w_spatial=w_spatial, b_spatial=b_spatial, w_out=w_out,
             ln_mix_g=ln_mix_g, ln_mix_b=ln_mix_b, ln_ffn_g=ln_ffn_g, ln_ffn_b=ln_ffn_b,
             ffn_w_gate=ffn_w_gate, ffn_w_up=ffn_w_up, ffn_w_down=ffn_w_down, router_w=router_w,
             moe_w_gate=moe_w_gate, moe_w_up=moe_w_up, moe_w_down=moe_w_down)
    weights = [_layer_weights(l, p) for l in range(DEPTH)]

    batch, seq, _ = x_prompt.shape
    dec_batch, dec_seq, _ = x_sample.shape
    cond_rows = jnp.concatenate(
        [c_ctx[None, :], c, jnp.zeros((MOD_ROWS - 1 - dec_batch, D_MODEL), F32)], axis=0)
    mod = _modulation(cond_rows, w_mod, b_mod).reshape(DEPTH, MOD_ROWS, 6, 1, D_MODEL)

    prompt = _Stream(batch, seq, mod_row0=0, per_row_mod=False, latent=False)
    sample = _Stream(dec_batch, dec_seq, mod_row0=1, per_row_mod=True, latent=True)

    y_prompt, caches = _trunk(
        prompt, x_prompt.reshape(batch * seq, D_MODEL), mod, weights, None, None,
        dict(mixer_in=256, attend=256, mix_out=512, route=512, experts=256, ffn=512))
    new_ckv = jnp.stack([ck.reshape(batch, seq, KV_RANK) for ck, _ in caches], axis=1)
    new_krope = jnp.stack([kr.reshape(batch, seq, ROPE_DIM) for _, kr in caches], axis=1)

    rope = _rope_tables(dec_seq)
    ctx = (cache_ckv, jnp.pad(cache_krope, ((0, 0), (0, 0), (0, 0), (0, LANES - ROPE_DIM))))
    y_sample, _ = _trunk(
        sample, x_sample.reshape(dec_batch * dec_seq, D_MODEL), mod, weights, rope, ctx,
        dict(mixer_in=1024, attend=1024, mix_out=1024, route=512, experts=512, ffn=512))
    return (y_prompt.reshape(batch, seq, D_MODEL), y_sample.reshape(dec_batch, dec_seq, D_MODEL),
            new_ckv, new_krope)
```

```python
import functools
import math

import jax
import jax.numpy as jnp
import numpy as np
from jax import lax
from jax.experimental import pallas as pl
from jax.experimental.pallas import tpu as pltpu

F32 = jnp.float32
BF16 = jnp.bfloat16

D_MODEL = 1024
DEPTH = 2
GRID_W = 64
CHUNK = 128
N_GROUPS = 4
GROUP_W = 128
CHUNK_W = N_GROUPS * GROUP_W
N_HEADS = 4
QK_NOPE = 128
ROPE_DIM = 64
V_DIM = 128
Q_RANK = 384
KV_RANK = 256
MLA_W = N_HEADS * V_DIM
HEAD_PAD = 256
QK_W = N_HEADS * HEAD_PAD
V_EXT = 2 * V_DIM
VEXT_W = N_HEADS * V_EXT
ROPE_BASE = 10000.0
N_EXPERTS = 8
ALPHA = (2 * DEPTH) ** 0.25
EPS = 1e-6
ATTN_SCALE = math.log2(math.e) / math.sqrt(QK_NOPE + ROPE_DIM)
MOD_ROWS = 16
LANES = 128
VMEM_LIMIT = 56 * 1024 * 1024

SH_A, SC_A, G_A, SH_F, SC_F, G_F = range(6)


def _sigmoid(x):
    return 1.0 / (1.0 + jnp.exp(-x))


def _gelu_tanh(x):
    return 0.5 * x * (1.0 + jnp.tanh(math.sqrt(2.0 / math.pi) * (x + 0.044715 * (x * x * x))))


def _layernorm(y, g, b):
    mu = jnp.mean(y, axis=-1, keepdims=True)
    d = y - mu
    var = jnp.mean(d * d, axis=-1, keepdims=True)
    return d * lax.rsqrt(var + EPS) * g + b


def _rmsnorm(y, g):
    return y * lax.rsqrt(jnp.mean(y * y, axis=-1, keepdims=True) + EPS) * g


def _dot(a, b):
    return jnp.dot(a, b, preferred_element_type=F32)


def _dot_nt(a, b):
    return lax.dot_general(a, b, (((1,), (1,)), ((), ())), preferred_element_type=F32)


def _params(*sem):
    return pltpu.CompilerParams(dimension_semantics=sem, vmem_limit_bytes=VMEM_LIMIT)


def _const_spec(shape):
    nd = len(shape)
    return pl.BlockSpec(shape, lambda *_: (0,) * nd)


class _Stream:
    def __init__(self, batch, seq, mod_row0, per_row_mod, latent):
        self.batch = batch
        self.seq = seq
        self.tokens = batch * seq
        self.mod_row0 = mod_row0
        self.per_row_mod = per_row_mod
        self.latent = latent

    def mod_spec(self, layer, which, tm):
        tiles_per_row = self.seq // tm
        row0, per_row = self.mod_row0, self.per_row_mod

        def index(i):
            r = row0 + (i // tiles_per_row if per_row else 0)
            return (layer, r, which, 0, 0)

        return pl.BlockSpec((None, None, None, 1, D_MODEL), index)


def _mod_body(c_ref, w_ref, b_ref, o_ref):
    a = c_ref[...]
    a = (a * _sigmoid(a)).astype(BF16)
    o_ref[...] = _dot(a, w_ref[...].astype(BF16)) + b_ref[...]


def _modulation(cond_rows, w_mod, b_mod):
    depth, _, width = w_mod.shape
    tn = 1536
    return pl.pallas_call(
        _mod_body,
        grid=(depth, width // tn),
        in_specs=[
            pl.BlockSpec((MOD_ROWS, D_MODEL), lambda l, j: (0, 0)),
            pl.BlockSpec((None, D_MODEL, tn), lambda l, j: (l, 0, j)),
            pl.BlockSpec((None, 1, tn), lambda l, j: (l, 0, j)),
        ],
        out_specs=pl.BlockSpec((None, MOD_ROWS, tn), lambda l, j: (l, 0, j)),
        out_shape=jax.ShapeDtypeStruct((depth, MOD_ROWS, width), F32),
        compiler_params=_params("parallel", "parallel"),
        name="modulation",
    )(cond_rows, w_mod, b_mod.reshape(depth, 1, width))


def _store_keys_values(kv, kr, kcat_ref, vext_ref):
    ones = jnp.ones((kv.shape[0], V_EXT - V_DIM), BF16)
    for hd in range(N_HEADS):
        a = hd * HEAD_PAD
        kcat_ref[:, a:a + QK_NOPE] = kv[:, hd * QK_NOPE:(hd + 1) * QK_NOPE].astype(BF16)
        kcat_ref[:, a + QK_NOPE:a + HEAD_PAD] = kr
        b = hd * V_EXT
        vext_ref[:, b:b + V_DIM] = kv[:, (N_HEADS + hd) * V_DIM:(N_HEADS + hd + 1) * V_DIM].astype(BF16)
        vext_ref[:, b + V_DIM:b + V_EXT] = ones


def _mixer_in_body(latent, tm, *refs):
    it = iter(refs)
    x_ref, sc_ref, sh_ref, win_ref, qg_ref, kvg_ref, wq_ref = (next(it) for _ in range(7))
    wkv_ref, lng_ref, ws_ref, bs_ref = (next(it) for _ in range(4))
    cos_ref, sin_ref = (next(it), next(it)) if latent else (None, None)
    co_ref, q_ref, kcat_ref, vext_ref = (next(it) for _ in range(4))
    ckv_ref, kr_ref = (None, None) if latent else (next(it), next(it))

    h = (x_ref[...] * (1.0 + sc_ref[...]) + sh_ref[...]).astype(BF16)
    p = _dot(h, win_ref[...])

    for g in range(N_GROUPS):
        cols = slice(g * GROUP_W, (g + 1) * GROUP_W)
        vg = _gelu_tanh(p[:, CHUNK_W + g * GROUP_W:CHUNK_W + (g + 1) * GROUP_W])
        mu = jnp.mean(vg, axis=-1, keepdims=True)
        d = vg - mu
        var = jnp.mean(d * d, axis=-1, keepdims=True)
        vn = (d * lax.rsqrt(var + EPS) * lng_ref[:, cols]).astype(BF16)
        ug = _gelu_tanh(p[:, cols])
        for c in range(tm // CHUNK):
            rows = slice(c * CHUNK, (c + 1) * CHUNK)
            z = _dot(ws_ref[g], vn[rows]) + bs_ref[g]
            co_ref[rows, cols] = (ug[rows] * z).astype(BF16)

    o = 2 * CHUNK_W
    cqn = _rmsnorm(p[:, o:o + Q_RANK], qg_ref[...]).astype(BF16)
    qc = _dot(cqn, wq_ref[...])
    q_rope = qc[:, N_HEADS * QK_NOPE:N_HEADS * (QK_NOPE + ROPE_DIM)]
    if latent:
        cos = cos_ref[...]
        sin = sin_ref[...]
        q_rope = q_rope * cos + qc[:, N_HEADS * (QK_NOPE + ROPE_DIM):] * sin
    for hd in range(N_HEADS):
        a = hd * HEAD_PAD
        q_ref[:, a:a + QK_NOPE] = (qc[:, hd * QK_NOPE:(hd + 1) * QK_NOPE] * ATTN_SCALE).astype(BF16)
        pair = q_rope[:, (hd // 2) * LANES:(hd // 2 + 1) * LANES]
        if hd % 2:
            pair = pltpu.roll(pair, ROPE_DIM, axis=1)
        q_ref[:, a + QK_NOPE:a + HEAD_PAD] = (pair * ATTN_SCALE).astype(BF16)

    o += Q_RANK
    ckvn = _rmsnorm(p[:, o:o + KV_RANK], kvg_ref[...])
    o += KV_RANK
    kr = p[:, o:o + LANES]
    if not latent:
        ckv_ref[...] = ckvn
        kr_ref[...] = kr[:, :ROPE_DIM]
    else:
        kr = kr * cos[:, :LANES] + p[:, o + LANES:o + 2 * LANES] * sin[:, :LANES]
    kr = kr.astype(BF16)
    kv = _dot(ckvn.astype(BF16), wkv_ref[...])
    _store_keys_values(kv, kr, kcat_ref, vext_ref)


def _mixer_in(stream, layer, x, mod, lw, rope, tm):
    latent = stream.latent
    t = stream.tokens
    row = lambda w: pl.BlockSpec((tm, w), lambda i: (i, 0))
    win = lw["w_in"]
    win_cols = win.shape[1] if latent else win.shape[1] - LANES
    wq = lw["wq"]
    wq_cols = wq.shape[1] if latent else wq.shape[1] - N_HEADS * ROPE_DIM
    ins = [x, mod, mod, win, lw["q_g"], lw["kv_g"], wq]
    specs = [row(D_MODEL), stream.mod_spec(layer, SC_A, tm), stream.mod_spec(layer, SH_A, tm),
             _const_spec((D_MODEL, win_cols)), _const_spec(lw["q_g"].shape), _const_spec(lw["kv_g"].shape),
             _const_spec((Q_RANK, wq_cols))]
    for name in ("wkv", "ln_v_g", "w_s", "b_s"):
        ins.append(lw[name])
        specs.append(_const_spec(lw[name].shape))
    if latent:
        tiles_per_seq = stream.seq // tm
        pos = pl.BlockSpec((tm, N_HEADS * ROPE_DIM), lambda i: (i % tiles_per_seq, 0))
        ins += [rope[0], rope[1]]
        specs += [pos, pos]
    out_shape = [jax.ShapeDtypeStruct((t, CHUNK_W), BF16), jax.ShapeDtypeStruct((t, QK_W), BF16),
                 jax.ShapeDtypeStruct((t, QK_W), BF16), jax.ShapeDtypeStruct((t, VEXT_W), BF16)]
    out_specs = [row(CHUNK_W), row(QK_W), row(QK_W), row(VEXT_W)]
    if not latent:
        out_shape += [jax.ShapeDtypeStruct((t, KV_RANK), F32), jax.ShapeDtypeStruct((t, ROPE_DIM), F32)]
        out_specs += [row(KV_RANK), row(ROPE_DIM)]
    return pl.pallas_call(
        functools.partial(_mixer_in_body, latent, tm),
        grid=(t // tm,),
        in_specs=specs,
        out_specs=out_specs,
        out_shape=out_shape,
        compiler_params=_params("parallel"),
        name="mixer_in",
    )(*ins)


KEY_CHUNK = 256


def _attend_body(has_ctx, *refs):
    if has_ctx:
        q_ref, k_ref, vext_ref, cckv_ref, ckr_ref, wkv_ref, o_ref, kctx_ref, vctx_ref = refs

        @pl.when(pl.program_id(1) == 0)
        def _():
            kv = _dot(cckv_ref[...].astype(BF16), wkv_ref[...])
            _store_keys_values(kv, ckr_ref[...].astype(BF16), kctx_ref, vctx_ref)

        sources = [(kctx_ref, vctx_ref), (k_ref, vext_ref)]
    else:
        q_ref, k_ref, vext_ref, o_ref = refs
        sources = [(k_ref, vext_ref)]
    chunks = [(kr, vr, slice(c * KEY_CHUNK, (c + 1) * KEY_CHUNK))
              for kr, vr in sources for c in range(kr.shape[0] // KEY_CHUNK)]

    def scores(hd):
        qk = slice(hd * HEAD_PAD, (hd + 1) * HEAD_PAD)
        return [_dot_nt(q_ref[:, qk], kr[rows, qk]) for kr, _, rows in chunks]

    s = scores(0)
    for hd in range(N_HEADS):
        m = jnp.max(functools.reduce(jnp.maximum, s), axis=-1, keepdims=True)
        s_next, acc = [], None
        for c, (kr, vr, rows) in enumerate(chunks):
            if hd + 1 < N_HEADS:
                qk = slice((hd + 1) * HEAD_PAD, (hd + 2) * HEAD_PAD)
                s_next.append(_dot_nt(q_ref[:, qk], kr[rows, qk]))
            p = jnp.exp2(s[c] - m).astype(BF16)
            part = _dot(p, vr[rows, hd * V_EXT:(hd + 1) * V_EXT])
            acc = part if acc is None else acc + part
        s = s_next
        o_ref[:, hd * V_DIM:(hd + 1) * V_DIM] = (acc[:, :V_DIM] / acc[:, V_DIM:]).astype(BF16)


def _attend(stream, layer, q, kcat, vext, lw, ctx, tq):
    n = stream.seq
    nq = n // tq
    has_ctx = stream.latent
    ins = [q, kcat, vext]
    specs = [pl.BlockSpec((tq, QK_W), lambda b, i: (b * nq + i, 0)),
             pl.BlockSpec((n, QK_W), lambda b, i: (b, 0)),
             pl.BlockSpec((n, VEXT_W), lambda b, i: (b, 0))]
    scratch = []
    if has_ctx:
        cache_ckv, cache_kr = ctx
        past = cache_ckv.shape[2]
        ins += [cache_ckv, cache_kr, lw["wkv"]]
        specs += [pl.BlockSpec((None, None, past, KV_RANK), lambda b, i: (b, layer, 0, 0)),
                  pl.BlockSpec((None, None, past, LANES), lambda b, i: (b, layer, 0, 0)),
                  _const_spec(lw["wkv"].shape)]
        scratch = [pltpu.VMEM((past, QK_W), BF16), pltpu.VMEM((past, VEXT_W), BF16)]
    return pl.pallas_call(
        functools.partial(_attend_body, has_ctx),
        grid=(stream.batch, nq),
        in_specs=specs,
        out_specs=pl.BlockSpec((tq, MLA_W), lambda b, i: (b * nq + i, 0)),
        out_shape=jax.ShapeDtypeStruct((stream.tokens, MLA_W), BF16),
        scratch_shapes=scratch,
        compiler_params=_params("parallel", "arbitrary"),
        name="attend",
    )(*ins)


MIX_PARTS = 4


def _mix_out_body(route, *refs):
    if route:
        (co_ref, mo_ref, wo1_ref, wo2_ref, x_ref, ga_ref, scf_ref, shf_ref, g_ref, b_ref,
         rcat_ref, x1_ref, h2_ref, meta_ref, cnt_ref) = refs
    else:
        co_ref, mo_ref, wo1_ref, wo2_ref, x_ref, ga_ref, g_ref, b_ref, x1_ref = refs
    rp = x_ref.shape[0] // MIX_PARTS
    parts = [slice(k * rp, (k + 1) * rp) for k in range(MIX_PARTS)]
    mix = [_dot(co_ref[r], wo1_ref[...]) + _dot(mo_ref[r], wo2_ref[...]) for r in parts]
    x1 = [_layernorm(ALPHA * x_ref[r] + ga_ref[...] * mix[k], g_ref[...], b_ref[...])
          for k, r in enumerate(parts)]
    for k, r in enumerate(parts):
        x1_ref[r] = x1[k]
    if not route:
        return
    h2 = [v * (1.0 + scf_ref[...]) + shf_ref[...] for v in x1]
    hh = [v.astype(BF16) for v in h2]
    for k, r in enumerate(parts):
        h2_ref[r] = hh[k]
    hl = [(h2[k] - hh[k].astype(F32)).astype(BF16) for k in range(MIX_PARTS)]
    ra = [_dot(v, rcat_ref[...]) for v in hh]
    rb = [_dot(v, rcat_ref[...]) for v in hl]
    logits = [ra[k][:, :LANES] + (ra[k][:, LANES:] + (rb[k][:, :LANES] + rb[k][:, LANES:]))
              for k in range(MIX_PARTS)]
    lane = lax.broadcasted_iota(jnp.int32, (rp, LANES), 1).astype(F32)
    neg = -jnp.inf
    lg = [jnp.where(lane < N_EXPERTS, v, neg) for v in logits]
    m1 = [jnp.max(v, axis=-1, keepdims=True) for v in lg]
    i1 = [jnp.min(jnp.where(lg[k] == m1[k], lane, float(LANES)), axis=-1, keepdims=True)
          for k in range(MIX_PARTS)]
    lg2 = [jnp.where(lane == i1[k], neg, lg[k]) for k in range(MIX_PARTS)]
    m2 = [jnp.max(v, axis=-1, keepdims=True) for v in lg2]
    i2 = [jnp.min(jnp.where(lg2[k] == m2[k], lane, float(LANES)), axis=-1, keepdims=True)
          for k in range(MIX_PARTS)]
    picked = [jnp.where(lane == i1[k], 1.0, jnp.where(lane == i2[k], 1.0, 0.0)) for k in range(MIX_PARTS)]
    earlier = jnp.where(lax.broadcasted_iota(jnp.int32, (rp, rp), 0)
                        > lax.broadcasted_iota(jnp.int32, (rp, rp), 1), 1.0, 0.0).astype(BF16)
    before = jnp.zeros((1, LANES), F32)
    for k, r in enumerate(parts):
        rank = _dot(earlier, picked[k].astype(BF16)) + before
        before = before + jnp.sum(picked[k], axis=0, keepdims=True)
        rank1 = jnp.sum(jnp.where(lane == i1[k], rank, 0.0), axis=-1, keepdims=True)
        rank2 = jnp.sum(jnp.where(lane == i2[k], rank, 0.0), axis=-1, keepdims=True)
        e2 = jnp.exp(m2[k] - m1[k])
        den = 1.0 + e2
        meta = jnp.zeros((rp, LANES), F32)
        for j, val in enumerate((i1[k], i2[k], 1.0 / den, e2 / den, rank1, rank2)):
            meta = jnp.where(lane == float(j), val, meta)
        meta_ref[r] = meta
    cnt_ref[...] = before


def _mix_out(stream, layer, co, mo, x, mod, lw, route, tm):
    t = stream.tokens
    row = lambda w: pl.BlockSpec((tm, w), lambda i: (i, 0))
    vec = _const_spec((1, D_MODEL))
    ins = [co, mo, lw["wo_chunk"], lw["wo_mla"], x, mod]
    specs = [row(CHUNK_W), row(MLA_W), _const_spec(lw["wo_chunk"].shape), _const_spec(lw["wo_mla"].shape),
             row(D_MODEL), stream.mod_spec(layer, G_A, tm)]
    if route:
        ins += [mod, mod]
        specs += [stream.mod_spec(layer, SC_F, tm), stream.mod_spec(layer, SH_F, tm)]
    ins += [lw["ln_mix_g"], lw["ln_mix_b"]]
    specs += [vec, vec]
    out_shape = [jax.ShapeDtypeStruct((t, D_MODEL), F32)]
    out_specs = [row(D_MODEL)]
    if route:
        ins.append(lw["router"])
        specs.append(_const_spec(lw["router"].shape))
        out_shape += [jax.ShapeDtypeStruct((t, D_MODEL), BF16), jax.ShapeDtypeStruct((t, LANES), F32),
                      jax.ShapeDtypeStruct((t // tm, 1, LANES), F32)]
        out_specs += [row(D_MODEL), row(LANES), pl.BlockSpec((None, 1, LANES), lambda i: (i, 0, 0))]
    return pl.pallas_call(
        functools.partial(_mix_out_body, route),
        grid=(t // tm,),
        in_specs=specs,
        out_specs=out_specs,
        out_shape=out_shape,
        compiler_params=_params("parallel"),
        name="mix_out",
    )(*ins)


def _ffn_body(x_ref, scf_ref, shf_ref, wg_ref, wu_ref, wd_ref, gf_ref, g_ref, b_ref, o_ref):
    h = (x_ref[...] * (1.0 + scf_ref[...]) + shf_ref[...]).astype(BF16)
    gate = _dot(h, wg_ref[...])
    up = _dot(h, wu_ref[...])
    a = (gate * _sigmoid(gate) * up).astype(BF16)
    f = _dot(a, wd_ref[...])
    o_ref[...] = _layernorm(ALPHA * x_ref[...] + gf_ref[...] * f, g_ref[...], b_ref[...])


def _ffn(stream, layer, x1, mod, lw, tm):
    t = stream.tokens
    row = lambda w: pl.BlockSpec((tm, w), lambda i: (i, 0))
    vec = _const_spec((1, D_MODEL))
    resident = lambda a: pl.BlockSpec(a.shape, lambda i: (0, 0), pipeline_mode=pl.Buffered(1))
    return pl.pallas_call(
        _ffn_body,
        grid=(t // tm,),
        in_specs=[row(D_MODEL), stream.mod_spec(layer, SC_F, tm), stream.mod_spec(layer, SH_F, tm),
                  resident(lw["ffn_gate"]), resident(lw["ffn_up"]), resident(lw["ffn_down"]),
                  stream.mod_spec(layer, G_F, tm), vec, vec],
        out_specs=row(D_MODEL),
        out_shape=jax.ShapeDtypeStruct((t, D_MODEL), F32),
        compiler_params=_params("parallel"),
        name="ffn_dense",
    )(x1, mod, mod, lw["ffn_gate"], lw["ffn_up"], lw["ffn_down"], mod, lw["ln_ffn_g"], lw["ln_ffn_b"])


RUN_ALIGN = 16


def _plan_body(cap, tile, cnt_ref, off_ref, run_ref, base_ref, te_ref, tb_ref, rows_ref, nv_ref):
    nw = cnt_ref.shape[0]
    lane = lax.broadcasted_iota(jnp.int32, (1, LANES), 1).astype(F32)
    run = jnp.floor((cnt_ref[...] + (RUN_ALIGN - 1)) * (1.0 / RUN_ALIGN)) * RUN_ALIGN
    before = jnp.where(lax.broadcasted_iota(jnp.int32, (LANES, LANES), 0)
                       < lax.broadcasted_iota(jnp.int32, (LANES, LANES), 1), 1.0, 0.0).astype(BF16)
    upto = jnp.where(lax.broadcasted_iota(jnp.int32, (LANES, LANES), 0)
                     <= lax.broadcasted_iota(jnp.int32, (LANES, LANES), 1), 1.0, 0.0).astype(BF16)
    above = jnp.where(lax.broadcasted_iota(jnp.int32, (nw, nw), 0)
                      > lax.broadcasted_iota(jnp.int32, (nw, nw), 1), 1.0, 0.0).astype(BF16)
    run_b = run.astype(BF16)
    off_ref[...] = _dot(run_b, before).astype(jnp.int32)
    run_ref[...] = run.astype(jnp.int32)
    base_ref[...] = (lane * cap + _dot(above, run_b)).astype(jnp.int32)
    totals = jnp.sum(run, axis=0, keepdims=True)
    tiles = jnp.floor((totals + (tile - 1)) * (1.0 / tile))
    cum = _dot(tiles.astype(BF16), upto)
    start = cum - tiles
    n_valid = jnp.max(cum, axis=-1, keepdims=True)
    step = jnp.minimum(lane, n_valid - 1.0)
    pick = lambda v, e: jnp.sum(jnp.where(lane == float(e), v, 0.0), axis=-1, keepdims=True)
    te = jnp.zeros_like(step)
    for e in range(N_EXPERTS):
        te = te + jnp.where(pick(cum, e) <= step, 1.0, 0.0)
    start_te, total_te = jnp.zeros_like(step), jnp.zeros_like(step)
    for e in range(N_EXPERTS):
        start_te = jnp.where(te == float(e), pick(start, e), start_te)
        total_te = jnp.where(te == float(e), pick(totals, e), total_te)
    tj = step - start_te
    te_ref[...] = te.astype(jnp.int32)
    tb_ref[...] = (te * (cap // tile) + tj).astype(jnp.int32)
    rows_ref[...] = jnp.minimum(float(tile), total_te - tj * tile).astype(jnp.int32)
    nv_ref[...] = jnp.broadcast_to(n_valid, (1, LANES)).astype(jnp.int32)


def _route_plan(cnt, tokens, tm, tile):
    nw = tokens // tm
    worst_pad = (RUN_ALIGN - 1) * nw
    cap = pl.cdiv(tokens + worst_pad, tile) * tile
    n_tiles = (2 * tokens + N_EXPERTS * worst_pad) // tile + N_EXPERTS
    assert n_tiles <= LANES and tm % RUN_ALIGN == 0
    per_run = jax.ShapeDtypeStruct((nw, LANES), jnp.int32)
    per_step = jax.ShapeDtypeStruct((1, LANES), jnp.int32)
    off, run, base, te, tb, rows, nv = pl.pallas_call(
        functools.partial(_plan_body, cap, tile),
        out_shape=[per_run, per_run, per_run, per_step, per_step, per_step, per_step],
        name="moe_plan",
    )(cnt.reshape(nw, LANES))
    return dict(
        tm=tm, nw=nw, cap=cap, tile=tile, n_tiles=n_tiles, buf_rows=2 * tm + N_EXPERTS * RUN_ALIGN,
        off=off.reshape(-1), run=run.reshape(-1), base=base.reshape(-1),
        tile_expert=te.reshape(-1), tile_block=tb.reshape(-1), tile_rows=rows.reshape(-1),
        n_valid=nv.reshape(-1))


def _run_piece_sizes(tm):
    sizes, b = [], RUN_ALIGN
    while b <= tm:
        sizes.append(b)
        b *= 2
    return sizes


def _row_run_copies(src, dst, src_off, dst_off, n_rows, sem, sizes, action):
    for b in sizes:
        done = n_rows & (-2 * b)
        cp = pltpu.make_async_copy(
            src.at[pl.ds(pl.multiple_of(src_off + done, RUN_ALIGN), b)],
            dst.at[pl.ds(pl.multiple_of(dst_off + done, RUN_ALIGN), b)], sem)

        @pl.when((n_rows & b) != 0)
        def _():
            action(cp)


def _local_rows(meta, off_ref, w):
    i1, i2 = meta[:, 0:1], meta[:, 1:2]
    o1, o2 = jnp.zeros_like(i1), jnp.zeros_like(i2)
    for e in range(N_EXPERTS):
        off_e = off_ref[w * LANES + e].astype(F32)
        o1 = jnp.where(i1 == float(e), off_e, o1)
        o2 = jnp.where(i2 == float(e), off_e, o2)
    return (o1 + meta[:, 4:5]).astype(jnp.int32), (o2 + meta[:, 5:6]).astype(jnp.int32)


def _start(cp):
    cp.start()


def _wait(cp):
    cp.wait()


def _dispatch_body(tm, nw, off_ref, run_ref, base_ref, h_ref, meta_ref, xs_ref, xw_ref, sems):
    w = pl.program_id(0)
    slot = w % 2
    sizes = _run_piece_sizes(tm)

    def push(win, buf, action):
        for e in range(N_EXPERTS):
            k = win * LANES + e
            _row_run_copies(xw_ref.at[buf], xs_ref, off_ref[k], base_ref[k], run_ref[k],
                            sems.at[buf], sizes, action)

    r1, r2 = _local_rows(meta_ref[...], off_ref, w)
    col = lax.broadcasted_iota(jnp.int32, (tm, xw_ref.shape[1]), 1)
    sel = jnp.where(col == r1, 1.0, jnp.where(col == r2, 1.0, 0.0)).astype(BF16)
    xw = lax.dot_general(sel, h_ref[...], (((0,), (0,)), ((), ())), preferred_element_type=F32)
    xw_ref[slot] = xw.astype(BF16)
    push(w, slot, _start)

    @pl.when(w > 0)
    def _():
        push(w - 1, 1 - slot, _wait)

    @pl.when(w == nw - 1)
    def _():
        push(w, slot, _wait)


def _dispatch(h2, meta, plan):
    tm = plan["tm"]
    return pl.pallas_call(
        functools.partial(_dispatch_body, tm, plan["nw"]),
        grid_spec=pltpu.PrefetchScalarGridSpec(
            num_scalar_prefetch=3,
            grid=(plan["nw"],),
            in_specs=[pl.BlockSpec((tm, D_MODEL), lambda i, *_: (i, 0)),
                      pl.BlockSpec((tm, LANES), lambda i, *_: (i, 0))],
            out_specs=pl.BlockSpec(memory_space=pl.ANY),
            scratch_shapes=[pltpu.VMEM((2, plan["buf_rows"], D_MODEL), BF16),
                            pltpu.SemaphoreType.DMA((2,))]),
        out_shape=jax.ShapeDtypeStruct((N_EXPERTS * plan["cap"], D_MODEL), BF16),
        compiler_params=_params("arbitrary"),
        name="moe_dispatch",
    )(plan["off"], plan["run"], plan["base"], h2, meta)


def _experts_body(te_ref, tb_ref, rows_ref, nv_ref, x_ref, wg_ref, wu_ref, wd_ref, o_ref):
    i = pl.program_id(0)

    @pl.when(i < nv_ref[0])
    def _():
        x = x_ref[...]
        live = lax.broadcasted_iota(jnp.int32, (x.shape[0], 1), 0) < rows_ref[i]
        x = jnp.where(live, x, jnp.zeros_like(x))
        gate = _dot(x, wg_ref[...])
        up = _dot(x, wu_ref[...])
        a = (gate * _sigmoid(gate) * up).astype(BF16)
        o_ref[...] = _dot(a, wd_ref[...]).astype(BF16)


def _experts(xs, plan, lw):
    dff = lw["moe_gate"].shape[2]
    rows = pl.BlockSpec((plan["tile"], D_MODEL), lambda i, te, tb, *_: (tb[i], 0))
    return pl.pallas_call(
        _experts_body,
        grid_spec=pltpu.PrefetchScalarGridSpec(
            num_scalar_prefetch=4,
            grid=(plan["n_tiles"],),
            in_specs=[rows,
                      pl.BlockSpec((None, D_MODEL, dff), lambda i, te, *_: (te[i], 0, 0)),
                      pl.BlockSpec((None, D_MODEL, dff), lambda i, te, *_: (te[i], 0, 0)),
                      pl.BlockSpec((None, dff, D_MODEL), lambda i, te, *_: (te[i], 0, 0))],
            out_specs=rows),
        out_shape=jax.ShapeDtypeStruct(xs.shape, BF16),
        compiler_params=_params("arbitrary"),
        name="moe_experts",
    )(plan["tile_expert"], plan["tile_block"], plan["tile_rows"], plan["n_valid"],
      xs, lw["moe_gate"], lw["moe_up"], lw["moe_down"])


def _combine_body(tm, nw, off_ref, run_ref, base_ref, ys_ref, meta_ref, x_ref, gf_ref, g_ref, b_ref,
                  o_ref, yw_ref, sems):
    w = pl.program_id(0)
    slot = w % 2
    sizes = _run_piece_sizes(tm)

    def fetch(win, buf, action):
        for e in range(N_EXPERTS):
            k = win * LANES + e
            _row_run_copies(ys_ref, yw_ref.at[buf], base_ref[k], off_ref[k], run_ref[k],
                            sems.at[buf], sizes, action)

    @pl.when(w == 0)
    def _():
        yw_ref[...] = jnp.zeros_like(yw_ref)
        fetch(0, 0, _start)

    @pl.when(w + 1 < nw)
    def _():
        fetch(w + 1, 1 - slot, _start)

    fetch(w, slot, _wait)
    rp = tm // MIX_PARTS
    parts = [slice(k * rp, (k + 1) * rp) for k in range(MIX_PARTS)]
    meta = [meta_ref[r] for r in parts]
    picks = [_local_rows(v, off_ref, w) for v in meta]
    col = lax.broadcasted_iota(jnp.int32, (rp, yw_ref.shape[1]), 1)
    yw = yw_ref[slot]
    gate = [jnp.where(col == r1, v[:, 2:3], jnp.where(col == r2, v[:, 3:4], 0.0)).astype(BF16)
            for v, (r1, r2) in zip(meta, picks)]
    f = [_dot(g, yw) for g in gate]
    for k, r in enumerate(parts):
        o_ref[r] = _layernorm(ALPHA * x_ref[r] + gf_ref[...] * f[k], g_ref[...], b_ref[...])


def _combine(stream, layer, ys, meta, x1, mod, plan, lw):
    tm = plan["tm"]
    row = lambda width: pl.BlockSpec((tm, width), lambda i, *_: (i, 0))
    vec = pl.BlockSpec((1, D_MODEL), lambda i, *_: (0, 0))
    mod_spec = stream.mod_spec(layer, G_F, tm)
    mod_spec = pl.BlockSpec(mod_spec.block_shape, lambda i, *_, f=mod_spec.index_map: f(i))
    return pl.pallas_call(
        functools.partial(_combine_body, tm, plan["nw"]),
        grid_spec=pltpu.PrefetchScalarGridSpec(
            num_scalar_prefetch=3,
            grid=(plan["nw"],),
            in_specs=[pl.BlockSpec(memory_space=pl.ANY), row(LANES), row(D_MODEL), mod_spec, vec, vec],
            out_specs=row(D_MODEL),
            scratch_shapes=[pltpu.VMEM((2, plan["buf_rows"], D_MODEL), BF16),
                            pltpu.SemaphoreType.DMA((2,))]),
        out_shape=jax.ShapeDtypeStruct((stream.tokens, D_MODEL), F32),
        compiler_params=_params("arbitrary"),
        name="moe_combine",
    )(plan["off"], plan["run"], plan["base"], ys, meta, x1, mod, lw["ln_ffn_g"], lw["ln_ffn_b"])


def _rope_tables(n_tokens):
    rows = n_tokens // GRID_W
    row = np.repeat(np.arange(rows, dtype=np.float64), GRID_W)
    col = np.tile(np.arange(GRID_W, dtype=np.float64), rows)
    half = ROPE_DIM // 2
    inv_freq = ROPE_BASE ** (-np.arange(0, half, 2, dtype=np.float64) / half)
    ar = row[:, None] * inv_freq[None, :]
    ac = col[:, None] * inv_freq[None, :]
    cos = np.concatenate([np.cos(ar), np.cos(ar), np.cos(ac), np.cos(ac)], axis=-1)
    sin = np.concatenate([-np.sin(ar), np.sin(ar), -np.sin(ac), np.sin(ac)], axis=-1)
    return jnp.asarray(np.tile(cos, (1, N_HEADS)), F32), jnp.asarray(np.tile(sin, (1, N_HEADS)), F32)


def _swap_rope_halves(a):
    q = ROPE_DIM // 4
    return jnp.concatenate([a[..., q:2 * q], a[..., :q], a[..., 3 * q:], a[..., 2 * q:3 * q]], axis=-1)


def _layer_weights(l, p):
    w_in = p["w_in"][l].astype(BF16)
    o_kr = 2 * CHUNK_W + Q_RANK + KV_RANK
    zpad = jnp.zeros((D_MODEL, LANES - ROPE_DIM), BF16)
    w_in_ext = jnp.concatenate(
        [w_in, zpad, _swap_rope_halves(w_in[:, o_kr:o_kr + ROPE_DIM]), zpad], axis=1)

    w_uq = p["w_uq"][l].astype(BF16).reshape(Q_RANK, N_HEADS, QK_NOPE + ROPE_DIM)
    wq_rope = w_uq[:, :, QK_NOPE:]
    wq = jnp.concatenate([w_uq[:, :, :QK_NOPE].reshape(Q_RANK, N_HEADS * QK_NOPE),
                          wq_rope.reshape(Q_RANK, N_HEADS * ROPE_DIM),
                          _swap_rope_halves(wq_rope).reshape(Q_RANK, N_HEADS * ROPE_DIM)], axis=1)

    w_ukv = p["w_ukv"][l].astype(BF16).reshape(KV_RANK, N_HEADS, QK_NOPE + V_DIM)
    wkv = jnp.concatenate([w_ukv[:, :, :QK_NOPE].reshape(KV_RANK, N_HEADS * QK_NOPE),
                           w_ukv[:, :, QK_NOPE:].reshape(KV_RANK, MLA_W)], axis=1)
    w_out = p["w_out"][l]
    lw = {
        "w_in": w_in_ext,
        "q_g": p["q_norm_g"][l].reshape(1, Q_RANK),
        "kv_g": p["kv_norm_g"][l].reshape(1, KV_RANK),
        "wq": wq,
        "wkv": wkv,
        "ln_v_g": p["chunk_ln_g"][l].reshape(1, CHUNK_W),
        "w_s": p["w_spatial"][l].astype(BF16),
        "b_s": jnp.broadcast_to(p["b_spatial"][l][:, :, None], (N_GROUPS, CHUNK, GROUP_W)),
        "wo_chunk": w_out[:CHUNK_W].astype(BF16),
        "wo_mla": w_out[CHUNK_W:].astype(BF16),
        "ln_mix_g": p["ln_mix_g"][l].reshape(1, D_MODEL),
        "ln_mix_b": p["ln_mix_b"][l].reshape(1, D_MODEL),
        "ln_ffn_g": p["ln_ffn_g"][l].reshape(1, D_MODEL),
        "ln_ffn_b": p["ln_ffn_b"][l].reshape(1, D_MODEL),
    }
    if l % 2 == 0:
        lw["ffn_gate"] = p["ffn_w_gate"][l // 2].astype(BF16)
        lw["ffn_up"] = p["ffn_w_up"][l // 2].astype(BF16)
        lw["ffn_down"] = p["ffn_w_down"][l // 2].astype(BF16)
    else:
        r = jnp.pad(p["router_w"][l // 2], ((0, 0), (0, LANES - N_EXPERTS)))
        r_hi = r.astype(BF16)
        lw["router"] = jnp.concatenate([r_hi, (r - r_hi.astype(F32)).astype(BF16)], axis=1)
        lw["moe_gate"] = p["moe_w_gate"][l // 2].astype(BF16)
        lw["moe_up"] = p["moe_w_up"][l // 2].astype(BF16)
        lw["moe_down"] = p["moe_w_down"][l // 2].astype(BF16)
    return lw


def _trunk(stream, x, mod, weights, rope, ctx, tiles):
    caches = []
    for l in range(DEPTH):
        lw = weights[l]
        outs = _mixer_in(stream, l, x, mod, lw, rope, tiles["mixer_in"])
        co, q, kcat, v = outs[:4]
        if not stream.latent:
            caches.append(outs[4:])
        mo = _attend(stream, l, q, kcat, v, lw, ctx, tiles["attend"])
        route = l % 2 == 1
        outs = _mix_out(stream, l, co, mo, x, mod, lw, route, tiles["route" if route else "mix_out"])
        if route:
            x1, h2, meta, cnt = outs
            plan = _route_plan(cnt, stream.tokens, tiles["route"], tiles["experts"])
            ys = _experts(_dispatch(h2, meta, plan), plan, lw)
            x = _combine(stream, l, ys, meta, x1, mod, plan, lw)
        else:
            x = _ffn(stream, l, outs[0], mod, lw, tiles["ffn"])
    return x, caches


def kernel(x_prompt, x_sample, c, cache_ckv, cache_krope, c_ctx, w_mod, b_mod, w_in, q_norm_g, kv_norm_g, w_uq, w_ukv, chunk_ln_g, w_spatial, b_spatial, w_out, ln_mix_g, ln_mix_b, ln_ffn_g, ln_ffn_b, ffn_w_gate, ffn_w_up, ffn_w_down, router_w, moe_w_gate, moe_w_up, moe_w_down):
    p = dict(w_in=w_in, q_norm_g=q_norm_g, kv_norm_g=kv_norm_g, w_uq=w_uq, w_ukv=w_ukv,
             chunk_ln_g=chunk_ln_g, w_spatial=w_spatial, b_spatial=b_spatial, w_out=w_out,
             ln_mix_g=ln_mix_g, ln_mix_b=ln_mix_b, ln_ffn_g=ln_ffn_g, ln_ffn_b=ln_ffn_b,
             ffn_w_gate=ffn_w_gate, ffn_w_up=ffn_w_up, ffn_w_down=ffn_w_down, router_w=router_w,
             moe_w_gate=moe_w_gate, moe_w_up=moe_w_up, moe_w_down=moe_w_down)
    weights = [_layer_weights(l, p) for l in range(DEPTH)]

    batch, seq, _ = x_prompt.shape
    dec_batch, dec_seq, _ = x_sample.shape
    cond_rows = jnp.concatenate(
        [c_ctx[None, :], c, jnp.zeros((MOD_ROWS - 1 - dec_batch, D_MODEL), F32)], axis=0)
    mod = _modulation(cond_rows, w_mod, b_mod).reshape(DEPTH, MOD_ROWS, 6, 1, D_MODEL)

    prompt = _Stream(batch, seq, mod_row0=0, per_row_mod=False, latent=False)
    sample = _Stream(dec_batch, dec_seq, mod_row0=1, per_row_mod=True, latent=True)

    y_prompt, caches = _trunk(
        prompt, x_prompt.reshape(batch * seq, D_MODEL), mod, weights, None, None,
        dict(mixer_in=256, attend=256, mix_out=512, route=512, experts=256, ffn=512))
    new_ckv = jnp.stack([ck.reshape(batch, seq, KV_RANK) for ck, _ in caches], axis=1)
    new_krope = jnp.stack([kr.reshape(batch, seq, ROPE_DIM) for _, kr in caches], axis=1)

    rope = _rope_tables(dec_seq)
    ctx = (cache_ckv, jnp.pad(cache_krope, ((0, 0), (0, 0), (0, 0), (0, LANES - ROPE_DIM))))
    y_sample, _ = _trunk(
        sample, x_sample.reshape(dec_batch * dec_seq, D_MODEL), mod, weights, rope, ctx,
        dict(mixer_in=1024, attend=1024, mix_out=1024, route=512, experts=512, ffn=512))
    return (y_prompt.reshape(batch, seq, D_MODEL), y_sample.reshape(dec_batch, dec_seq, D_MODEL),
            new_ckv, new_krope)
```

```python
import functools
import math

import jax
import jax.numpy as jnp
import numpy as np
from jax import lax
from jax.experimental import pallas as pl
from jax.experimental.pallas import tpu as pltpu

F32 = jnp.float32
BF16 = jnp.bfloat16

D_MODEL = 1024
DEPTH = 2
GRID_W = 64
CHUNK = 128
N_GROUPS = 4
GROUP_W = 128
CHUNK_W = N_GROUPS * GROUP_W
N_HEADS = 4
QK_NOPE = 128
ROPE_DIM = 64
V_DIM = 128
Q_RANK = 384
KV_RANK = 256
MLA_W = N_HEADS * V_DIM
HEAD_PAD = 256
QK_W = N_HEADS * HEAD_PAD
V_EXT = 2 * V_DIM
VEXT_W = N_HEADS * V_EXT
ROPE_BASE = 10000.0
N_EXPERTS = 8
ALPHA = (2 * DEPTH) ** 0.25
EPS = 1e-6
ATTN_SCALE = math.log2(math.e) / math.sqrt(QK_NOPE + ROPE_DIM)
MOD_ROWS = 16
LANES = 128
VMEM_LIMIT = 56 * 1024 * 1024

SH_A, SC_A, G_A, SH_F, SC_F, G_F = range(6)


def _sigmoid(x):
    return 1.0 / (1.0 + jnp.exp(-x))


def _gelu_tanh(x):
    return 0.5 * x * (1.0 + jnp.tanh(math.sqrt(2.0 / math.pi) * (x + 0.044715 * (x * x * x))))


def _layernorm(y, g, b):
    mu = jnp.mean(y, axis=-1, keepdims=True)
    d = y - mu
    var = jnp.mean(d * d, axis=-1, keepdims=True)
    return d * lax.rsqrt(var + EPS) * g + b


def _rmsnorm(y, g):
    return y * lax.rsqrt(jnp.mean(y * y, axis=-1, keepdims=True) + EPS) * g


def _dot(a, b):
    return jnp.dot(a, b, preferred_element_type=F32)


def _dot_nt(a, b):
    return lax.dot_general(a, b, (((1,), (1,)), ((), ())), preferred_element_type=F32)


def _params(*sem):
    return pltpu.CompilerParams(dimension_semantics=sem, vmem_limit_bytes=VMEM_LIMIT)


def _const_spec(shape):
    nd = len(shape)
    return pl.BlockSpec(shape, lambda *_: (0,) * nd)


class _Stream:
    def __init__(self, batch, seq, mod_row0, per_row_mod, latent):
        self.batch = batch
        self.seq = seq
        self.tokens = batch * seq
        self.mod_row0 = mod_row0
        self.per_row_mod = per_row_mod
        self.latent = latent

    def mod_spec(self, layer, which, tm):
        tiles_per_row = self.seq // tm
        row0, per_row = self.mod_row0, self.per_row_mod

        def index(i):
            r = row0 + (i // tiles_per_row if per_row else 0)
            return (layer, r, which, 0, 0)

        return pl.BlockSpec((None, None, None, 1, D_MODEL), index)


def _mod_body(c_ref, w_ref, b_ref, o_ref):
    a = c_ref[...]
    a = (a * _sigmoid(a)).astype(BF16)
    o_ref[...] = _dot(a, w_ref[...].astype(BF16)) + b_ref[...]


def _modulation(cond_rows, w_mod, b_mod):
    depth, _, width = w_mod.shape
    tn = 1536
    return pl.pallas_call(
        _mod_body,
        grid=(depth, width // tn),
        in_specs=[
            pl.BlockSpec((MOD_ROWS, D_MODEL), lambda l, j: (0, 0)),
            pl.BlockSpec((None, D_MODEL, tn), lambda l, j: (l, 0, j)),
            pl.BlockSpec((None, 1, tn), lambda l, j: (l, 0, j)),
        ],
        out_specs=pl.BlockSpec((None, MOD_ROWS, tn), lambda l, j: (l, 0, j)),
        out_shape=jax.ShapeDtypeStruct((depth, MOD_ROWS, width), F32),
        compiler_params=_params("parallel", "parallel"),
        name="modulation",
    )(cond_rows, w_mod, b_mod.reshape(depth, 1, width))


def _store_keys_values(kv, kr, kcat_ref, vext_ref):
    ones = jnp.ones((kv.shape[0], V_EXT - V_DIM), BF16)
    for hd in range(N_HEADS):
        a = hd * HEAD_PAD
        kcat_ref[:, a:a + QK_NOPE] = kv[:, hd * QK_NOPE:(hd + 1) * QK_NOPE].astype(BF16)
        kcat_ref[:, a + QK_NOPE:a + HEAD_PAD] = kr
        b = hd * V_EXT
        vext_ref[:, b:b + V_DIM] = kv[:, (N_HEADS + hd) * V_DIM:(N_HEADS + hd + 1) * V_DIM].astype(BF16)
        vext_ref[:, b + V_DIM:b + V_EXT] = ones


def _mixer_in_body(latent, tm, *refs):
    it = iter(refs)
    x_ref, sc_ref, sh_ref, win_ref, qg_ref, kvg_ref, wq_ref = (next(it) for _ in range(7))
    wkv_ref, lng_ref, ws_ref, bs_ref = (next(it) for _ in range(4))
    cos_ref, sin_ref = (next(it), next(it)) if latent else (None, None)
    co_ref, q_ref, kcat_ref, vext_ref = (next(it) for _ in range(4))
    ckv_ref, kr_ref = (None, None) if latent else (next(it), next(it))

    h = (x_ref[...] * (1.0 + sc_ref[...]) + sh_ref[...]).astype(BF16)
    p = _dot(h, win_ref[...])

    for g in range(N_GROUPS):
        cols = slice(g * GROUP_W, (g + 1) * GROUP_W)
        vg = _gelu_tanh(p[:, CHUNK_W + g * GROUP_W:CHUNK_W + (g + 1) * GROUP_W])
        mu = jnp.mean(vg, axis=-1, keepdims=True)
        d = vg - mu
        var = jnp.mean(d * d, axis=-1, keepdims=True)
        vn = (d * lax.rsqrt(var + EPS) * lng_ref[:, cols]).astype(BF16)
        ug = _gelu_tanh(p[:, cols])
        for c in range(tm // CHUNK):
            rows = slice(c * CHUNK, (c + 1) * CHUNK)
            z = _dot(ws_ref[g], vn[rows]) + bs_ref[g]
            co_ref[rows, cols] = (ug[rows] * z).astype(BF16)

    o = 2 * CHUNK_W
    cqn = _rmsnorm(p[:, o:o + Q_RANK], qg_ref[...]).astype(BF16)
    qc = _dot(cqn, wq_ref[...])
    q_rope = qc[:, N_HEADS * QK_NOPE:N_HEADS * (QK_NOPE + ROPE_DIM)]
    if latent:
        cos = cos_ref[...]
        sin = sin_ref[...]
        q_rope = q_rope * cos + qc[:, N_HEADS * (QK_NOPE + ROPE_DIM):] * sin
    for hd in range(N_HEADS):
        a = hd * HEAD_PAD
        q_ref[:, a:a + QK_NOPE] = (qc[:, hd * QK_NOPE:(hd + 1) * QK_NOPE] * ATTN_SCALE).astype(BF16)
        pair = q_rope[:, (hd // 2) * LANES:(hd // 2 + 1) * LANES]
        if hd % 2:
            pair = pltpu.roll(pair, ROPE_DIM, axis=1)
        q_ref[:, a + QK_NOPE:a + HEAD_PAD] = (pair * ATTN_SCALE).astype(BF16)

    o += Q_RANK
    ckvn = _rmsnorm(p[:, o:o + KV_RANK], kvg_ref[...])
    o += KV_RANK
    kr = p[:, o:o + LANES]
    if not latent:
        ckv_ref[...] = ckvn
        kr_ref[...] = kr[:, :ROPE_DIM]
    else:
        kr = kr * cos[:, :LANES] + p[:, o + LANES:o + 2 * LANES] * sin[:, :LANES]
    kr = kr.astype(BF16)
    kv = _dot(ckvn.astype(BF16), wkv_ref[...])
    _store_keys_values(kv, kr, kcat_ref, vext_ref)


def _mixer_in(stream, layer, x, mod, lw, rope, tm):
    latent = stream.latent
    t = stream.tokens
    row = lambda w: pl.BlockSpec((tm, w), lambda i: (i, 0))
    win = lw["w_in"]
    win_cols = win.shape[1] if latent else win.shape[1] - LANES
    wq = lw["wq"]
    wq_cols = wq.shape[1] if latent else wq.shape[1] - N_HEADS * ROPE_DIM
    ins = [x, mod, mod, win, lw["q_g"], lw["kv_g"], wq]
    specs = [row(D_MODEL), stream.mod_spec(layer, SC_A, tm), stream.mod_spec(layer, SH_A, tm),
             _const_spec((D_MODEL, win_cols)), _const_spec(lw["q_g"].shape), _const_spec(lw["kv_g"].shape),
             _const_spec((Q_RANK, wq_cols))]
    for name in ("wkv", "ln_v_g", "w_s", "b_s"):
        ins.append(lw[name])
        specs.append(_const_spec(lw[name].shape))
    if latent:
        tiles_per_seq = stream.seq // tm
        pos = pl.BlockSpec((tm, N_HEADS * ROPE_DIM), lambda i: (i % tiles_per_seq, 0))
        ins += [rope[0], rope[1]]
        specs += [pos, pos]
    out_shape = [jax.ShapeDtypeStruct((t, CHUNK_W), BF16), jax.ShapeDtypeStruct((t, QK_W), BF16),
                 jax.ShapeDtypeStruct((t, QK_W), BF16), jax.ShapeDtypeStruct((t, VEXT_W), BF16)]
    out_specs = [row(CHUNK_W), row(QK_W), row(QK_W), row(VEXT_W)]
    if not latent:
        out_shape += [jax.ShapeDtypeStruct((t, KV_RANK), F32), jax.ShapeDtypeStruct((t, ROPE_DIM), F32)]
        out_specs += [row(KV_RANK), row(ROPE_DIM)]
    return pl.pallas_call(
        functools.partial(_mixer_in_body, latent, tm),
        grid=(t // tm,),
        in_specs=specs,
        out_specs=out_specs,
        out_shape=out_shape,
        compiler_params=_params("parallel"),
        name="mixer_in",
    )(*ins)


KEY_CHUNK = 256


def _attend_body(has_ctx, *refs):
    if has_ctx:
        q_ref, k_ref, vext_ref, cckv_ref, ckr_ref, wkv_ref, o_ref, kctx_ref, vctx_ref = refs

        @pl.when(pl.program_id(1) == 0)
        def _():
            kv = _dot(cckv_ref[...].astype(BF16), wkv_ref[...])
            _store_keys_values(kv, ckr_ref[...].astype(BF16), kctx_ref, vctx_ref)

        sources = [(kctx_ref, vctx_ref), (k_ref, vext_ref)]
    else:
        q_ref, k_ref, vext_ref, o_ref = refs
        sources = [(k_ref, vext_ref)]
    chunks = [(kr, vr, slice(c * KEY_CHUNK, (c + 1) * KEY_CHUNK))
              for kr, vr in sources for c in range(kr.shape[0] // KEY_CHUNK)]

    def scores(hd):
        qk = slice(hd * HEAD_PAD, (hd + 1) * HEAD_PAD)
        return [_dot_nt(q_ref[:, qk], kr[rows, qk]) for kr, _, rows in chunks]

    s = scores(0)
    for hd in range(N_HEADS):
        m = jnp.max(functools.reduce(jnp.maximum, s), axis=-1, keepdims=True)
        s_next, acc = [], None
        for c, (kr, vr, rows) in enumerate(chunks):
            if hd + 1 < N_HEADS:
                qk = slice((hd + 1) * HEAD_PAD, (hd + 2) * HEAD_PAD)
                s_next.append(_dot_nt(q_ref[:, qk], kr[rows, qk]))
            p = jnp.exp2(s[c] - m).astype(BF16)
            part = _dot(p, vr[rows, hd * V_EXT:(hd + 1) * V_EXT])
            acc = part if acc is None else acc + part
        s = s_next
        o_ref[:, hd * V_DIM:(hd + 1) * V_DIM] = (acc[:, :V_DIM] / acc[:, V_DIM:]).astype(BF16)


def _attend(stream, layer, q, kcat, vext, lw, ctx, tq):
    n = stream.seq
    nq = n // tq
    has_ctx = stream.latent
    ins = [q, kcat, vext]
    specs = [pl.BlockSpec((tq, QK_W), lambda b, i: (b * nq + i, 0)),
             pl.BlockSpec((n, QK_W), lambda b, i: (b, 0)),
             pl.BlockSpec((n, VEXT_W), lambda b, i: (b, 0))]
    scratch = []
    if has_ctx:
        cache_ckv, cache_kr = ctx
        past = cache_ckv.shape[2]
        ins += [cache_ckv, cache_kr, lw["wkv"]]
        specs += [pl.BlockSpec((None, None, past, KV_RANK), lambda b, i: (b, layer, 0, 0)),
                  pl.BlockSpec((None, None, past, LANES), lambda b, i: (b, layer, 0, 0)),
                  _const_spec(lw["wkv"].shape)]
        scratch = [pltpu.VMEM((past, QK_W), BF16), pltpu.VMEM((past, VEXT_W), BF16)]
    return pl.pallas_call(
        functools.partial(_attend_body, has_ctx),
        grid=(stream.batch, nq),
        in_specs=specs,
        out_specs=pl.BlockSpec((tq, MLA_W), lambda b, i: (b * nq + i, 0)),
        out_shape=jax.ShapeDtypeStruct((stream.tokens, MLA_W), BF16),
        scratch_shapes=scratch,
        compiler_params=_params("parallel", "arbitrary"),
        name="attend",
    )(*ins)


MIX_PARTS = 4


def _mixer_residual(co_ref, mo_ref, wo1_ref, wo2_ref, x_ref, ga_ref, g_ref, b_ref):
    rp = x_ref.shape[0] // MIX_PARTS
    parts = [slice(k * rp, (k + 1) * rp) for k in range(MIX_PARTS)]
    mix = [_dot(co_ref[r], wo1_ref[...]) + _dot(mo_ref[r], wo2_ref[...]) for r in parts]
    x1 = [_layernorm(ALPHA * x_ref[r] + ga_ref[...] * mix[k], g_ref[...], b_ref[...])
          for k, r in enumerate(parts)]
    return parts, x1


def _mix_out_body(co_ref, mo_ref, wo1_ref, wo2_ref, x_ref, ga_ref, scf_ref, shf_ref, g_ref, b_ref,
                  rcat_ref, x1_ref, h2_ref, meta_ref, cnt_ref):
    parts, x1 = _mixer_residual(co_ref, mo_ref, wo1_ref, wo2_ref, x_ref, ga_ref, g_ref, b_ref)
    rp = x_ref.shape[0] // MIX_PARTS
    for k, r in enumerate(parts):
        x1_ref[r] = x1[k]
    h2 = [v * (1.0 + scf_ref[...]) + shf_ref[...] for v in x1]
    hh = [v.astype(BF16) for v in h2]
    for k, r in enumerate(parts):
        h2_ref[r] = hh[k]
    hl = [(h2[k] - hh[k].astype(F32)).astype(BF16) for k in range(MIX_PARTS)]
    ra = [_dot(v, rcat_ref[...]) for v in hh]
    rb = [_dot(v, rcat_ref[...]) for v in hl]
    logits = [ra[k][:, :LANES] + (ra[k][:, LANES:] + (rb[k][:, :LANES] + rb[k][:, LANES:]))
              for k in range(MIX_PARTS)]
    lane = lax.broadcasted_iota(jnp.int32, (rp, LANES), 1).astype(F32)
    neg = -jnp.inf
    lg = [jnp.where(lane < N_EXPERTS, v, neg) for v in logits]
    m1 = [jnp.max(v, axis=-1, keepdims=True) for v in lg]
    i1 = [jnp.min(jnp.where(lg[k] == m1[k], lane, float(LANES)), axis=-1, keepdims=True)
          for k in range(MIX_PARTS)]
    lg2 = [jnp.where(lane == i1[k], neg, lg[k]) for k in range(MIX_PARTS)]
    m2 = [jnp.max(v, axis=-1, keepdims=True) for v in lg2]
    i2 = [jnp.min(jnp.where(lg2[k] == m2[k], lane, float(LANES)), axis=-1, keepdims=True)
          for k in range(MIX_PARTS)]
    picked = [jnp.where(lane == i1[k], 1.0, jnp.where(lane == i2[k], 1.0, 0.0)) for k in range(MIX_PARTS)]
    earlier = jnp.where(lax.broadcasted_iota(jnp.int32, (rp, rp), 0)
                        > lax.broadcasted_iota(jnp.int32, (rp, rp), 1), 1.0, 0.0).astype(BF16)
    before = jnp.zeros((1, LANES), F32)
    for k, r in enumerate(parts):
        rank = _dot(earlier, picked[k].astype(BF16)) + before
        before = before + jnp.sum(picked[k], axis=0, keepdims=True)
        rank1 = jnp.sum(jnp.where(lane == i1[k], rank, 0.0), axis=-1, keepdims=True)
        rank2 = jnp.sum(jnp.where(lane == i2[k], rank, 0.0), axis=-1, keepdims=True)
        e2 = jnp.exp(m2[k] - m1[k])
        den = 1.0 + e2
        meta = jnp.zeros((rp, LANES), F32)
        for j, val in enumerate((i1[k], i2[k], 1.0 / den, e2 / den, rank1, rank2)):
            meta = jnp.where(lane == float(j), val, meta)
        meta_ref[r] = meta
    cnt_ref[...] = before


def _mix_out(stream, layer, co, mo, x, mod, lw, tm):
    t = stream.tokens
    row = lambda w: pl.BlockSpec((tm, w), lambda i: (i, 0))
    vec = _const_spec((1, D_MODEL))
    return pl.pallas_call(
        _mix_out_body,
        grid=(t // tm,),
        in_specs=[row(CHUNK_W), row(MLA_W), _const_spec(lw["wo_chunk"].shape), _const_spec(lw["wo_mla"].shape),
                  row(D_MODEL), stream.mod_spec(layer, G_A, tm), stream.mod_spec(layer, SC_F, tm),
                  stream.mod_spec(layer, SH_F, tm), vec, vec, _const_spec(lw["router"].shape)],
        out_specs=[row(D_MODEL), row(D_MODEL), row(LANES), pl.BlockSpec((None, 1, LANES), lambda i: (i, 0, 0))],
        out_shape=[jax.ShapeDtypeStruct((t, D_MODEL), F32), jax.ShapeDtypeStruct((t, D_MODEL), BF16),
                   jax.ShapeDtypeStruct((t, LANES), F32), jax.ShapeDtypeStruct((t // tm, 1, LANES), F32)],
        compiler_params=_params("parallel"),
        name="mix_out",
    )(co, mo, lw["wo_chunk"], lw["wo_mla"], x, mod, mod, mod, lw["ln_mix_g"], lw["ln_mix_b"], lw["router"])


def _mix_ffn_body(co_ref, mo_ref, wo1_ref, wo2_ref, x_ref, ga_ref, scf_ref, shf_ref, gf_ref,
                  gm_ref, bm_ref, g_ref, b_ref, wg_ref, wu_ref, wd_ref, o_ref):
    _, x1 = _mixer_residual(co_ref, mo_ref, wo1_ref, wo2_ref, x_ref, ga_ref, gm_ref, bm_ref)
    x1 = jnp.concatenate(x1, axis=0)
    h = (x1 * (1.0 + scf_ref[...]) + shf_ref[...]).astype(BF16)
    gate = _dot(h, wg_ref[...])
    up = _dot(h, wu_ref[...])
    a = (gate * _sigmoid(gate) * up).astype(BF16)
    f = _dot(a, wd_ref[...])
    o_ref[...] = _layernorm(ALPHA * x1 + gf_ref[...] * f, g_ref[...], b_ref[...])


def _mix_ffn(stream, layer, co, mo, x, mod, lw, tm):
    t = stream.tokens
    row = lambda w: pl.BlockSpec((tm, w), lambda i: (i, 0))
    vec = _const_spec((1, D_MODEL))
    resident = lambda a: pl.BlockSpec(a.shape, lambda i: (0, 0), pipeline_mode=pl.Buffered(1))
    mods = [stream.mod_spec(layer, which, tm) for which in (G_A, SC_F, SH_F, G_F)]
    return pl.pallas_call(
        _mix_ffn_body,
        grid=(t // tm,),
        in_specs=[row(CHUNK_W), row(MLA_W), resident(lw["wo_chunk"]), resident(lw["wo_mla"]), row(D_MODEL),
                  *mods, vec, vec, vec, vec,
                  resident(lw["ffn_gate"]), resident(lw["ffn_up"]), resident(lw["ffn_down"])],
        out_specs=row(D_MODEL),
        out_shape=jax.ShapeDtypeStruct((t, D_MODEL), F32),
        compiler_params=_params("parallel"),
        name="mix_ffn",
    )(co, mo, lw["wo_chunk"], lw["wo_mla"], x, mod, mod, mod, mod,
      lw["ln_mix_g"], lw["ln_mix_b"], lw["ln_ffn_g"], lw["ln_ffn_b"],
      lw["ffn_gate"], lw["ffn_up"], lw["ffn_down"])


RUN_ALIGN = 16


def _plan_body(cap, tile, cnt_ref, off_ref, run_ref, base_ref, te_ref, tb_ref, rows_ref, nv_ref):
    nw = cnt_ref.shape[0]
    lane = lax.broadcasted_iota(jnp.int32, (1, LANES), 1).astype(F32)
    run = jnp.floor((cnt_ref[...] + (RUN_ALIGN - 1)) * (1.0 / RUN_ALIGN)) * RUN_ALIGN
    before = jnp.where(lax.broadcasted_iota(jnp.int32, (LANES, LANES), 0)
                       < lax.broadcasted_iota(jnp.int32, (LANES, LANES), 1), 1.0, 0.0).astype(BF16)
    upto = jnp.where(lax.broadcasted_iota(jnp.int32, (LANES, LANES), 0)
                     <= lax.broadcasted_iota(jnp.int32, (LANES, LANES), 1), 1.0, 0.0).astype(BF16)
    above = jnp.where(lax.broadcasted_iota(jnp.int32, (nw, nw), 0)
                      > lax.broadcasted_iota(jnp.int32, (nw, nw), 1), 1.0, 0.0).astype(BF16)
    run_b = run.astype(BF16)
    off_ref[...] = _dot(run_b, before).astype(jnp.int32)
    run_ref[...] = run.astype(jnp.int32)
    base_ref[...] = (lane * cap + _dot(above, run_b)).astype(jnp.int32)
    totals = jnp.sum(run, axis=0, keepdims=True)
    tiles = jnp.floor((totals + (tile - 1)) * (1.0 / tile))
    cum = _dot(tiles.astype(BF16), upto)
    start = cum - tiles
    n_valid = jnp.max(cum, axis=-1, keepdims=True)
    step = jnp.minimum(lane, n_valid - 1.0)
    pick = lambda v, e: jnp.sum(jnp.where(lane == float(e), v, 0.0), axis=-1, keepdims=True)
    te = jnp.zeros_like(step)
    for e in range(N_EXPERTS):
        te = te + jnp.where(pick(cum, e) <= step, 1.0, 0.0)
    start_te, total_te = jnp.zeros_like(step), jnp.zeros_like(step)
    for e in range(N_EXPERTS):
        start_te = jnp.where(te == float(e), pick(start, e), start_te)
        total_te = jnp.where(te == float(e), pick(totals, e), total_te)
    tj = step - start_te
    te_ref[...] = te.astype(jnp.int32)
    tb_ref[...] = (te * (cap // tile) + tj).astype(jnp.int32)
    rows_ref[...] = jnp.minimum(float(tile), total_te - tj * tile).astype(jnp.int32)
    nv_ref[...] = jnp.broadcast_to(n_valid, (1, LANES)).astype(jnp.int32)


def _route_plan(cnt, tokens, tm, tile):
    nw = tokens // tm
    worst_pad = (RUN_ALIGN - 1) * nw
    cap = pl.cdiv(tokens + worst_pad, tile) * tile
    n_tiles = (2 * tokens + N_EXPERTS * worst_pad) // tile + N_EXPERTS
    assert n_tiles <= LANES and tm % RUN_ALIGN == 0
    per_run = jax.ShapeDtypeStruct((nw, LANES), jnp.int32)
    per_step = jax.ShapeDtypeStruct((1, LANES), jnp.int32)
    off, run, base, te, tb, rows, nv = pl.pallas_call(
        functools.partial(_plan_body, cap, tile),
        out_shape=[per_run, per_run, per_run, per_step, per_step, per_step, per_step],
        name="moe_plan",
    )(cnt.reshape(nw, LANES))
    return dict(
        tm=tm, nw=nw, cap=cap, tile=tile, n_tiles=n_tiles, buf_rows=2 * tm + N_EXPERTS * RUN_ALIGN,
        off=off.reshape(-1), run=run.reshape(-1), base=base.reshape(-1),
        tile_expert=te.reshape(-1), tile_block=tb.reshape(-1), tile_rows=rows.reshape(-1),
        n_valid=nv.reshape(-1))


def _run_piece_sizes(tm):
    sizes, b = [], RUN_ALIGN
    while b <= tm:
        sizes.append(b)
        b *= 2
    return sizes


def _row_run_copies(src, dst, src_off, dst_off, n_rows, sem, sizes, action):
    for b in sizes:
        done = n_rows & (-2 * b)
        cp = pltpu.make_async_copy(
            src.at[pl.ds(pl.multiple_of(src_off + done, RUN_ALIGN), b)],
            dst.at[pl.ds(pl.multiple_of(dst_off + done, RUN_ALIGN), b)], sem)

        @pl.when((n_rows & b) != 0)
        def _():
            action(cp)


def _local_rows(meta, off_ref, w):
    i1, i2 = meta[:, 0:1], meta[:, 1:2]
    o1, o2 = jnp.zeros_like(i1), jnp.zeros_like(i2)
    for e in range(N_EXPERTS):
        off_e = off_ref[w * LANES + e].astype(F32)
        o1 = jnp.where(i1 == float(e), off_e, o1)
        o2 = jnp.where(i2 == float(e), off_e, o2)
    return (o1 + meta[:, 4:5]).astype(jnp.int32), (o2 + meta[:, 5:6]).astype(jnp.int32)


def _start(cp):
    cp.start()


def _wait(cp):
    cp.wait()


def _dispatch_body(tm, nw, off_ref, run_ref, base_ref, h_ref, meta_ref, xs_ref, xw_ref, sems):
    w = pl.program_id(0)
    slot = w % 2
    sizes = _run_piece_sizes(tm)

    def push(win, buf, action):
        for e in range(N_EXPERTS):
            k = win * LANES + e
            _row_run_copies(xw_ref.at[buf], xs_ref, off_ref[k], base_ref[k], run_ref[k],
                            sems.at[buf], sizes, action)

    r1, r2 = _local_rows(meta_ref[...], off_ref, w)
    col = lax.broadcasted_iota(jnp.int32, (tm, xw_ref.shape[1]), 1)
    sel = jnp.where(col == r1, 1.0, jnp.where(col == r2, 1.0, 0.0)).astype(BF16)
    xw = lax.dot_general(sel, h_ref[...], (((0,), (0,)), ((), ())), preferred_element_type=F32)
    xw_ref[slot] = xw.astype(BF16)
    push(w, slot, _start)

    @pl.when(w > 0)
    def _():
        push(w - 1, 1 - slot, _wait)

    @pl.when(w == nw - 1)
    def _():
        push(w, slot, _wait)


def _dispatch(h2, meta, plan):
    tm = plan["tm"]
    return pl.pallas_call(
        functools.partial(_dispatch_body, tm, plan["nw"]),
        grid_spec=pltpu.PrefetchScalarGridSpec(
            num_scalar_prefetch=3,
            grid=(plan["nw"],),
            in_specs=[pl.BlockSpec((tm, D_MODEL), lambda i, *_: (i, 0)),
                      pl.BlockSpec((tm, LANES), lambda i, *_: (i, 0))],
            out_specs=pl.BlockSpec(memory_space=pl.ANY),
            scratch_shapes=[pltpu.VMEM((2, plan["buf_rows"], D_MODEL), BF16),
                            pltpu.SemaphoreType.DMA((2,))]),
        out_shape=jax.ShapeDtypeStruct((N_EXPERTS * plan["cap"], D_MODEL), BF16),
        compiler_params=_params("arbitrary"),
        name="moe_dispatch",
    )(plan["off"], plan["run"], plan["base"], h2, meta)


def _experts_body(te_ref, tb_ref, rows_ref, nv_ref, x_ref, wg_ref, wu_ref, wd_ref, o_ref):
    i = pl.program_id(0)

    @pl.when(i < nv_ref[0])
    def _():
        x = x_ref[...]
        live = lax.broadcasted_iota(jnp.int32, (x.shape[0], 1), 0) < rows_ref[i]
        x = jnp.where(live, x, jnp.zeros_like(x))
        gate = _dot(x, wg_ref[...])
        up = _dot(x, wu_ref[...])
        a = (gate * _sigmoid(gate) * up).astype(BF16)
        o_ref[...] = _dot(a, wd_ref[...]).astype(BF16)


def _experts(xs, plan, lw):
    dff = lw["moe_gate"].shape[2]
    rows = pl.BlockSpec((plan["tile"], D_MODEL), lambda i, te, tb, *_: (tb[i], 0))
    return pl.pallas_call(
        _experts_body,
        grid_spec=pltpu.PrefetchScalarGridSpec(
            num_scalar_prefetch=4,
            grid=(plan["n_tiles"],),
            in_specs=[rows,
                      pl.BlockSpec((None, D_MODEL, dff), lambda i, te, *_: (te[i], 0, 0)),
                      pl.BlockSpec((None, D_MODEL, dff), lambda i, te, *_: (te[i], 0, 0)),
                      pl.BlockSpec((None, dff, D_MODEL), lambda i, te, *_: (te[i], 0, 0))],
            out_specs=rows),
        out_shape=jax.ShapeDtypeStruct(xs.shape, BF16),
        compiler_params=_params("arbitrary"),
        name="moe_experts",
    )(plan["tile_expert"], plan["tile_block"], plan["tile_rows"], plan["n_valid"],
      xs, lw["moe_gate"], lw["moe_up"], lw["moe_down"])


def _combine_body(tm, nw, off_ref, run_ref, base_ref, ys_ref, meta_ref, x_ref, gf_ref, g_ref, b_ref,
                  o_ref, yw_ref, sems):
    w = pl.program_id(0)
    slot = w % 2
    sizes = _run_piece_sizes(tm)

    def fetch(win, buf, action):
        for e in range(N_EXPERTS):
            k = win * LANES + e
            _row_run_copies(ys_ref, yw_ref.at[buf], base_ref[k], off_ref[k], run_ref[k],
                            sems.at[buf], sizes, action)

    @pl.when(w == 0)
    def _():
        yw_ref[...] = jnp.zeros_like(yw_ref)
        fetch(0, 0, _start)

    @pl.when(w + 1 < nw)
    def _():
        fetch(w + 1, 1 - slot, _start)

    fetch(w, slot, _wait)
    rp = tm // MIX_PARTS
    parts = [slice(k * rp, (k + 1) * rp) for k in range(MIX_PARTS)]
    meta = [meta_ref[r] for r in parts]
    picks = [_local_rows(v, off_ref, w) for v in meta]
    col = lax.broadcasted_iota(jnp.int32, (rp, yw_ref.shape[1]), 1)
    yw = yw_ref[slot]
    gate = [jnp.where(col == r1, v[:, 2:3], jnp.where(col == r2, v[:, 3:4], 0.0)).astype(BF16)
            for v, (r1, r2) in zip(meta, picks)]
    f = [_dot(g, yw) for g in gate]
    for k, r in enumerate(parts):
        o_ref[r] = _layernorm(ALPHA * x_ref[r] + gf_ref[...] * f[k], g_ref[...], b_ref[...])


def _combine(stream, layer, ys, meta, x1, mod, plan, lw):
    tm = plan["tm"]
    row = lambda width: pl.BlockSpec((tm, width), lambda i, *_: (i, 0))
    vec = pl.BlockSpec((1, D_MODEL), lambda i, *_: (0, 0))
    mod_spec = stream.mod_spec(layer, G_F, tm)
    mod_spec = pl.BlockSpec(mod_spec.block_shape, lambda i, *_, f=mod_spec.index_map: f(i))
    return pl.pallas_call(
        functools.partial(_combine_body, tm, plan["nw"]),
        grid_spec=pltpu.PrefetchScalarGridSpec(
            num_scalar_prefetch=3,
            grid=(plan["nw"],),
            in_specs=[pl.BlockSpec(memory_space=pl.ANY), row(LANES), row(D_MODEL), mod_spec, vec, vec],
            out_specs=row(D_MODEL),
            scratch_shapes=[pltpu.VMEM((2, plan["buf_rows"], D_MODEL), BF16),
                            pltpu.SemaphoreType.DMA((2,))]),
        out_shape=jax.ShapeDtypeStruct((stream.tokens, D_MODEL), F32),
        compiler_params=_params("arbitrary"),
        name="moe_combine",
    )(plan["off"], plan["run"], plan["base"], ys, meta, x1, mod, lw["ln_ffn_g"], lw["ln_ffn_b"])


def _rope_tables(n_tokens):
    rows = n_tokens // GRID_W
    row = np.repeat(np.arange(rows, dtype=np.float64), GRID_W)
    col = np.tile(np.arange(GRID_W, dtype=np.float64), rows)
    half = ROPE_DIM // 2
    inv_freq = ROPE_BASE ** (-np.arange(0, half, 2, dtype=np.float64) / half)
    ar = row[:, None] * inv_freq[None, :]
    ac = col[:, None] * inv_freq[None, :]
    cos = np.concatenate([np.cos(ar), np.cos(ar), np.cos(ac), np.cos(ac)], axis=-1)
    sin = np.concatenate([-np.sin(ar), np.sin(ar), -np.sin(ac), np.sin(ac)], axis=-1)
    return jnp.asarray(np.tile(cos, (1, N_HEADS)), F32), jnp.asarray(np.tile(sin, (1, N_HEADS)), F32)


def _swap_rope_halves(a):
    q = ROPE_DIM // 4
    return jnp.concatenate([a[..., q:2 * q], a[..., :q], a[..., 3 * q:], a[..., 2 * q:3 * q]], axis=-1)


def _layer_weights(l, p):
    w_in = p["w_in"][l].astype(BF16)
    o_kr = 2 * CHUNK_W + Q_RANK + KV_RANK
    zpad = jnp.zeros((D_MODEL, LANES - ROPE_DIM), BF16)
    w_in_ext = jnp.concatenate(
        [w_in, zpad, _swap_rope_halves(w_in[:, o_kr:o_kr + ROPE_DIM]), zpad], axis=1)

    w_uq = p["w_uq"][l].astype(BF16).reshape(Q_RANK, N_HEADS, QK_NOPE + ROPE_DIM)
    wq_rope = w_uq[:, :, QK_NOPE:]
    wq = jnp.concatenate([w_uq[:, :, :QK_NOPE].reshape(Q_RANK, N_HEADS * QK_NOPE),
                          wq_rope.reshape(Q_RANK, N_HEADS * ROPE_DIM),
                          _swap_rope_halves(wq_rope).reshape(Q_RANK, N_HEADS * ROPE_DIM)], axis=1)

    w_ukv = p["w_ukv"][l].astype(BF16).reshape(KV_RANK, N_HEADS, QK_NOPE + V_DIM)
    wkv = jnp.concatenate([w_ukv[:, :, :QK_NOPE].reshape(KV_RANK, N_HEADS * QK_NOPE),
                           w_ukv[:, :, QK_NOPE:].reshape(KV_RANK, MLA_W)], axis=1)
    w_out = p["w_out"][l]
    lw = {
        "w_in": w_in_ext,
        "q_g": p["q_norm_g"][l].reshape(1, Q_RANK),
        "kv_g": p["kv_norm_g"][l].reshape(1, KV_RANK),
        "wq": wq,
        "wkv": wkv,
        "ln_v_g": p["chunk_ln_g"][l].reshape(1, CHUNK_W),
        "w_s": p["w_spatial"][l].astype(BF16),
        "b_s": jnp.broadcast_to(p["b_spatial"][l][:, :, None], (N_GROUPS, CHUNK, GROUP_W)),
        "wo_chunk": w_out[:CHUNK_W].astype(BF16),
        "wo_mla": w_out[CHUNK_W:].astype(BF16),
        "ln_mix_g": p["ln_mix_g"][l].reshape(1, D_MODEL),
        "ln_mix_b": p["ln_mix_b"][l].reshape(1, D_MODEL),
        "ln_ffn_g": p["ln_ffn_g"][l].reshape(1, D_MODEL),
        "ln_ffn_b": p["ln_ffn_b"][l].reshape(1, D_MODEL),
    }
    if l % 2 == 0:
        lw["ffn_gate"] = p["ffn_w_gate"][l // 2].astype(BF16)
        lw["ffn_up"] = p["ffn_w_up"][l // 2].astype(BF16)
        lw["ffn_down"] = p["ffn_w_down"][l // 2].astype(BF16)
    else:
        r = jnp.pad(p["router_w"][l // 2], ((0, 0), (0, LANES - N_EXPERTS)))
        r_hi = r.astype(BF16)
        lw["router"] = jnp.concatenate([r_hi, (r - r_hi.astype(F32)).astype(BF16)], axis=1)
        lw["moe_gate"] = p["moe_w_gate"][l // 2].astype(BF16)
        lw["moe_up"] = p["moe_w_up"][l // 2].astype(BF16)
        lw["moe_down"] = p["moe_w_down"][l // 2].astype(BF16)
    return lw


def _trunk(stream, x, mod, weights, rope, ctx, tiles):
    caches = []
    for l in range(DEPTH):
        lw = weights[l]
        outs = _mixer_in(stream, l, x, mod, lw, rope, tiles["mixer_in"])
        co, q, kcat, v = outs[:4]
        if not stream.latent:
            caches.append(outs[4:])
        mo = _attend(stream, l, q, kcat, v, lw, ctx, tiles["attend"])
        if l % 2 == 1:
            x1, h2, meta, cnt = _mix_out(stream, l, co, mo, x, mod, lw, tiles["route"])
            plan = _route_plan(cnt, stream.tokens, tiles["route"], tiles["experts"])
            ys = _experts(_dispatch(h2, meta, plan), plan, lw)
            x = _combine(stream, l, ys, meta, x1, mod, plan, lw)
        else:
            x = _mix_ffn(stream, l, co, mo, x, mod, lw, tiles["ffn"])
    return x, caches


def kernel(x_prompt, x_sample, c, cache_ckv, cache_krope, c_ctx, w_mod, b_mod, w_in, q_norm_g, kv_norm_g, w_uq, w_ukv, chunk_ln_g, w_spatial, b_spatial, w_out, ln_mix_g, ln_mix_b, ln_ffn_g, ln_ffn_b, ffn_w_gate, ffn_w_up, ffn_w_down, router_w, moe_w_gate, moe_w_up, moe_w_down):
    p = dict(w_in=w_in, q_norm_g=q_norm_g, kv_norm_g=kv_norm_g, w_uq=w_uq, w_ukv=w_ukv,
             chunk_ln_g=chunk_ln_g, w_spatial=w_spatial, b_spatial=b_spatial, w_out=w_out,
             ln_mix_g=ln_mix_g, ln_mix_b=ln_mix_b, ln_ffn_g=ln_ffn_g, ln_ffn_b=ln_ffn_b,
             ffn_w_gate=ffn_w_gate, ffn_w_up=ffn_w_up, ffn_w_down=ffn_w_down, router_w=router_w,
             moe_w_gate=moe_w_gate, moe_w_up=moe_w_up, moe_w_down=moe_w_down)
    weights = [_layer_weights(l, p) for l in range(DEPTH)]

    batch, seq, _ = x_prompt.shape
    dec_batch, dec_seq, _ = x_sample.shape
    cond_rows = jnp.concatenate(
        [c_ctx[None, :], c, jnp.zeros((MOD_ROWS - 1 - dec_batch, D_MODEL), F32)], axis=0)
    mod = _modulation(cond_rows, w_mod, b_mod).reshape(DEPTH, MOD_ROWS, 6, 1, D_MODEL)

    prompt = _Stream(batch, seq, mod_row0=0, per_row_mod=False, latent=False)
    sample = _Stream(dec_batch, dec_seq, mod_row0=1, per_row_mod=True, latent=True)

    y_prompt, caches = _trunk(
        prompt, x_prompt.reshape(batch * seq, D_MODEL), mod, weights, None, None,
        dict(mixer_in=256, attend=256, route=512, experts=512, ffn=512))
    new_ckv = jnp.stack([ck.reshape(batch, seq, KV_RANK) for ck, _ in caches], axis=1)
    new_krope = jnp.stack([kr.reshape(batch, seq, ROPE_DIM) for _, kr in caches], axis=1)

    rope = _rope_tables(dec_seq)
    ctx = (cache_ckv, jnp.pad(cache_krope, ((0, 0), (0, 0), (0, 0), (0, LANES - ROPE_DIM))))
    y_sample, _ = _trunk(
        sample, x_sample.reshape(dec_batch * dec_seq, D_MODEL), mod, weights, rope, ctx,
        dict(mixer_in=1024, attend=1024, route=512, experts=512, ffn=512))
    return (y_prompt.reshape(batch, seq, D_MODEL), y_sample.reshape(dec_batch, dec_seq, D_MODEL),
            new_ckv, new_krope)
```

```python
import functools
import math

import jax
import jax.numpy as jnp
import numpy as np
from jax import lax
from jax.experimental import pallas as pl
from jax.experimental.pallas import tpu as pltpu

F32 = jnp.float32
BF16 = jnp.bfloat16

D_MODEL = 1024
DEPTH = 2
GRID_W = 64
CHUNK = 128
N_GROUPS = 4
GROUP_W = 128
CHUNK_W = N_GROUPS * GROUP_W
N_HEADS = 4
QK_NOPE = 128
ROPE_DIM = 64
V_DIM = 128
Q_RANK = 384
KV_RANK = 256
MLA_W = N_HEADS * V_DIM
HEAD_PAD = 256
QK_W = N_HEADS * HEAD_PAD
V_EXT = 2 * V_DIM
VEXT_W = N_HEADS * V_EXT
ROPE_BASE = 10000.0
N_EXPERTS = 8
ALPHA = (2 * DEPTH) ** 0.25
EPS = 1e-6
ATTN_SCALE = math.log2(math.e) / math.sqrt(QK_NOPE + ROPE_DIM)
MOD_ROWS = 16
LANES = 128
VMEM_LIMIT = 56 * 1024 * 1024

SH_A, SC_A, G_A, SH_F, SC_F, G_F = range(6)


def _sigmoid(x):
    return 1.0 / (1.0 + jnp.exp(-x))


def _gelu_tanh(x):
    return 0.5 * x * (1.0 + jnp.tanh(math.sqrt(2.0 / math.pi) * (x + 0.044715 * (x * x * x))))


def _layernorm(y, g, b):
    mu = jnp.mean(y, axis=-1, keepdims=True)
    d = y - mu
    var = jnp.mean(d * d, axis=-1, keepdims=True)
    return d * lax.rsqrt(var + EPS) * g + b


def _rmsnorm(y, g):
    return y * lax.rsqrt(jnp.mean(y * y, axis=-1, keepdims=True) + EPS) * g


def _dot(a, b):
    return jnp.dot(a, b, preferred_element_type=F32)


def _dot_nt(a, b):
    return lax.dot_general(a, b, (((1,), (1,)), ((), ())), preferred_element_type=F32)


def _params(*sem):
    return pltpu.CompilerParams(dimension_semantics=sem, vmem_limit_bytes=VMEM_LIMIT)


def _const_spec(shape):
    nd = len(shape)
    return pl.BlockSpec(shape, lambda *_: (0,) * nd)


class _Stream:
    def __init__(self, batch, seq, mod_row0, per_row_mod, latent):
        self.batch = batch
        self.seq = seq
        self.tokens = batch * seq
        self.mod_row0 = mod_row0
        self.per_row_mod = per_row_mod
        self.latent = latent

    def mod_spec(self, layer, which, tm):
        tiles_per_row = self.seq // tm
        row0, per_row = self.mod_row0, self.per_row_mod

        def index(i):
            r = row0 + (i // tiles_per_row if per_row else 0)
            return (layer, r, which, 0, 0)

        return pl.BlockSpec((None, None, None, 1, D_MODEL), index)


def _mod_body(c_ref, w_ref, b_ref, o_ref):
    a = c_ref[...]
    a = (a * _sigmoid(a)).astype(BF16)
    o_ref[...] = _dot(a, w_ref[...].astype(BF16)) + b_ref[...]


def _modulation(cond_rows, w_mod, b_mod):
    depth, _, width = w_mod.shape
    tn = 1536
    return pl.pallas_call(
        _mod_body,
        grid=(depth, width // tn),
        in_specs=[
            pl.BlockSpec((MOD_ROWS, D_MODEL), lambda l, j: (0, 0)),
            pl.BlockSpec((None, D_MODEL, tn), lambda l, j: (l, 0, j)),
            pl.BlockSpec((None, 1, tn), lambda l, j: (l, 0, j)),
        ],
        out_specs=pl.BlockSpec((None, MOD_ROWS, tn), lambda l, j: (l, 0, j)),
        out_shape=jax.ShapeDtypeStruct((depth, MOD_ROWS, width), F32),
        compiler_params=_params("parallel", "parallel"),
        name="modulation",
    )(cond_rows, w_mod, b_mod.reshape(depth, 1, width))


def _store_keys_values(kv, kr, kcat_ref, vext_ref):
    ones = jnp.ones((kv.shape[0], V_EXT - V_DIM), BF16)
    for hd in range(N_HEADS):
        a = hd * HEAD_PAD
        kcat_ref[:, a:a + QK_NOPE] = kv[:, hd * QK_NOPE:(hd + 1) * QK_NOPE].astype(BF16)
        kcat_ref[:, a + QK_NOPE:a + HEAD_PAD] = kr
        b = hd * V_EXT
        vext_ref[:, b:b + V_DIM] = kv[:, (N_HEADS + hd) * V_DIM:(N_HEADS + hd + 1) * V_DIM].astype(BF16)
        vext_ref[:, b + V_DIM:b + V_EXT] = ones


def _mixer_in_body(latent, tm, *refs):
    it = iter(refs)
    x_ref, sc_ref, sh_ref, win_ref, qg_ref, kvg_ref, wq_ref = (next(it) for _ in range(7))
    wkv_ref, lng_ref, ws_ref, bs_ref = (next(it) for _ in range(4))
    cos_ref, sin_ref = (next(it), next(it)) if latent else (None, None)
    co_ref, q_ref, kcat_ref, vext_ref = (next(it) for _ in range(4))
    ckv_ref, kr_ref = (None, None) if latent else (next(it), next(it))

    h = (x_ref[...] * (1.0 + sc_ref[...]) + sh_ref[...]).astype(BF16)
    p = _dot(h, win_ref[...])

    for g in range(N_GROUPS):
        cols = slice(g * GROUP_W, (g + 1) * GROUP_W)
        vg = _gelu_tanh(p[:, CHUNK_W + g * GROUP_W:CHUNK_W + (g + 1) * GROUP_W])
        mu = jnp.mean(vg, axis=-1, keepdims=True)
        d = vg - mu
        var = jnp.mean(d * d, axis=-1, keepdims=True)
        vn = (d * lax.rsqrt(var + EPS) * lng_ref[:, cols]).astype(BF16)
        ug = _gelu_tanh(p[:, cols])
        for c in range(tm // CHUNK):
            rows = slice(c * CHUNK, (c + 1) * CHUNK)
            z = _dot(ws_ref[g], vn[rows]) + bs_ref[g]
            co_ref[rows, cols] = (ug[rows] * z).astype(BF16)

    o = 2 * CHUNK_W
    cqn = _rmsnorm(p[:, o:o + Q_RANK], qg_ref[...]).astype(BF16)
    qc = _dot(cqn, wq_ref[...])
    q_rope = qc[:, N_HEADS * QK_NOPE:N_HEADS * (QK_NOPE + ROPE_DIM)]
    if latent:
        cos = cos_ref[...]
        sin = sin_ref[...]
        q_rope = q_rope * cos + qc[:, N_HEADS * (QK_NOPE + ROPE_DIM):] * sin
    for hd in range(N_HEADS):
        a = hd * HEAD_PAD
        q_ref[:, a:a + QK_NOPE] = (qc[:, hd * QK_NOPE:(hd + 1) * QK_NOPE] * ATTN_SCALE).astype(BF16)
        pair = q_rope[:, (hd // 2) * LANES:(hd // 2 + 1) * LANES]
        if hd % 2:
            pair = pltpu.roll(pair, ROPE_DIM, axis=1)
        q_ref[:, a + QK_NOPE:a + HEAD_PAD] = (pair * ATTN_SCALE).astype(BF16)

    o += Q_RANK
    ckvn = _rmsnorm(p[:, o:o + KV_RANK], kvg_ref[...])
    o += KV_RANK
    kr = p[:, o:o + LANES]
    if not latent:
        ckv_ref[...] = ckvn
        kr_ref[...] = kr[:, :ROPE_DIM]
    else:
        kr = kr * cos[:, :LANES] + p[:, o + LANES:o + 2 * LANES] * sin[:, :LANES]
    kr = kr.astype(BF16)
    kv = _dot(ckvn.astype(BF16), wkv_ref[...])
    _store_keys_values(kv, kr, kcat_ref, vext_ref)


def _mixer_in(stream, layer, x, mod, lw, rope, tm):
    latent = stream.latent
    t = stream.tokens
    row = lambda w: pl.BlockSpec((tm, w), lambda i: (i, 0))
    win = lw["w_in"]
    win_cols = win.shape[1] if latent else win.shape[1] - LANES
    wq = lw["wq"]
    wq_cols = wq.shape[1] if latent else wq.shape[1] - N_HEADS * ROPE_DIM
    ins = [x, mod, mod, win, lw["q_g"], lw["kv_g"], wq]
    specs = [row(D_MODEL), stream.mod_spec(layer, SC_A, tm), stream.mod_spec(layer, SH_A, tm),
             _const_spec((D_MODEL, win_cols)), _const_spec(lw["q_g"].shape), _const_spec(lw["kv_g"].shape),
             _const_spec((Q_RANK, wq_cols))]
    for name in ("wkv", "ln_v_g", "w_s", "b_s"):
        ins.append(lw[name])
        specs.append(_const_spec(lw[name].shape))
    if latent:
        tiles_per_seq = stream.seq // tm
        pos = pl.BlockSpec((tm, N_HEADS * ROPE_DIM), lambda i: (i % tiles_per_seq, 0))
        ins += [rope[0], rope[1]]
        specs += [pos, pos]
    out_shape = [jax.ShapeDtypeStruct((t, CHUNK_W), BF16), jax.ShapeDtypeStruct((t, QK_W), BF16),
                 jax.ShapeDtypeStruct((t, QK_W), BF16), jax.ShapeDtypeStruct((t, VEXT_W), BF16)]
    out_specs = [row(CHUNK_W), row(QK_W), row(QK_W), row(VEXT_W)]
    if not latent:
        out_shape += [jax.ShapeDtypeStruct((t, KV_RANK), F32), jax.ShapeDtypeStruct((t, ROPE_DIM), F32)]
        out_specs += [row(KV_RANK), row(ROPE_DIM)]
    return pl.pallas_call(
        functools.partial(_mixer_in_body, latent, tm),
        grid=(t // tm,),
        in_specs=specs,
        out_specs=out_specs,
        out_shape=out_shape,
        compiler_params=_params("parallel"),
        name="mixer_in",
    )(*ins)


KEY_CHUNK = 256


def _attend_body(has_ctx, *refs):
    if has_ctx:
        q_ref, k_ref, vext_ref, cckv_ref, ckr_ref, wkv_ref, o_ref, kctx_ref, vctx_ref = refs

        @pl.when(pl.program_id(1) == 0)
        def _():
            kv = _dot(cckv_ref[...].astype(BF16), wkv_ref[...])
            _store_keys_values(kv, ckr_ref[...].astype(BF16), kctx_ref, vctx_ref)

        sources = [(kctx_ref, vctx_ref), (k_ref, vext_ref)]
    else:
        q_ref, k_ref, vext_ref, o_ref = refs
        sources = [(k_ref, vext_ref)]
    chunks = [(kr, vr, slice(c * KEY_CHUNK, (c + 1) * KEY_CHUNK))
              for kr, vr in sources for c in range(kr.shape[0] // KEY_CHUNK)]

    def scores(hd):
        qk = slice(hd * HEAD_PAD, (hd + 1) * HEAD_PAD)
        return [_dot_nt(q_ref[:, qk], kr[rows, qk]) for kr, _, rows in chunks]

    s = scores(0)
    for hd in range(N_HEADS):
        m = jnp.max(functools.reduce(jnp.maximum, s), axis=-1, keepdims=True)
        s_next, acc = [], None
        for c, (kr, vr, rows) in enumerate(chunks):
            if hd + 1 < N_HEADS:
                qk = slice((hd + 1) * HEAD_PAD, (hd + 2) * HEAD_PAD)
                s_next.append(_dot_nt(q_ref[:, qk], kr[rows, qk]))
            p = jnp.exp2(s[c] - m).astype(BF16)
            part = _dot(p, vr[rows, hd * V_EXT:(hd + 1) * V_EXT])
            acc = part if acc is None else acc + part
        s = s_next
        o_ref[:, hd * V_DIM:(hd + 1) * V_DIM] = (acc[:, :V_DIM] / acc[:, V_DIM:]).astype(BF16)


def _attend(stream, layer, q, kcat, vext, lw, ctx, tq):
    n = stream.seq
    nq = n // tq
    has_ctx = stream.latent
    ins = [q, kcat, vext]
    specs = [pl.BlockSpec((tq, QK_W), lambda b, i: (b * nq + i, 0)),
             pl.BlockSpec((n, QK_W), lambda b, i: (b, 0)),
             pl.BlockSpec((n, VEXT_W), lambda b, i: (b, 0))]
    scratch = []
    if has_ctx:
        cache_ckv, cache_kr = ctx
        past = cache_ckv.shape[2]
        ins += [cache_ckv, cache_kr, lw["wkv"]]
        specs += [pl.BlockSpec((None, None, past, KV_RANK), lambda b, i: (b, layer, 0, 0)),
                  pl.BlockSpec((None, None, past, LANES), lambda b, i: (b, layer, 0, 0)),
                  _const_spec(lw["wkv"].shape)]
        scratch = [pltpu.VMEM((past, QK_W), BF16), pltpu.VMEM((past, VEXT_W), BF16)]
    return pl.pallas_call(
        functools.partial(_attend_body, has_ctx),
        grid=(stream.batch, nq),
        in_specs=specs,
        out_specs=pl.BlockSpec((tq, MLA_W), lambda b, i: (b * nq + i, 0)),
        out_shape=jax.ShapeDtypeStruct((stream.tokens, MLA_W), BF16),
        scratch_shapes=scratch,
        compiler_params=_params("parallel", "arbitrary"),
        name="attend",
    )(*ins)


MIX_PARTS = 4


def _mixer_residual(co_ref, mo_ref, wo1_ref, wo2_ref, x_ref, ga_ref, g_ref, b_ref):
    rp = x_ref.shape[0] // MIX_PARTS
    parts = [slice(k * rp, (k + 1) * rp) for k in range(MIX_PARTS)]
    mix = [_dot(co_ref[r], wo1_ref[...]) + _dot(mo_ref[r], wo2_ref[...]) for r in parts]
    x1 = [_layernorm(ALPHA * x_ref[r] + ga_ref[...] * mix[k], g_ref[...], b_ref[...])
          for k, r in enumerate(parts)]
    return parts, x1


def _mix_out_body(co_ref, mo_ref, wo1_ref, wo2_ref, x_ref, ga_ref, scf_ref, shf_ref, g_ref, b_ref,
                  rcat_ref, x1_ref, h2_ref, meta_ref, cnt_ref):
    parts, x1 = _mixer_residual(co_ref, mo_ref, wo1_ref, wo2_ref, x_ref, ga_ref, g_ref, b_ref)
    rp = x_ref.shape[0] // MIX_PARTS
    for k, r in enumerate(parts):
        x1_ref[r] = x1[k]
    h2 = [v * (1.0 + scf_ref[...]) + shf_ref[...] for v in x1]
    hh = [v.astype(BF16) for v in h2]
    for k, r in enumerate(parts):
        h2_ref[r] = hh[k]
    hl = [(h2[k] - hh[k].astype(F32)).astype(BF16) for k in range(MIX_PARTS)]
    ra = [_dot(v, rcat_ref[...]) for v in hh]
    rb = [_dot(v, rcat_ref[...]) for v in hl]
    logits = [ra[k][:, :LANES] + (ra[k][:, LANES:] + (rb[k][:, :LANES] + rb[k][:, LANES:]))
              for k in range(MIX_PARTS)]
    lane = lax.broadcasted_iota(jnp.int32, (rp, LANES), 1).astype(F32)
    neg = -jnp.inf
    lg = [jnp.where(lane < N_EXPERTS, v, neg) for v in logits]
    m1 = [jnp.max(v, axis=-1, keepdims=True) for v in lg]
    i1 = [jnp.min(jnp.where(lg[k] == m1[k], lane, float(LANES)), axis=-1, keepdims=True)
          for k in range(MIX_PARTS)]
    lg2 = [jnp.where(lane == i1[k], neg, lg[k]) for k in range(MIX_PARTS)]
    m2 = [jnp.max(v, axis=-1, keepdims=True) for v in lg2]
    i2 = [jnp.min(jnp.where(lg2[k] == m2[k], lane, float(LANES)), axis=-1, keepdims=True)
          for k in range(MIX_PARTS)]
    picked = [jnp.where(lane == i1[k], 1.0, jnp.where(lane == i2[k], 1.0, 0.0)) for k in range(MIX_PARTS)]
    earlier = jnp.where(lax.broadcasted_iota(jnp.int32, (rp, rp), 0)
                        > lax.broadcasted_iota(jnp.int32, (rp, rp), 1), 1.0, 0.0).astype(BF16)
    before = jnp.zeros((1, LANES), F32)
    for k, r in enumerate(parts):
        rank = _dot(earlier, picked[k].astype(BF16)) + before
        before = before + jnp.sum(picked[k], axis=0, keepdims=True)
        rank1 = jnp.sum(jnp.where(lane == i1[k], rank, 0.0), axis=-1, keepdims=True)
        rank2 = jnp.sum(jnp.where(lane == i2[k], rank, 0.0), axis=-1, keepdims=True)
        e2 = jnp.exp(m2[k] - m1[k])
        den = 1.0 + e2
        meta = jnp.zeros((rp, LANES), F32)
        for j, val in enumerate((i1[k], i2[k], 1.0 / den, e2 / den, rank1, rank2)):
            meta = jnp.where(lane == float(j), val, meta)
        meta_ref[r] = meta
    cnt_ref[...] = before


def _mix_out(stream, layer, co, mo, x, mod, lw, tm):
    t = stream.tokens
    row = lambda w: pl.BlockSpec((tm, w), lambda i: (i, 0))
    vec = _const_spec((1, D_MODEL))
    return pl.pallas_call(
        _mix_out_body,
        grid=(t // tm,),
        in_specs=[row(CHUNK_W), row(MLA_W), _const_spec(lw["wo_chunk"].shape), _const_spec(lw["wo_mla"].shape),
                  row(D_MODEL), stream.mod_spec(layer, G_A, tm), stream.mod_spec(layer, SC_F, tm),
                  stream.mod_spec(layer, SH_F, tm), vec, vec, _const_spec(lw["router"].shape)],
        out_specs=[row(D_MODEL), row(D_MODEL), row(LANES), pl.BlockSpec((None, 1, LANES), lambda i: (i, 0, 0))],
        out_shape=[jax.ShapeDtypeStruct((t, D_MODEL), F32), jax.ShapeDtypeStruct((t, D_MODEL), BF16),
                   jax.ShapeDtypeStruct((t, LANES), F32), jax.ShapeDtypeStruct((t // tm, 1, LANES), F32)],
        compiler_params=_params("parallel"),
        name="mix_out",
    )(co, mo, lw["wo_chunk"], lw["wo_mla"], x, mod, mod, mod, lw["ln_mix_g"], lw["ln_mix_b"], lw["router"])


def _mix_ffn_body(co_ref, mo_ref, wo1_ref, wo2_ref, x_ref, ga_ref, scf_ref, shf_ref, gf_ref,
                  gm_ref, bm_ref, g_ref, b_ref, wg_ref, wu_ref, wd_ref, o_ref):
    _, x1 = _mixer_residual(co_ref, mo_ref, wo1_ref, wo2_ref, x_ref, ga_ref, gm_ref, bm_ref)
    x1 = jnp.concatenate(x1, axis=0)
    h = (x1 * (1.0 + scf_ref[...]) + shf_ref[...]).astype(BF16)
    gate = _dot(h, wg_ref[...])
    up = _dot(h, wu_ref[...])
    a = (gate * _sigmoid(gate) * up).astype(BF16)
    f = _dot(a, wd_ref[...])
    o_ref[...] = _layernorm(ALPHA * x1 + gf_ref[...] * f, g_ref[...], b_ref[...])


def _mix_ffn(stream, layer, co, mo, x, mod, lw, tm):
    t = stream.tokens
    row = lambda w: pl.BlockSpec((tm, w), lambda i: (i, 0))
    vec = _const_spec((1, D_MODEL))
    resident = lambda a: pl.BlockSpec(a.shape, lambda i: (0, 0), pipeline_mode=pl.Buffered(1))
    mods = [stream.mod_spec(layer, which, tm) for which in (G_A, SC_F, SH_F, G_F)]
    return pl.pallas_call(
        _mix_ffn_body,
        grid=(t // tm,),
        in_specs=[row(CHUNK_W), row(MLA_W), resident(lw["wo_chunk"]), resident(lw["wo_mla"]), row(D_MODEL),
                  *mods, vec, vec, vec, vec,
                  resident(lw["ffn_gate"]), resident(lw["ffn_up"]), resident(lw["ffn_down"])],
        out_specs=row(D_MODEL),
        out_shape=jax.ShapeDtypeStruct((t, D_MODEL), F32),
        compiler_params=_params("parallel"),
        name="mix_ffn",
    )(co, mo, lw["wo_chunk"], lw["wo_mla"], x, mod, mod, mod, mod,
      lw["ln_mix_g"], lw["ln_mix_b"], lw["ln_ffn_g"], lw["ln_ffn_b"],
      lw["ffn_gate"], lw["ffn_up"], lw["ffn_down"])


RUN_ALIGN = 16


def _plan_body(cap, tile, cnt_ref, off_ref, run_ref, base_ref, te_ref, tb_ref, rows_ref, nv_ref):
    nw = cnt_ref.shape[0]
    lane = lax.broadcasted_iota(jnp.int32, (1, LANES), 1).astype(F32)
    run = jnp.floor((cnt_ref[...] + (RUN_ALIGN - 1)) * (1.0 / RUN_ALIGN)) * RUN_ALIGN
    before = jnp.where(lax.broadcasted_iota(jnp.int32, (LANES, LANES), 0)
                       < lax.broadcasted_iota(jnp.int32, (LANES, LANES), 1), 1.0, 0.0).astype(BF16)
    upto = jnp.where(lax.broadcasted_iota(jnp.int32, (LANES, LANES), 0)
                     <= lax.broadcasted_iota(jnp.int32, (LANES, LANES), 1), 1.0, 0.0).astype(BF16)
    above = jnp.where(lax.broadcasted_iota(jnp.int32, (nw, nw), 0)
                      > lax.broadcasted_iota(jnp.int32, (nw, nw), 1), 1.0, 0.0).astype(BF16)
    run_b = run.astype(BF16)
    off_ref[...] = _dot(run_b, before).astype(jnp.int32)
    run_ref[...] = run.astype(jnp.int32)
    base_ref[...] = (lane * cap + _dot(above, run_b)).astype(jnp.int32)
    totals = jnp.sum(run, axis=0, keepdims=True)
    tiles = jnp.floor((totals + (tile - 1)) * (1.0 / tile))
    cum = _dot(tiles.astype(BF16), upto)
    start = cum - tiles
    n_valid = jnp.max(cum, axis=-1, keepdims=True)
    step = jnp.minimum(lane, n_valid - 1.0)
    pick = lambda v, e: jnp.sum(jnp.where(lane == float(e), v, 0.0), axis=-1, keepdims=True)
    te = jnp.zeros_like(step)
    for e in range(N_EXPERTS):
        te = te + jnp.where(pick(cum, e) <= step, 1.0, 0.0)
    start_te, total_te = jnp.zeros_like(step), jnp.zeros_like(step)
    for e in range(N_EXPERTS):
        start_te = jnp.where(te == float(e), pick(start, e), start_te)
        total_te = jnp.where(te == float(e), pick(totals, e), total_te)
    tj = step - start_te
    te_ref[...] = te.astype(jnp.int32)
    tb_ref[...] = (te * (cap // tile) + tj).astype(jnp.int32)
    rows_ref[...] = jnp.minimum(float(tile), total_te - tj * tile).astype(jnp.int32)
    nv_ref[...] = jnp.broadcast_to(n_valid, (1, LANES)).astype(jnp.int32)


def _route_plan(cnt, tokens, tm, tile):
    nw = tokens // tm
    worst_pad = (RUN_ALIGN - 1) * nw
    cap = pl.cdiv(tokens + worst_pad, tile) * tile
    n_tiles = (2 * tokens + N_EXPERTS * worst_pad) // tile + N_EXPERTS
    assert n_tiles <= LANES and tm % RUN_ALIGN == 0
    per_run = jax.ShapeDtypeStruct((nw, LANES), jnp.int32)
    per_step = jax.ShapeDtypeStruct((1, LANES), jnp.int32)
    off, run, base, te, tb, rows, nv = pl.pallas_call(
        functools.partial(_plan_body, cap, tile),
        out_shape=[per_run, per_run, per_run, per_step, per_step, per_step, per_step],
        name="moe_plan",
    )(cnt.reshape(nw, LANES))
    return dict(
        tm=tm, nw=nw, cap=cap, tile=tile, n_tiles=n_tiles, buf_rows=2 * tm + N_EXPERTS * RUN_ALIGN,
        off=off.reshape(-1), run=run.reshape(-1), base=base.reshape(-1),
        tile_expert=te.reshape(-1), tile_block=tb.reshape(-1), tile_rows=rows.reshape(-1),
        n_valid=nv.reshape(-1))


def _run_piece_sizes(tm):
    sizes, b = [], RUN_ALIGN
    while b <= tm:
        sizes.append(b)
        b *= 2
    return sizes


def _row_run_copies(src, dst, src_off, dst_off, n_rows, sem, sizes, action):
    for b in sizes:
        done = n_rows & (-2 * b)
        cp = pltpu.make_async_copy(
            src.at[pl.ds(pl.multiple_of(src_off + done, RUN_ALIGN), b)],
            dst.at[pl.ds(pl.multiple_of(dst_off + done, RUN_ALIGN), b)], sem)

        @pl.when((n_rows & b) != 0)
        def _():
            action(cp)


def _local_rows(meta, off_ref, w):
    i1, i2 = meta[:, 0:1], meta[:, 1:2]
    o1, o2 = jnp.zeros_like(i1), jnp.zeros_like(i2)
    for e in range(N_EXPERTS):
        off_e = off_ref[w * LANES + e].astype(F32)
        o1 = jnp.where(i1 == float(e), off_e, o1)
        o2 = jnp.where(i2 == float(e), off_e, o2)
    return (o1 + meta[:, 4:5]).astype(jnp.int32), (o2 + meta[:, 5:6]).astype(jnp.int32)


def _start(cp):
    cp.start()


def _wait(cp):
    cp.wait()


def _dispatch_body(tm, nw, off_ref, run_ref, base_ref, h_ref, meta_ref, xs_ref, xw_ref, sems):
    w = pl.program_id(0)
    slot = w % 2
    sizes = _run_piece_sizes(tm)

    def push(win, buf, action):
        for e in range(N_EXPERTS):
            k = win * LANES + e
            _row_run_copies(xw_ref.at[buf], xs_ref, off_ref[k], base_ref[k], run_ref[k],
                            sems.at[buf], sizes, action)

    r1, r2 = _local_rows(meta_ref[...], off_ref, w)
    col = lax.broadcasted_iota(jnp.int32, (tm, xw_ref.shape[1]), 1)
    sel = jnp.where(col == r1, 1.0, jnp.where(col == r2, 1.0, 0.0)).astype(BF16)
    xw = lax.dot_general(sel, h_ref[...], (((0,), (0,)), ((), ())), preferred_element_type=F32)
    xw_ref[slot] = xw.astype(BF16)
    push(w, slot, _start)

    @pl.when(w > 0)
    def _():
        push(w - 1, 1 - slot, _wait)

    @pl.when(w == nw - 1)
    def _():
        push(w, slot, _wait)


def _dispatch(h2, meta, plan):
    tm = plan["tm"]
    return pl.pallas_call(
        functools.partial(_dispatch_body, tm, plan["nw"]),
        grid_spec=pltpu.PrefetchScalarGridSpec(
            num_scalar_prefetch=3,
            grid=(plan["nw"],),
            in_specs=[pl.BlockSpec((tm, D_MODEL), lambda i, *_: (i, 0)),
                      pl.BlockSpec((tm, LANES), lambda i, *_: (i, 0))],
            out_specs=pl.BlockSpec(memory_space=pl.ANY),
            scratch_shapes=[pltpu.VMEM((2, plan["buf_rows"], D_MODEL), BF16),
                            pltpu.SemaphoreType.DMA((2,))]),
        out_shape=jax.ShapeDtypeStruct((N_EXPERTS * plan["cap"], D_MODEL), BF16),
        compiler_params=_params("arbitrary"),
        name="moe_dispatch",
    )(plan["off"], plan["run"], plan["base"], h2, meta)


def _experts_body(te_ref, tb_ref, rows_ref, nv_ref, x_ref, wg_ref, wu_ref, wd_ref, o_ref):
    i = pl.program_id(0)
    rows = rows_ref[i]
    valid = i < nv_ref[0]
    half = x_ref.shape[0] // 2

    def swiglu(n):
        x = x_ref[0:n, :]
        live = lax.broadcasted_iota(jnp.int32, (n, 1), 0) < rows
        x = jnp.where(live, x, jnp.zeros_like(x))
        gate = _dot(x, wg_ref[...])
        up = _dot(x, wu_ref[...])
        a = (gate * _sigmoid(gate) * up).astype(BF16)
        o_ref[0:n, :] = _dot(a, wd_ref[...]).astype(BF16)

    @pl.when(jnp.logical_and(valid, rows > half))
    def _():
        swiglu(2 * half)

    @pl.when(jnp.logical_and(valid, rows <= half))
    def _():
        swiglu(half)


def _experts(xs, plan, lw):
    dff = lw["moe_gate"].shape[2]
    rows = pl.BlockSpec((plan["tile"], D_MODEL), lambda i, te, tb, *_: (tb[i], 0))
    return pl.pallas_call(
        _experts_body,
        grid_spec=pltpu.PrefetchScalarGridSpec(
            num_scalar_prefetch=4,
            grid=(plan["n_tiles"],),
            in_specs=[rows,
                      pl.BlockSpec((None, D_MODEL, dff), lambda i, te, *_: (te[i], 0, 0)),
                      pl.BlockSpec((None, D_MODEL, dff), lambda i, te, *_: (te[i], 0, 0)),
                      pl.BlockSpec((None, dff, D_MODEL), lambda i, te, *_: (te[i], 0, 0))],
            out_specs=rows),
        out_shape=jax.ShapeDtypeStruct(xs.shape, BF16),
        compiler_params=_params("arbitrary"),
        name="moe_experts",
    )(plan["tile_expert"], plan["tile_block"], plan["tile_rows"], plan["n_valid"],
      xs, lw["moe_gate"], lw["moe_up"], lw["moe_down"])


def _combine_body(tm, nw, off_ref, run_ref, base_ref, ys_ref, meta_ref, x_ref, gf_ref, g_ref, b_ref,
                  o_ref, yw_ref, sems):
    w = pl.program_id(0)
    slot = w % 2
    sizes = _run_piece_sizes(tm)

    def fetch(win, buf, action):
        for e in range(N_EXPERTS):
            k = win * LANES + e
            _row_run_copies(ys_ref, yw_ref.at[buf], base_ref[k], off_ref[k], run_ref[k],
                            sems.at[buf], sizes, action)

    @pl.when(w == 0)
    def _():
        yw_ref[...] = jnp.zeros_like(yw_ref)
        fetch(0, 0, _start)

    @pl.when(w + 1 < nw)
    def _():
        fetch(w + 1, 1 - slot, _start)

    fetch(w, slot, _wait)
    rp = tm // MIX_PARTS
    parts = [slice(k * rp, (k + 1) * rp) for k in range(MIX_PARTS)]
    meta = [meta_ref[r] for r in parts]
    picks = [_local_rows(v, off_ref, w) for v in meta]
    col = lax.broadcasted_iota(jnp.int32, (rp, yw_ref.shape[1]), 1)
    yw = yw_ref[slot]
    gate = [jnp.where(col == r1, v[:, 2:3], jnp.where(col == r2, v[:, 3:4], 0.0)).astype(BF16)
            for v, (r1, r2) in zip(meta, picks)]
    f = [_dot(g, yw) for g in gate]
    for k, r in enumerate(parts):
        o_ref[r] = _layernorm(ALPHA * x_ref[r] + gf_ref[...] * f[k], g_ref[...], b_ref[...])


def _combine(stream, layer, ys, meta, x1, mod, plan, lw):
    tm = plan["tm"]
    row = lambda width: pl.BlockSpec((tm, width), lambda i, *_: (i, 0))
    vec = pl.BlockSpec((1, D_MODEL), lambda i, *_: (0, 0))
    mod_spec = stream.mod_spec(layer, G_F, tm)
    mod_spec = pl.BlockSpec(mod_spec.block_shape, lambda i, *_, f=mod_spec.index_map: f(i))
    return pl.pallas_call(
        functools.partial(_combine_body, tm, plan["nw"]),
        grid_spec=pltpu.PrefetchScalarGridSpec(
            num_scalar_prefetch=3,
            grid=(plan["nw"],),
            in_specs=[pl.BlockSpec(memory_space=pl.ANY), row(LANES), row(D_MODEL), mod_spec, vec, vec],
            out_specs=row(D_MODEL),
            scratch_shapes=[pltpu.VMEM((2, plan["buf_rows"], D_MODEL), BF16),
                            pltpu.SemaphoreType.DMA((2,))]),
        out_shape=jax.ShapeDtypeStruct((stream.tokens, D_MODEL), F32),
        compiler_params=_params("arbitrary"),
        name="moe_combine",
    )(plan["off"], plan["run"], plan["base"], ys, meta, x1, mod, lw["ln_ffn_g"], lw["ln_ffn_b"])


def _rope_tables(n_tokens):
    rows = n_tokens // GRID_W
    row = np.repeat(np.arange(rows, dtype=np.float64), GRID_W)
    col = np.tile(np.arange(GRID_W, dtype=np.float64), rows)
    half = ROPE_DIM // 2
    inv_freq = ROPE_BASE ** (-np.arange(0, half, 2, dtype=np.float64) / half)
    ar = row[:, None] * inv_freq[None, :]
    ac = col[:, None] * inv_freq[None, :]
    cos = np.concatenate([np.cos(ar), np.cos(ar), np.cos(ac), np.cos(ac)], axis=-1)
    sin = np.concatenate([-np.sin(ar), np.sin(ar), -np.sin(ac), np.sin(ac)], axis=-1)
    return jnp.asarray(np.tile(cos, (1, N_HEADS)), F32), jnp.asarray(np.tile(sin, (1, N_HEADS)), F32)


def _swap_rope_halves(a):
    q = ROPE_DIM // 4
    return jnp.concatenate([a[..., q:2 * q], a[..., :q], a[..., 3 * q:], a[..., 2 * q:3 * q]], axis=-1)


def _layer_weights(l, p):
    w_in = p["w_in"][l].astype(BF16)
    o_kr = 2 * CHUNK_W + Q_RANK + KV_RANK
    zpad = jnp.zeros((D_MODEL, LANES - ROPE_DIM), BF16)
    w_in_ext = jnp.concatenate(
        [w_in, zpad, _swap_rope_halves(w_in[:, o_kr:o_kr + ROPE_DIM]), zpad], axis=1)

    w_uq = p["w_uq"][l].astype(BF16).reshape(Q_RANK, N_HEADS, QK_NOPE + ROPE_DIM)
    wq_rope = w_uq[:, :, QK_NOPE:]
    wq = jnp.concatenate([w_uq[:, :, :QK_NOPE].reshape(Q_RANK, N_HEADS * QK_NOPE),
                          wq_rope.reshape(Q_RANK, N_HEADS * ROPE_DIM),
                          _swap_rope_halves(wq_rope).reshape(Q_RANK, N_HEADS * ROPE_DIM)], axis=1)

    w_ukv = p["w_ukv"][l].astype(BF16).reshape(KV_RANK, N_HEADS, QK_NOPE + V_DIM)
    wkv = jnp.concatenate([w_ukv[:, :, :QK_NOPE].reshape(KV_RANK, N_HEADS * QK_NOPE),
                           w_ukv[:, :, QK_NOPE:].reshape(KV_RANK, MLA_W)], axis=1)
    w_out = p["w_out"][l]
    lw = {
        "w_in": w_in_ext,
        "q_g": p["q_norm_g"][l].reshape(1, Q_RANK),
        "kv_g": p["kv_norm_g"][l].reshape(1, KV_RANK),
        "wq": wq,
        "wkv": wkv,
        "ln_v_g": p["chunk_ln_g"][l].reshape(1, CHUNK_W),
        "w_s": p["w_spatial"][l].astype(BF16),
        "b_s": jnp.broadcast_to(p["b_spatial"][l][:, :, None], (N_GROUPS, CHUNK, GROUP_W)),
        "wo_chunk": w_out[:CHUNK_W].astype(BF16),
        "wo_mla": w_out[CHUNK_W:].astype(BF16),
        "ln_mix_g": p["ln_mix_g"][l].reshape(1, D_MODEL),
        "ln_mix_b": p["ln_mix_b"][l].reshape(1, D_MODEL),
        "ln_ffn_g": p["ln_ffn_g"][l].reshape(1, D_MODEL),
        "ln_ffn_b": p["ln_ffn_b"][l].reshape(1, D_MODEL),
    }
    if l % 2 == 0:
        lw["ffn_gate"] = p["ffn_w_gate"][l // 2].astype(BF16)
        lw["ffn_up"] = p["ffn_w_up"][l // 2].astype(BF16)
        lw["ffn_down"] = p["ffn_w_down"][l // 2].astype(BF16)
    else:
        r = jnp.pad(p["router_w"][l // 2], ((0, 0), (0, LANES - N_EXPERTS)))
        r_hi = r.astype(BF16)
        lw["router"] = jnp.concatenate([r_hi, (r - r_hi.astype(F32)).astype(BF16)], axis=1)
        lw["moe_gate"] = p["moe_w_gate"][l // 2].astype(BF16)
        lw["moe_up"] = p["moe_w_up"][l // 2].astype(BF16)
        lw["moe_down"] = p["moe_w_down"][l // 2].astype(BF16)
    return lw


def _trunk(stream, x, mod, weights, rope, ctx, tiles):
    caches = []
    for l in range(DEPTH):
        lw = weights[l]
        outs = _mixer_in(stream, l, x, mod, lw, rope, tiles["mixer_in"])
        co, q, kcat, v = outs[:4]
        if not stream.latent:
            caches.append(outs[4:])
        mo = _attend(stream, l, q, kcat, v, lw, ctx, tiles["attend"])
        if l % 2 == 1:
            x1, h2, meta, cnt = _mix_out(stream, l, co, mo, x, mod, lw, tiles["route"])
            plan = _route_plan(cnt, stream.tokens, tiles["route"], tiles["experts"])
            ys = _experts(_dispatch(h2, meta, plan), plan, lw)
            x = _combine(stream, l, ys, meta, x1, mod, plan, lw)
        else:
            x = _mix_ffn(stream, l, co, mo, x, mod, lw, tiles["ffn"])
    return x, caches


def kernel(x_prompt, x_sample, c, cache_ckv, cache_krope, c_ctx, w_mod, b_mod, w_in, q_norm_g, kv_norm_g, w_uq, w_ukv, chunk_ln_g, w_spatial, b_spatial, w_out, ln_mix_g, ln_mix_b, ln_ffn_g, ln_ffn_b, ffn_w_gate, ffn_w_up, ffn_w_down, router_w, moe_w_gate, moe_w_up, moe_w_down):
    p = dict(w_in=w_in, q_norm_g=q_norm_g, kv_norm_g=kv_norm_g, w_uq=w_uq, w_ukv=w_ukv,
             chunk_ln_g=chunk_ln_g, w_spatial=w_spatial, b_spatial=b_spatial, w_out=w_out,
             ln_mix_g=ln_mix_g, ln_mix_b=ln_mix_b, ln_ffn_g=ln_ffn_g, ln_ffn_b=ln_ffn_b,
             ffn_w_gate=ffn_w_gate, ffn_w_up=ffn_w_up, ffn_w_down=ffn_w_down, router_w=router_w,
             moe_w_gate=moe_w_gate, moe_w_up=moe_w_up, moe_w_down=moe_w_down)
    weights = [_layer_weights(l, p) for l in range(DEPTH)]

    batch, seq, _ = x_prompt.shape
    dec_batch, dec_seq, _ = x_sample.shape
    cond_rows = jnp.concatenate(
        [c_ctx[None, :], c, jnp.zeros((MOD_ROWS - 1 - dec_batch, D_MODEL), F32)], axis=0)
    mod = _modulation(cond_rows, w_mod, b_mod).reshape(DEPTH, MOD_ROWS, 6, 1, D_MODEL)

    prompt = _Stream(batch, seq, mod_row0=0, per_row_mod=False, latent=False)
    sample = _Stream(dec_batch, dec_seq, mod_row0=1, per_row_mod=True, latent=True)

    y_prompt, caches = _trunk(
        prompt, x_prompt.reshape(batch * seq, D_MODEL), mod, weights, None, None,
        dict(mixer_in=512, attend=256, route=512, experts=256, ffn=512))
    new_ckv = jnp.stack([ck.reshape(batch, seq, KV_RANK) for ck, _ in caches], axis=1)
    new_krope = jnp.stack([kr.reshape(batch, seq, ROPE_DIM) for _, kr in caches], axis=1)

    rope = _rope_tables(dec_seq)
    ctx = (cache_ckv, jnp.pad(cache_krope, ((0, 0), (0, 0), (0, 0), (0, LANES - ROPE_DIM))))
    y_sample, _ = _trunk(
        sample, x_sample.reshape(dec_batch * dec_seq, D_MODEL), mod, weights, rope, ctx,
        dict(mixer_in=1024, attend=1024, route=512, experts=512, ffn=512))
    return (y_prompt.reshape(batch, seq, D_MODEL), y_sample.reshape(dec_batch, dec_seq, D_MODEL),
            new_ckv, new_krope)
```

```python
import functools
import math

import jax
import jax.numpy as jnp
import numpy as np
from jax import lax
from jax.experimental import pallas as pl
from jax.experimental.pallas import tpu as pltpu

F32 = jnp.float32
BF16 = jnp.bfloat16

D_MODEL = 1024
DEPTH = 2
GRID_W = 64
CHUNK = 128
N_GROUPS = 4
GROUP_W = 128
CHUNK_W = N_GROUPS * GROUP_W
N_HEADS = 4
QK_NOPE = 128
ROPE_DIM = 64
V_DIM = 128
Q_RANK = 384
KV_RANK = 256
MLA_W = N_HEADS * V_DIM
HEAD_PAD = 256
QK_W = N_HEADS * HEAD_PAD
V_EXT = 2 * V_DIM
VEXT_W = N_HEADS * V_EXT
ROPE_BASE = 10000.0
N_EXPERTS = 8
ALPHA = (2 * DEPTH) ** 0.25
EPS = 1e-6
ATTN_SCALE = math.log2(math.e) / math.sqrt(QK_NOPE + ROPE_DIM)
MOD_ROWS = 16
LANES = 128
VMEM_LIMIT = 56 * 1024 * 1024

SH_A, SC_A, G_A, SH_F, SC_F, G_F = range(6)


def _sigmoid(x):
    return 1.0 / (1.0 + jnp.exp(-x))


def _gelu_tanh(x):
    return 0.5 * x * (1.0 + jnp.tanh(math.sqrt(2.0 / math.pi) * (x + 0.044715 * (x * x * x))))


def _layernorm(y, g, b):
    mu = jnp.mean(y, axis=-1, keepdims=True)
    d = y - mu
    var = jnp.mean(d * d, axis=-1, keepdims=True)
    return d * lax.rsqrt(var + EPS) * g + b


def _rmsnorm(y, g):
    return y * lax.rsqrt(jnp.mean(y * y, axis=-1, keepdims=True) + EPS) * g


def _dot(a, b):
    return jnp.dot(a, b, preferred_element_type=F32)


def _dot_nt(a, b):
    return lax.dot_general(a, b, (((1,), (1,)), ((), ())), preferred_element_type=F32)


def _params(*sem):
    return pltpu.CompilerParams(dimension_semantics=sem, vmem_limit_bytes=VMEM_LIMIT)


def _const_spec(shape):
    nd = len(shape)
    return pl.BlockSpec(shape, lambda *_: (0,) * nd)


class _Stream:
    def __init__(self, batch, seq, mod_row0, per_row_mod, latent, tiles):
        self.tiles = tiles
        self.batch = batch
        self.seq = seq
        self.tokens = batch * seq
        self.mod_row0 = mod_row0
        self.per_row_mod = per_row_mod
        self.latent = latent

    def mod_spec(self, layer, which, tm):
        tiles_per_row = self.seq // tm
        row0, per_row = self.mod_row0, self.per_row_mod

        def index(i):
            r = row0 + (i // tiles_per_row if per_row else 0)
            return (layer, r, which, 0, 0)

        return pl.BlockSpec((None, None, None, 1, D_MODEL), index)


def _mod_body(c_ref, w_ref, b_ref, o_ref):
    a = c_ref[...]
    a = (a * _sigmoid(a)).astype(BF16)
    o_ref[...] = _dot(a, w_ref[...].astype(BF16)) + b_ref[...]


def _modulation(cond_rows, w_mod, b_mod):
    depth, _, width = w_mod.shape
    tn = 1536
    return pl.pallas_call(
        _mod_body,
        grid=(depth, width // tn),
        in_specs=[
            pl.BlockSpec((MOD_ROWS, D_MODEL), lambda l, j: (0, 0)),
            pl.BlockSpec((None, D_MODEL, tn), lambda l, j: (l, 0, j)),
            pl.BlockSpec((None, 1, tn), lambda l, j: (l, 0, j)),
        ],
        out_specs=pl.BlockSpec((None, MOD_ROWS, tn), lambda l, j: (l, 0, j)),
        out_shape=jax.ShapeDtypeStruct((depth, MOD_ROWS, width), F32),
        compiler_params=_params("parallel", "parallel"),
        name="modulation",
    )(cond_rows, w_mod, b_mod.reshape(depth, 1, width))


def _store_keys_values(kv, kr, kcat_ref, vext_ref):
    ones = jnp.ones((kv.shape[0], V_EXT - V_DIM), BF16)
    for hd in range(N_HEADS):
        a = hd * HEAD_PAD
        kcat_ref[:, a:a + QK_NOPE] = kv[:, hd * QK_NOPE:(hd + 1) * QK_NOPE].astype(BF16)
        kcat_ref[:, a + QK_NOPE:a + HEAD_PAD] = kr
        b = hd * V_EXT
        vext_ref[:, b:b + V_DIM] = kv[:, (N_HEADS + hd) * V_DIM:(N_HEADS + hd + 1) * V_DIM].astype(BF16)
        vext_ref[:, b + V_DIM:b + V_EXT] = ones


def _mixer_in_body(latent, tm, *refs):
    it = iter(refs)
    x_ref, sc_ref, sh_ref, win_ref, qg_ref, kvg_ref, wq_ref = (next(it) for _ in range(7))
    wkv_ref, lng_ref, ws_ref, bs_ref = (next(it) for _ in range(4))
    cos_ref, sin_ref = (next(it), next(it)) if latent else (None, None)
    co_ref, q_ref, kcat_ref, vext_ref = (next(it) for _ in range(4))
    ckv_ref, kr_ref = (None, None) if latent else (next(it), next(it))

    h = (x_ref[...] * (1.0 + sc_ref[...]) + sh_ref[...]).astype(BF16)
    p = _dot(h, win_ref[...])

    for g in range(N_GROUPS):
        cols = slice(g * GROUP_W, (g + 1) * GROUP_W)
        vg = _gelu_tanh(p[:, CHUNK_W + g * GROUP_W:CHUNK_W + (g + 1) * GROUP_W])
        mu = jnp.mean(vg, axis=-1, keepdims=True)
        d = vg - mu
        var = jnp.mean(d * d, axis=-1, keepdims=True)
        vn = (d * lax.rsqrt(var + EPS) * lng_ref[:, cols]).astype(BF16)
        ug = _gelu_tanh(p[:, cols])
        for c in range(tm // CHUNK):
            rows = slice(c * CHUNK, (c + 1) * CHUNK)
            z = _dot(ws_ref[g], vn[rows]) + bs_ref[g]
            co_ref[rows, cols] = (ug[rows] * z).astype(BF16)

    o = 2 * CHUNK_W
    cqn = _rmsnorm(p[:, o:o + Q_RANK], qg_ref[...]).astype(BF16)
    qc = _dot(cqn, wq_ref[...])
    q_rope = qc[:, N_HEADS * QK_NOPE:N_HEADS * (QK_NOPE + ROPE_DIM)]
    if latent:
        cos = cos_ref[...]
        sin = sin_ref[...]
        q_rope = q_rope * cos + qc[:, N_HEADS * (QK_NOPE + ROPE_DIM):] * sin
    for hd in range(N_HEADS):
        a = hd * HEAD_PAD
        q_ref[:, a:a + QK_NOPE] = (qc[:, hd * QK_NOPE:(hd + 1) * QK_NOPE] * ATTN_SCALE).astype(BF16)
        pair = q_rope[:, (hd // 2) * LANES:(hd // 2 + 1) * LANES]
        if hd % 2:
            pair = pltpu.roll(pair, ROPE_DIM, axis=1)
        q_ref[:, a + QK_NOPE:a + HEAD_PAD] = (pair * ATTN_SCALE).astype(BF16)

    o += Q_RANK
    ckvn = _rmsnorm(p[:, o:o + KV_RANK], kvg_ref[...])
    o += KV_RANK
    kr = p[:, o:o + LANES]
    if not latent:
        ckv_ref[...] = ckvn
        kr_ref[...] = kr[:, :ROPE_DIM]
    else:
        kr = kr * cos[:, :LANES] + p[:, o + LANES:o + 2 * LANES] * sin[:, :LANES]
    kr = kr.astype(BF16)
    kv = _dot(ckvn.astype(BF16), wkv_ref[...])
    _store_keys_values(kv, kr, kcat_ref, vext_ref)


def _mixer_in(stream, layer, x, mod, lw, rope, tm):
    latent = stream.latent
    t = stream.tokens
    row = lambda w: pl.BlockSpec((tm, w), lambda i: (i, 0))
    win = lw["w_in"]
    win_cols = win.shape[1] if latent else win.shape[1] - LANES
    wq = lw["wq"]
    wq_cols = wq.shape[1] if latent else wq.shape[1] - N_HEADS * ROPE_DIM
    ins = [x, mod, mod, win, lw["q_g"], lw["kv_g"], wq]
    specs = [row(D_MODEL), stream.mod_spec(layer, SC_A, tm), stream.mod_spec(layer, SH_A, tm),
             _const_spec((D_MODEL, win_cols)), _const_spec(lw["q_g"].shape), _const_spec(lw["kv_g"].shape),
             _const_spec((Q_RANK, wq_cols))]
    for name in ("wkv", "ln_v_g", "w_s", "b_s"):
        ins.append(lw[name])
        specs.append(_const_spec(lw[name].shape))
    if latent:
        tiles_per_seq = stream.seq // tm
        pos = pl.BlockSpec((tm, N_HEADS * ROPE_DIM), lambda i: (i % tiles_per_seq, 0))
        ins += [rope[0], rope[1]]
        specs += [pos, pos]
    out_shape = [jax.ShapeDtypeStruct((t, CHUNK_W), BF16), jax.ShapeDtypeStruct((t, QK_W), BF16),
                 jax.ShapeDtypeStruct((t, QK_W), BF16), jax.ShapeDtypeStruct((t, VEXT_W), BF16)]
    out_specs = [row(CHUNK_W), row(QK_W), row(QK_W), row(VEXT_W)]
    if not latent:
        out_shape += [jax.ShapeDtypeStruct((t, KV_RANK), F32), jax.ShapeDtypeStruct((t, ROPE_DIM), F32)]
        out_specs += [row(KV_RANK), row(ROPE_DIM)]
    return pl.pallas_call(
        functools.partial(_mixer_in_body, latent, tm),
        grid=(t // tm,),
        in_specs=specs,
        out_specs=out_specs,
        out_shape=out_shape,
        compiler_params=_params("parallel"),
        name="mixer_in",
    )(*ins)


KEY_CHUNK = 256


def _attend_body(has_ctx, *refs):
    if has_ctx:
        q_ref, k_ref, vext_ref, cckv_ref, ckr_ref, wkv_ref, o_ref, kctx_ref, vctx_ref = refs

        @pl.when(pl.program_id(1) == 0)
        def _():
            kv = _dot(cckv_ref[...].astype(BF16), wkv_ref[...])
            _store_keys_values(kv, ckr_ref[...].astype(BF16), kctx_ref, vctx_ref)

        sources = [(kctx_ref, vctx_ref), (k_ref, vext_ref)]
    else:
        q_ref, k_ref, vext_ref, o_ref = refs
        sources = [(k_ref, vext_ref)]
    chunks = [(kr, vr, slice(c * KEY_CHUNK, (c + 1) * KEY_CHUNK))
              for kr, vr in sources for c in range(kr.shape[0] // KEY_CHUNK)]

    def scores(hd):
        qk = slice(hd * HEAD_PAD, (hd + 1) * HEAD_PAD)
        return [_dot_nt(q_ref[:, qk], kr[rows, qk]) for kr, _, rows in chunks]

    s = scores(0)
    for hd in range(N_HEADS):
        m = jnp.max(functools.reduce(jnp.maximum, s), axis=-1, keepdims=True)
        s_next, acc = [], None
        for c, (kr, vr, rows) in enumerate(chunks):
            if hd + 1 < N_HEADS:
                qk = slice((hd + 1) * HEAD_PAD, (hd + 2) * HEAD_PAD)
                s_next.append(_dot_nt(q_ref[:, qk], kr[rows, qk]))
            p = jnp.exp2(s[c] - m).astype(BF16)
            part = _dot(p, vr[rows, hd * V_EXT:(hd + 1) * V_EXT])
            acc = part if acc is None else acc + part
        s = s_next
        o_ref[:, hd * V_DIM:(hd + 1) * V_DIM] = (acc[:, :V_DIM] / acc[:, V_DIM:]).astype(BF16)


def _attend(stream, layer, q, kcat, vext, lw, ctx, tq):
    n = stream.seq
    nq = n // tq
    has_ctx = stream.latent
    ins = [q, kcat, vext]
    specs = [pl.BlockSpec((tq, QK_W), lambda b, i: (b * nq + i, 0)),
             pl.BlockSpec((n, QK_W), lambda b, i: (b, 0)),
             pl.BlockSpec((n, VEXT_W), lambda b, i: (b, 0))]
    scratch = []
    if has_ctx:
        cache_ckv, cache_kr = ctx
        past = cache_ckv.shape[2]
        ins += [cache_ckv, cache_kr, lw["wkv"]]
        specs += [pl.BlockSpec((None, None, past, KV_RANK), lambda b, i: (b, layer, 0, 0)),
                  pl.BlockSpec((None, None, past, LANES), lambda b, i: (b, layer, 0, 0)),
                  _const_spec(lw["wkv"].shape)]
        scratch = [pltpu.VMEM((past, QK_W), BF16), pltpu.VMEM((past, VEXT_W), BF16)]
    return pl.pallas_call(
        functools.partial(_attend_body, has_ctx),
        grid=(stream.batch, nq),
        in_specs=specs,
        out_specs=pl.BlockSpec((tq, MLA_W), lambda b, i: (b * nq + i, 0)),
        out_shape=jax.ShapeDtypeStruct((stream.tokens, MLA_W), BF16),
        scratch_shapes=scratch,
        compiler_params=_params("parallel", "arbitrary"),
        name="attend",
    )(*ins)


MIX_PARTS = 4


def _mixer_residual(co_ref, mo_ref, wo1_ref, wo2_ref, x_ref, ga_ref, g_ref, b_ref):
    rp = x_ref.shape[0] // MIX_PARTS
    parts = [slice(k * rp, (k + 1) * rp) for k in range(MIX_PARTS)]
    mix = [_dot(co_ref[r], wo1_ref[...]) + _dot(mo_ref[r], wo2_ref[...]) for r in parts]
    x1 = [_layernorm(ALPHA * x_ref[r] + ga_ref[...] * mix[k], g_ref[...], b_ref[...])
          for k, r in enumerate(parts)]
    return parts, x1


def _mix_out_body(co_ref, mo_ref, wo1_ref, wo2_ref, x_ref, ga_ref, scf_ref, shf_ref, g_ref, b_ref,
                  rcat_ref, x1_ref, h2_ref, meta_ref, cnt_ref):
    parts, x1 = _mixer_residual(co_ref, mo_ref, wo1_ref, wo2_ref, x_ref, ga_ref, g_ref, b_ref)
    rp = x_ref.shape[0] // MIX_PARTS
    for k, r in enumerate(parts):
        x1_ref[r] = x1[k]
    h2 = [v * (1.0 + scf_ref[...]) + shf_ref[...] for v in x1]
    hh = [v.astype(BF16) for v in h2]
    for k, r in enumerate(parts):
        h2_ref[r] = hh[k]
    hl = [(h2[k] - hh[k].astype(F32)).astype(BF16) for k in range(MIX_PARTS)]
    ra = [_dot(v, rcat_ref[...]) for v in hh]
    rb = [_dot(v, rcat_ref[...]) for v in hl]
    logits = [ra[k][:, :LANES] + (ra[k][:, LANES:] + (rb[k][:, :LANES] + rb[k][:, LANES:]))
              for k in range(MIX_PARTS)]
    lane = lax.broadcasted_iota(jnp.int32, (rp, LANES), 1).astype(F32)
    neg = -jnp.inf
    lg = [jnp.where(lane < N_EXPERTS, v, neg) for v in logits]
    m1 = [jnp.max(v, axis=-1, keepdims=True) for v in lg]
    i1 = [jnp.min(jnp.where(lg[k] == m1[k], lane, float(LANES)), axis=-1, keepdims=True)
          for k in range(MIX_PARTS)]
    lg2 = [jnp.where(lane == i1[k], neg, lg[k]) for k in range(MIX_PARTS)]
    m2 = [jnp.max(v, axis=-1, keepdims=True) for v in lg2]
    i2 = [jnp.min(jnp.where(lg2[k] == m2[k], lane, float(LANES)), axis=-1, keepdims=True)
          for k in range(MIX_PARTS)]
    picked = [jnp.where(lane == i1[k], 1.0, jnp.where(lane == i2[k], 1.0, 0.0)) for k in range(MIX_PARTS)]
    earlier = jnp.where(lax.broadcasted_iota(jnp.int32, (rp, rp), 0)
                        > lax.broadcasted_iota(jnp.int32, (rp, rp), 1), 1.0, 0.0).astype(BF16)
    before = jnp.zeros((1, LANES), F32)
    for k, r in enumerate(parts):
        rank = _dot(earlier, picked[k].astype(BF16)) + before
        before = before + jnp.sum(picked[k], axis=0, keepdims=True)
        rank1 = jnp.sum(jnp.where(lane == i1[k], rank, 0.0), axis=-1, keepdims=True)
        rank2 = jnp.sum(jnp.where(lane == i2[k], rank, 0.0), axis=-1, keepdims=True)
        e2 = jnp.exp(m2[k] - m1[k])
        den = 1.0 + e2
        meta = jnp.zeros((rp, LANES), F32)
        for j, val in enumerate((i1[k], i2[k], 1.0 / den, e2 / den, rank1, rank2)):
            meta = jnp.where(lane == float(j), val, meta)
        meta_ref[r] = meta
    cnt_ref[...] = before


def _mix_out(stream, layer, co, mo, x, mod, lw, tm):
    t = stream.tokens
    row = lambda w: pl.BlockSpec((tm, w), lambda i: (i, 0))
    vec = _const_spec((1, D_MODEL))
    return pl.pallas_call(
        _mix_out_body,
        grid=(t // tm,),
        in_specs=[row(CHUNK_W), row(MLA_W), _const_spec(lw["wo_chunk"].shape), _const_spec(lw["wo_mla"].shape),
                  row(D_MODEL), stream.mod_spec(layer, G_A, tm), stream.mod_spec(layer, SC_F, tm),
                  stream.mod_spec(layer, SH_F, tm), vec, vec, _const_spec(lw["router"].shape)],
        out_specs=[row(D_MODEL), row(D_MODEL), row(LANES), pl.BlockSpec((None, 1, LANES), lambda i: (i, 0, 0))],
        out_shape=[jax.ShapeDtypeStruct((t, D_MODEL), F32), jax.ShapeDtypeStruct((t, D_MODEL), BF16),
                   jax.ShapeDtypeStruct((t, LANES), F32), jax.ShapeDtypeStruct((t // tm, 1, LANES), F32)],
        compiler_params=_params("parallel"),
        name="mix_out",
    )(co, mo, lw["wo_chunk"], lw["wo_mla"], x, mod, mod, mod, lw["ln_mix_g"], lw["ln_mix_b"], lw["router"])


def _mix_ffn_body(co_ref, mo_ref, wo1_ref, wo2_ref, x_ref, ga_ref, scf_ref, shf_ref, gf_ref,
                  gm_ref, bm_ref, g_ref, b_ref, wg_ref, wu_ref, wd_ref, o_ref):
    _, x1 = _mixer_residual(co_ref, mo_ref, wo1_ref, wo2_ref, x_ref, ga_ref, gm_ref, bm_ref)
    x1 = jnp.concatenate(x1, axis=0)
    h = (x1 * (1.0 + scf_ref[...]) + shf_ref[...]).astype(BF16)
    gate = _dot(h, wg_ref[...])
    up = _dot(h, wu_ref[...])
    a = (gate * _sigmoid(gate) * up).astype(BF16)
    f = _dot(a, wd_ref[...])
    o_ref[...] = _layernorm(ALPHA * x1 + gf_ref[...] * f, g_ref[...], b_ref[...])


def _mix_ffn(stream, layer, co, mo, x, mod, lw, tm):
    t = stream.tokens
    row = lambda w: pl.BlockSpec((tm, w), lambda i: (i, 0))
    vec = _const_spec((1, D_MODEL))
    resident = lambda a: pl.BlockSpec(a.shape, lambda i: (0, 0), pipeline_mode=pl.Buffered(1))
    mods = [stream.mod_spec(layer, which, tm) for which in (G_A, SC_F, SH_F, G_F)]
    return pl.pallas_call(
        _mix_ffn_body,
        grid=(t // tm,),
        in_specs=[row(CHUNK_W), row(MLA_W), resident(lw["wo_chunk"]), resident(lw["wo_mla"]), row(D_MODEL),
                  *mods, vec, vec, vec, vec,
                  resident(lw["ffn_gate"]), resident(lw["ffn_up"]), resident(lw["ffn_down"])],
        out_specs=row(D_MODEL),
        out_shape=jax.ShapeDtypeStruct((t, D_MODEL), F32),
        compiler_params=_params("parallel"),
        name="mix_ffn",
    )(co, mo, lw["wo_chunk"], lw["wo_mla"], x, mod, mod, mod, mod,
      lw["ln_mix_g"], lw["ln_mix_b"], lw["ln_ffn_g"], lw["ln_ffn_b"],
      lw["ffn_gate"], lw["ffn_up"], lw["ffn_down"])


RUN_ALIGN = 16
ROUTE_TILE = 512
EXPERT_TILE = 512


def _plan_body(cap, tile, cnt_ref, off_ref, run_ref, base_ref, te_ref, tb_ref, rows_ref, nv_ref):
    nw = cnt_ref.shape[0]
    lane = lax.broadcasted_iota(jnp.int32, (1, LANES), 1).astype(F32)
    run = jnp.floor((cnt_ref[...] + (RUN_ALIGN - 1)) * (1.0 / RUN_ALIGN)) * RUN_ALIGN
    before = jnp.where(lax.broadcasted_iota(jnp.int32, (LANES, LANES), 0)
                       < lax.broadcasted_iota(jnp.int32, (LANES, LANES), 1), 1.0, 0.0).astype(BF16)
    upto = jnp.where(lax.broadcasted_iota(jnp.int32, (LANES, LANES), 0)
                     <= lax.broadcasted_iota(jnp.int32, (LANES, LANES), 1), 1.0, 0.0).astype(BF16)
    above = jnp.where(lax.broadcasted_iota(jnp.int32, (nw, nw), 0)
                      > lax.broadcasted_iota(jnp.int32, (nw, nw), 1), 1.0, 0.0).astype(BF16)
    run_b = run.astype(BF16)
    off_ref[...] = _dot(run_b, before).astype(jnp.int32)
    run_ref[...] = run.astype(jnp.int32)
    base_ref[...] = (lane * cap + _dot(above, run_b)).astype(jnp.int32)
    totals = jnp.sum(run, axis=0, keepdims=True)
    tiles = jnp.floor((totals + (tile - 1)) * (1.0 / tile))
    cum = _dot(tiles.astype(BF16), upto)
    start = cum - tiles
    n_valid = jnp.max(cum, axis=-1, keepdims=True)
    step = jnp.minimum(lane, n_valid - 1.0)
    pick = lambda v, e: jnp.sum(jnp.where(lane == float(e), v, 0.0), axis=-1, keepdims=True)
    te = jnp.zeros_like(step)
    for e in range(N_EXPERTS):
        te = te + jnp.where(pick(cum, e) <= step, 1.0, 0.0)
    start_te, total_te = jnp.zeros_like(step), jnp.zeros_like(step)
    for e in range(N_EXPERTS):
        start_te = jnp.where(te == float(e), pick(start, e), start_te)
        total_te = jnp.where(te == float(e), pick(totals, e), total_te)
    tj = step - start_te
    te_ref[...] = te.astype(jnp.int32)
    tb_ref[...] = (te * (cap // tile) + tj).astype(jnp.int32)
    rows_ref[...] = jnp.minimum(float(tile), total_te - tj * tile).astype(jnp.int32)
    nv_ref[...] = jnp.broadcast_to(n_valid, (1, LANES)).astype(jnp.int32)


def _route_plan(counts, tokens, tm, tile):
    nw = tokens // tm
    cnt = jnp.concatenate([c.reshape(-1, LANES) for c in counts], axis=0)
    worst_pad = (RUN_ALIGN - 1) * nw
    cap = pl.cdiv(tokens + worst_pad, tile) * tile
    n_tiles = (2 * tokens + N_EXPERTS * worst_pad) // tile + N_EXPERTS
    assert n_tiles <= LANES and tm % RUN_ALIGN == 0
    per_run = jax.ShapeDtypeStruct((nw, LANES), jnp.int32)
    per_step = jax.ShapeDtypeStruct((1, LANES), jnp.int32)
    off, run, base, te, tb, rows, nv = pl.pallas_call(
        functools.partial(_plan_body, cap, tile),
        out_shape=[per_run, per_run, per_run, per_step, per_step, per_step, per_step],
        name="moe_plan",
    )(cnt)
    return dict(
        tm=tm, cap=cap, tile=tile, n_tiles=n_tiles, buf_rows=2 * tm + N_EXPERTS * RUN_ALIGN,
        off=off.reshape(-1), run=run.reshape(-1), base=base.reshape(-1),
        tile_expert=te.reshape(-1), tile_block=tb.reshape(-1), tile_rows=rows.reshape(-1),
        n_valid=nv.reshape(-1))


def _run_piece_sizes(tm):
    sizes, b = [], RUN_ALIGN
    while b <= tm:
        sizes.append(b)
        b *= 2
    return sizes


def _row_run_copies(src, dst, src_off, dst_off, n_rows, sem, sizes, action):
    for b in sizes:
        done = n_rows & (-2 * b)
        cp = pltpu.make_async_copy(
            src.at[pl.ds(pl.multiple_of(src_off + done, RUN_ALIGN), b)],
            dst.at[pl.ds(pl.multiple_of(dst_off + done, RUN_ALIGN), b)], sem)

        @pl.when((n_rows & b) != 0)
        def _():
            action(cp)


def _local_rows(meta, off_ref, w):
    i1, i2 = meta[:, 0:1], meta[:, 1:2]
    o1, o2 = jnp.zeros_like(i1), jnp.zeros_like(i2)
    for e in range(N_EXPERTS):
        off_e = off_ref[w * LANES + e].astype(F32)
        o1 = jnp.where(i1 == float(e), off_e, o1)
        o2 = jnp.where(i2 == float(e), off_e, o2)
    return (o1 + meta[:, 4:5]).astype(jnp.int32), (o2 + meta[:, 5:6]).astype(jnp.int32)


def _start(cp):
    cp.start()


def _wait(cp):
    cp.wait()


def _dispatch_body(tm, nw, tile0, off_ref, run_ref, base_ref, h_ref, meta_ref, *refs):
    xs_ref, xw_ref, sems = refs[-3:]
    w = pl.program_id(0)
    slot = w % 2
    sizes = _run_piece_sizes(tm)

    def push(win, buf, action):
        for e in range(N_EXPERTS):
            k = (tile0 + win) * LANES + e
            _row_run_copies(xw_ref.at[buf], xs_ref, off_ref[k], base_ref[k], run_ref[k],
                            sems.at[buf], sizes, action)

    r1, r2 = _local_rows(meta_ref[...], off_ref, tile0 + w)
    col = lax.broadcasted_iota(jnp.int32, (tm, xw_ref.shape[1]), 1)
    sel = jnp.where(col == r1, 1.0, jnp.where(col == r2, 1.0, 0.0)).astype(BF16)
    xw = lax.dot_general(sel, h_ref[...], (((0,), (0,)), ((), ())), preferred_element_type=F32)
    xw_ref[slot] = xw.astype(BF16)
    push(w, slot, _start)

    @pl.when(w > 0)
    def _():
        push(w - 1, 1 - slot, _wait)

    @pl.when(w == nw - 1)
    def _():
        push(w, slot, _wait)


def _dispatch(h2, meta, plan, tile0, xs=None):
    tm = plan["tm"]
    nw = h2.shape[0] // tm
    ins = [plan["off"], plan["run"], plan["base"], h2, meta]
    specs = [pl.BlockSpec((tm, D_MODEL), lambda i, *_: (i, 0)),
             pl.BlockSpec((tm, LANES), lambda i, *_: (i, 0))]
    aliases = {}
    if xs is not None:
        aliases = {len(ins): 0}
        ins.append(xs)
        specs.append(pl.BlockSpec(memory_space=pl.ANY))
    return pl.pallas_call(
        functools.partial(_dispatch_body, tm, nw, tile0),
        grid_spec=pltpu.PrefetchScalarGridSpec(
            num_scalar_prefetch=3,
            grid=(nw,),
            in_specs=specs,
            out_specs=pl.BlockSpec(memory_space=pl.ANY),
            scratch_shapes=[pltpu.VMEM((2, plan["buf_rows"], D_MODEL), BF16),
                            pltpu.SemaphoreType.DMA((2,))]),
        out_shape=jax.ShapeDtypeStruct((N_EXPERTS * plan["cap"], D_MODEL), BF16),
        input_output_aliases=aliases,
        compiler_params=_params("arbitrary"),
        name="moe_dispatch",
    )(*ins)


def _experts_body(te_ref, tb_ref, rows_ref, nv_ref, x_ref, wg_ref, wu_ref, wd_ref, o_ref):
    i = pl.program_id(0)
    rows = rows_ref[i]
    valid = i < nv_ref[0]
    half = x_ref.shape[0] // 2

    def swiglu(n):
        x = x_ref[0:n, :]
        live = lax.broadcasted_iota(jnp.int32, (n, 1), 0) < rows
        x = jnp.where(live, x, jnp.zeros_like(x))
        gate = _dot(x, wg_ref[...])
        up = _dot(x, wu_ref[...])
        a = (gate * _sigmoid(gate) * up).astype(BF16)
        o_ref[0:n, :] = _dot(a, wd_ref[...]).astype(BF16)

    @pl.when(jnp.logical_and(valid, rows > half))
    def _():
        swiglu(2 * half)

    @pl.when(jnp.logical_and(valid, rows <= half))
    def _():
        swiglu(half)


def _experts(xs, plan, lw):
    dff = lw["moe_gate"].shape[2]
    rows = pl.BlockSpec((plan["tile"], D_MODEL), lambda i, te, tb, *_: (tb[i], 0))
    return pl.pallas_call(
        _experts_body,
        grid_spec=pltpu.PrefetchScalarGridSpec(
            num_scalar_prefetch=4,
            grid=(plan["n_tiles"],),
            in_specs=[rows,
                      pl.BlockSpec((None, D_MODEL, dff), lambda i, te, *_: (te[i], 0, 0)),
                      pl.BlockSpec((None, D_MODEL, dff), lambda i, te, *_: (te[i], 0, 0)),
                      pl.BlockSpec((None, dff, D_MODEL), lambda i, te, *_: (te[i], 0, 0))],
            out_specs=rows),
        out_shape=jax.ShapeDtypeStruct(xs.shape, BF16),
        compiler_params=_params("arbitrary"),
        name="moe_experts",
    )(plan["tile_expert"], plan["tile_block"], plan["tile_rows"], plan["n_valid"],
      xs, lw["moe_gate"], lw["moe_up"], lw["moe_down"])


def _combine_body(tm, nw, tile0, off_ref, run_ref, base_ref, ys_ref, meta_ref, x_ref, gf_ref, g_ref,
                  b_ref, o_ref, yw_ref, sems):
    w = pl.program_id(0)
    slot = w % 2
    sizes = _run_piece_sizes(tm)

    def fetch(win, buf, action):
        for e in range(N_EXPERTS):
            k = (tile0 + win) * LANES + e
            _row_run_copies(ys_ref, yw_ref.at[buf], base_ref[k], off_ref[k], run_ref[k],
                            sems.at[buf], sizes, action)

    @pl.when(w == 0)
    def _():
        yw_ref[...] = jnp.zeros_like(yw_ref)
        fetch(0, 0, _start)

    @pl.when(w + 1 < nw)
    def _():
        fetch(w + 1, 1 - slot, _start)

    fetch(w, slot, _wait)
    rp = tm // MIX_PARTS
    parts = [slice(k * rp, (k + 1) * rp) for k in range(MIX_PARTS)]
    meta = [meta_ref[r] for r in parts]
    picks = [_local_rows(v, off_ref, tile0 + w) for v in meta]
    col = lax.broadcasted_iota(jnp.int32, (rp, yw_ref.shape[1]), 1)
    yw = yw_ref[slot]
    gate = [jnp.where(col == r1, v[:, 2:3], jnp.where(col == r2, v[:, 3:4], 0.0)).astype(BF16)
            for v, (r1, r2) in zip(meta, picks)]
    f = [_dot(g, yw) for g in gate]
    for k, r in enumerate(parts):
        o_ref[r] = _layernorm(ALPHA * x_ref[r] + gf_ref[...] * f[k], g_ref[...], b_ref[...])


def _combine(stream, layer, ys, meta, x1, mod, plan, lw, tile0):
    tm = plan["tm"]
    nw = stream.tokens // tm
    row = lambda width: pl.BlockSpec((tm, width), lambda i, *_: (i, 0))
    vec = pl.BlockSpec((1, D_MODEL), lambda i, *_: (0, 0))
    mod_spec = stream.mod_spec(layer, G_F, tm)
    mod_spec = pl.BlockSpec(mod_spec.block_shape, lambda i, *_, f=mod_spec.index_map: f(i))
    return pl.pallas_call(
        functools.partial(_combine_body, tm, nw, tile0),
        grid_spec=pltpu.PrefetchScalarGridSpec(
            num_scalar_prefetch=3,
            grid=(nw,),
            in_specs=[pl.BlockSpec(memory_space=pl.ANY), row(LANES), row(D_MODEL), mod_spec, vec, vec],
            out_specs=row(D_MODEL),
            scratch_shapes=[pltpu.VMEM((2, plan["buf_rows"], D_MODEL), BF16),
                            pltpu.SemaphoreType.DMA((2,))]),
        out_shape=jax.ShapeDtypeStruct((stream.tokens, D_MODEL), F32),
        compiler_params=_params("arbitrary"),
        name="moe_combine",
    )(plan["off"], plan["run"], plan["base"], ys, meta, x1, mod, lw["ln_ffn_g"], lw["ln_ffn_b"])


def _rope_tables(n_tokens):
    rows = n_tokens // GRID_W
    row = np.repeat(np.arange(rows, dtype=np.float64), GRID_W)
    col = np.tile(np.arange(GRID_W, dtype=np.float64), rows)
    half = ROPE_DIM // 2
    inv_freq = ROPE_BASE ** (-np.arange(0, half, 2, dtype=np.float64) / half)
    ar = row[:, None] * inv_freq[None, :]
    ac = col[:, None] * inv_freq[None, :]
    cos = np.concatenate([np.cos(ar), np.cos(ar), np.cos(ac), np.cos(ac)], axis=-1)
    sin = np.concatenate([-np.sin(ar), np.sin(ar), -np.sin(ac), np.sin(ac)], axis=-1)
    return jnp.asarray(np.tile(cos, (1, N_HEADS)), F32), jnp.asarray(np.tile(sin, (1, N_HEADS)), F32)


def _swap_rope_halves(a):
    q = ROPE_DIM // 4
    return jnp.concatenate([a[..., q:2 * q], a[..., :q], a[..., 3 * q:], a[..., 2 * q:3 * q]], axis=-1)


def _layer_weights(l, p):
    w_in = p["w_in"][l].astype(BF16)
    o_kr = 2 * CHUNK_W + Q_RANK + KV_RANK
    zpad = jnp.zeros((D_MODEL, LANES - ROPE_DIM), BF16)
    w_in_ext = jnp.concatenate(
        [w_in, zpad, _swap_rope_halves(w_in[:, o_kr:o_kr + ROPE_DIM]), zpad], axis=1)

    w_uq = p["w_uq"][l].astype(BF16).reshape(Q_RANK, N_HEADS, QK_NOPE + ROPE_DIM)
    wq_rope = w_uq[:, :, QK_NOPE:]
    wq = jnp.concatenate([w_uq[:, :, :QK_NOPE].reshape(Q_RANK, N_HEADS * QK_NOPE),
                          wq_rope.reshape(Q_RANK, N_HEADS * ROPE_DIM),
                          _swap_rope_halves(wq_rope).reshape(Q_RANK, N_HEADS * ROPE_DIM)], axis=1)

    w_ukv = p["w_ukv"][l].astype(BF16).reshape(KV_RANK, N_HEADS, QK_NOPE + V_DIM)
    wkv = jnp.concatenate([w_ukv[:, :, :QK_NOPE].reshape(KV_RANK, N_HEADS * QK_NOPE),
                           w_ukv[:, :, QK_NOPE:].reshape(KV_RANK, MLA_W)], axis=1)
    w_out = p["w_out"][l]
    lw = {
        "w_in": w_in_ext,
        "q_g": p["q_norm_g"][l].reshape(1, Q_RANK),
        "kv_g": p["kv_norm_g"][l].reshape(1, KV_RANK),
        "wq": wq,
        "wkv": wkv,
        "ln_v_g": p["chunk_ln_g"][l].reshape(1, CHUNK_W),
        "w_s": p["w_spatial"][l].astype(BF16),
        "b_s": jnp.broadcast_to(p["b_spatial"][l][:, :, None], (N_GROUPS, CHUNK, GROUP_W)),
        "wo_chunk": w_out[:CHUNK_W].astype(BF16),
        "wo_mla": w_out[CHUNK_W:].astype(BF16),
        "ln_mix_g": p["ln_mix_g"][l].reshape(1, D_MODEL),
        "ln_mix_b": p["ln_mix_b"][l].reshape(1, D_MODEL),
        "ln_ffn_g": p["ln_ffn_g"][l].reshape(1, D_MODEL),
        "ln_ffn_b": p["ln_ffn_b"][l].reshape(1, D_MODEL),
    }
    if l % 2 == 0:
        lw["ffn_gate"] = p["ffn_w_gate"][l // 2].astype(BF16)
        lw["ffn_up"] = p["ffn_w_up"][l // 2].astype(BF16)
        lw["ffn_down"] = p["ffn_w_down"][l // 2].astype(BF16)
    else:
        r = jnp.pad(p["router_w"][l // 2], ((0, 0), (0, LANES - N_EXPERTS)))
        r_hi = r.astype(BF16)
        lw["router"] = jnp.concatenate([r_hi, (r - r_hi.astype(F32)).astype(BF16)], axis=1)
        lw["moe_gate"] = p["moe_w_gate"][l // 2].astype(BF16)
        lw["moe_up"] = p["moe_w_up"][l // 2].astype(BF16)
        lw["moe_down"] = p["moe_w_down"][l // 2].astype(BF16)
    return lw


def _trunk(streams, xs, mod, weights, rope, ctx):
    caches = []
    for l in range(DEPTH):
        lw = weights[l]
        mixed = []
        for stream, x in zip(streams, xs):
            outs = _mixer_in(stream, l, x, mod, lw, rope, stream.tiles["mixer_in"])
            co, q, kcat, vext = outs[:4]
            if not stream.latent:
                caches.append(outs[4:])
            mixed.append((co, _attend(stream, l, q, kcat, vext, lw, ctx, stream.tiles["attend"])))
        if l % 2 == 0:
            xs = [_mix_ffn(stream, l, co, mo, x, mod, lw, stream.tiles["ffn"])
                  for stream, x, (co, mo) in zip(streams, xs, mixed)]
            continue
        routed = [_mix_out(stream, l, co, mo, x, mod, lw, ROUTE_TILE)
                  for stream, x, (co, mo) in zip(streams, xs, mixed)]
        plan = _route_plan([r[3] for r in routed], sum(s.tokens for s in streams), ROUTE_TILE, EXPERT_TILE)
        tile0, sorted_rows = [], None
        for stream, (_, h2, meta, _) in zip(streams, routed):
            tile0.append(sum(s.tokens for s in streams[:len(tile0)]) // ROUTE_TILE)
            sorted_rows = _dispatch(h2, meta, plan, tile0[-1], sorted_rows)
        ys = _experts(sorted_rows, plan, lw)
        xs = [_combine(stream, l, ys, meta, x1, mod, plan, lw, t0)
              for stream, (x1, _, meta, _), t0 in zip(streams, routed, tile0)]
    return xs, caches


def kernel(x_prompt, x_sample, c, cache_ckv, cache_krope, c_ctx, w_mod, b_mod, w_in, q_norm_g, kv_norm_g, w_uq, w_ukv, chunk_ln_g, w_spatial, b_spatial, w_out, ln_mix_g, ln_mix_b, ln_ffn_g, ln_ffn_b, ffn_w_gate, ffn_w_up, ffn_w_down, router_w, moe_w_gate, moe_w_up, moe_w_down):
    p = dict(w_in=w_in, q_norm_g=q_norm_g, kv_norm_g=kv_norm_g, w_uq=w_uq, w_ukv=w_ukv,
             chunk_ln_g=chunk_ln_g, w_spatial=w_spatial, b_spatial=b_spatial, w_out=w_out,
             ln_mix_g=ln_mix_g, ln_mix_b=ln_mix_b, ln_ffn_g=ln_ffn_g, ln_ffn_b=ln_ffn_b,
             ffn_w_gate=ffn_w_gate, ffn_w_up=ffn_w_up, ffn_w_down=ffn_w_down, router_w=router_w,
             moe_w_gate=moe_w_gate, moe_w_up=moe_w_up, moe_w_down=moe_w_down)
    weights = [_layer_weights(l, p) for l in range(DEPTH)]

    batch, seq, _ = x_prompt.shape
    dec_batch, dec_seq, _ = x_sample.shape
    cond_rows = jnp.concatenate(
        [c_ctx[None, :], c, jnp.zeros((MOD_ROWS - 1 - dec_batch, D_MODEL), F32)], axis=0)
    mod = _modulation(cond_rows, w_mod, b_mod).reshape(DEPTH, MOD_ROWS, 6, 1, D_MODEL)

    prompt = _Stream(batch, seq, mod_row0=0, per_row_mod=False, latent=False,
                     tiles=dict(mixer_in=512, attend=256, ffn=512))
    sample = _Stream(dec_batch, dec_seq, mod_row0=1, per_row_mod=True, latent=True,
                     tiles=dict(mixer_in=1024, attend=1024, ffn=512))
    rope = _rope_tables(dec_seq)
    ctx = (cache_ckv, jnp.pad(cache_krope, ((0, 0), (0, 0), (0, 0), (0, LANES - ROPE_DIM))))
    (y_prompt, y_sample), caches = _trunk(
        [prompt, sample],
        [x_prompt.reshape(batch * seq, D_MODEL), x_sample.reshape(dec_batch * dec_seq, D_MODEL)],
        mod, weights, rope, ctx)
    new_ckv = jnp.stack([ck.reshape(batch, seq, KV_RANK) for ck, _ in caches], axis=1)
    new_krope = jnp.stack([kr.reshape(batch, seq, ROPE_DIM) for _, kr in caches], axis=1)
    return (y_prompt.reshape(batch, seq, D_MODEL), y_sample.reshape(dec_batch, dec_seq, D_MODEL),
            new_ckv, new_krope)
```

```python
import functools
import math

import jax
import jax.numpy as jnp
import numpy as np
from jax import lax
from jax.experimental import pallas as pl
from jax.experimental.pallas import tpu as pltpu

F32 = jnp.float32
BF16 = jnp.bfloat16

D_MODEL = 1024
DEPTH = 2
GRID_W = 64
CHUNK = 128
N_GROUPS = 4
GROUP_W = 128
CHUNK_W = N_GROUPS * GROUP_W
N_HEADS = 4
QK_NOPE = 128
ROPE_DIM = 64
V_DIM = 128
Q_RANK = 384
KV_RANK = 256
MLA_W = N_HEADS * V_DIM
HEAD_PAD = 256
QK_W = N_HEADS * HEAD_PAD
V_EXT = 2 * V_DIM
VEXT_W = N_HEADS * V_EXT
ROPE_BASE = 10000.0
N_EXPERTS = 8
ALPHA = (2 * DEPTH) ** 0.25
EPS = 1e-6
ATTN_SCALE = math.log2(math.e) / math.sqrt(QK_NOPE + ROPE_DIM)
MOD_ROWS = 16
LANES = 128
VMEM_LIMIT = 56 * 1024 * 1024

SH_A, SC_A, G_A, SH_F, SC_F, G_F = range(6)


def _sigmoid(x):
    return 1.0 / (1.0 + jnp.exp(-x))


def _gelu_tanh(x):
    return 0.5 * x * (1.0 + jnp.tanh(math.sqrt(2.0 / math.pi) * (x + 0.044715 * (x * x * x))))


def _layernorm(y, g, b):
    mu = jnp.mean(y, axis=-1, keepdims=True)
    d = y - mu
    var = jnp.mean(d * d, axis=-1, keepdims=True)
    return d * lax.rsqrt(var + EPS) * g + b


def _rmsnorm(y, g):
    return y * lax.rsqrt(jnp.mean(y * y, axis=-1, keepdims=True) + EPS) * g


def _dot(a, b):
    return jnp.dot(a, b, preferred_element_type=F32)


def _dot_nt(a, b):
    return lax.dot_general(a, b, (((1,), (1,)), ((), ())), preferred_element_type=F32)


def _params(*sem):
    return pltpu.CompilerParams(dimension_semantics=sem, vmem_limit_bytes=VMEM_LIMIT)


def _const_spec(shape):
    nd = len(shape)
    return pl.BlockSpec(shape, lambda *_: (0,) * nd)


class _Stream:
    def __init__(self, batch, seq, mod_row0, per_row_mod, latent, tiles):
        self.tiles = tiles
        self.batch = batch
        self.seq = seq
        self.tokens = batch * seq
        self.mod_row0 = mod_row0
        self.per_row_mod = per_row_mod
        self.latent = latent

    def mod_spec(self, layer, which, tm):
        tiles_per_row = self.seq // tm
        row0, per_row = self.mod_row0, self.per_row_mod

        def index(i):
            r = row0 + (i // tiles_per_row if per_row else 0)
            return (layer, r, which, 0, 0)

        return pl.BlockSpec((None, None, None, 1, D_MODEL), index)


def _mod_body(c_ref, w_ref, b_ref, o_ref):
    a = c_ref[...]
    a = (a * _sigmoid(a)).astype(BF16)
    o_ref[...] = _dot(a, w_ref[...].astype(BF16)) + b_ref[...]


def _modulation(cond_rows, w_mod, b_mod):
    depth, _, width = w_mod.shape
    tn = 1536
    return pl.pallas_call(
        _mod_body,
        grid=(depth, width // tn),
        in_specs=[
            pl.BlockSpec((MOD_ROWS, D_MODEL), lambda l, j: (0, 0)),
            pl.BlockSpec((None, D_MODEL, tn), lambda l, j: (l, 0, j)),
            pl.BlockSpec((None, 1, tn), lambda l, j: (l, 0, j)),
        ],
        out_specs=pl.BlockSpec((None, MOD_ROWS, tn), lambda l, j: (l, 0, j)),
        out_shape=jax.ShapeDtypeStruct((depth, MOD_ROWS, width), F32),
        compiler_params=_params("parallel", "parallel"),
        name="modulation",
    )(cond_rows, w_mod, b_mod.reshape(depth, 1, width))


def _store_keys_values(kv, kr, kcat_ref, vext_ref):
    ones = jnp.ones((kv.shape[0], V_EXT - V_DIM), BF16)
    for hd in range(N_HEADS):
        a = hd * HEAD_PAD
        kcat_ref[:, a:a + QK_NOPE] = kv[:, hd * QK_NOPE:(hd + 1) * QK_NOPE].astype(BF16)
        kcat_ref[:, a + QK_NOPE:a + HEAD_PAD] = kr
        b = hd * V_EXT
        vext_ref[:, b:b + V_DIM] = kv[:, (N_HEADS + hd) * V_DIM:(N_HEADS + hd + 1) * V_DIM].astype(BF16)
        vext_ref[:, b + V_DIM:b + V_EXT] = ones


def _mixer_in_body(latent, tm, n_aliased, *refs):
    it = iter(refs)
    x_ref, sc_ref, sh_ref, win_ref, qg_ref, kvg_ref, wq_ref = (next(it) for _ in range(7))
    wkv_ref, lng_ref, ws_ref, bs_ref = (next(it) for _ in range(4))
    cos_ref, sin_ref = (next(it), next(it)) if latent else (None, None)
    for _ in range(n_aliased):
        next(it)
    co_ref, q_ref, kcat_ref, vext_ref = (next(it) for _ in range(4))
    ckv_ref, kr_ref = (None, None) if latent else (next(it), next(it))

    h = (x_ref[...] * (1.0 + sc_ref[...]) + sh_ref[...]).astype(BF16)
    p = _dot(h, win_ref[...])

    for g in range(N_GROUPS):
        cols = slice(g * GROUP_W, (g + 1) * GROUP_W)
        vg = _gelu_tanh(p[:, CHUNK_W + g * GROUP_W:CHUNK_W + (g + 1) * GROUP_W])
        mu = jnp.mean(vg, axis=-1, keepdims=True)
        d = vg - mu
        var = jnp.mean(d * d, axis=-1, keepdims=True)
        vn = (d * lax.rsqrt(var + EPS) * lng_ref[:, cols]).astype(BF16)
        ug = _gelu_tanh(p[:, cols])
        for c in range(tm // CHUNK):
            rows = slice(c * CHUNK, (c + 1) * CHUNK)
            z = _dot(ws_ref[g], vn[rows]) + bs_ref[g]
            co_ref[rows, cols] = (ug[rows] * z).astype(BF16)

    o = 2 * CHUNK_W
    cqn = _rmsnorm(p[:, o:o + Q_RANK], qg_ref[...]).astype(BF16)
    qc = _dot(cqn, wq_ref[...])
    q_rope = qc[:, N_HEADS * QK_NOPE:N_HEADS * (QK_NOPE + ROPE_DIM)]
    if latent:
        cos = cos_ref[...]
        sin = sin_ref[...]
        q_rope = q_rope * cos + qc[:, N_HEADS * (QK_NOPE + ROPE_DIM):] * sin
    for hd in range(N_HEADS):
        a = hd * HEAD_PAD
        q_ref[:, a:a + QK_NOPE] = (qc[:, hd * QK_NOPE:(hd + 1) * QK_NOPE] * ATTN_SCALE).astype(BF16)
        pair = q_rope[:, (hd // 2) * LANES:(hd // 2 + 1) * LANES]
        if hd % 2:
            pair = pltpu.roll(pair, ROPE_DIM, axis=1)
        q_ref[:, a + QK_NOPE:a + HEAD_PAD] = (pair * ATTN_SCALE).astype(BF16)

    o += Q_RANK
    ckvn = _rmsnorm(p[:, o:o + KV_RANK], kvg_ref[...])
    o += KV_RANK
    kr = p[:, o:o + LANES]
    if not latent:
        seq = ckv_ref.shape[1]
        for b in range(ckv_ref.shape[0]):
            ckv_ref[b] = ckvn[b * seq:(b + 1) * seq]
            kr_ref[b] = kr[b * seq:(b + 1) * seq, :ROPE_DIM]
    else:
        kr = kr * cos[:, :LANES] + p[:, o + LANES:o + 2 * LANES] * sin[:, :LANES]
    kr = kr.astype(BF16)
    kv = _dot(ckvn.astype(BF16), wkv_ref[...])
    _store_keys_values(kv, kr, kcat_ref, vext_ref)


def _mixer_in(stream, layer, x, mod, lw, rope, tm, caches=None):
    latent = stream.latent
    t = stream.tokens
    row = lambda w: pl.BlockSpec((tm, w), lambda i: (i, 0))
    win = lw["w_in"]
    win_cols = win.shape[1] if latent else win.shape[1] - LANES
    wq = lw["wq"]
    wq_cols = wq.shape[1] if latent else wq.shape[1] - N_HEADS * ROPE_DIM
    ins = [x, mod, mod, win, lw["q_g"], lw["kv_g"], wq]
    specs = [row(D_MODEL), stream.mod_spec(layer, SC_A, tm), stream.mod_spec(layer, SH_A, tm),
             _const_spec((D_MODEL, win_cols)), _const_spec(lw["q_g"].shape), _const_spec(lw["kv_g"].shape),
             _const_spec((Q_RANK, wq_cols))]
    for name in ("wkv", "ln_v_g", "w_s", "b_s"):
        ins.append(lw[name])
        specs.append(_const_spec(lw[name].shape))
    if latent:
        tiles_per_seq = stream.seq // tm
        pos = pl.BlockSpec((tm, N_HEADS * ROPE_DIM), lambda i: (i % tiles_per_seq, 0))
        ins += [rope[0], rope[1]]
        specs += [pos, pos]
    out_shape = [jax.ShapeDtypeStruct((t, CHUNK_W), BF16), jax.ShapeDtypeStruct((t, QK_W), BF16),
                 jax.ShapeDtypeStruct((t, QK_W), BF16), jax.ShapeDtypeStruct((t, VEXT_W), BF16)]
    out_specs = [row(CHUNK_W), row(QK_W), row(QK_W), row(VEXT_W)]
    aliases = {}
    if not latent:
        for rank in (KV_RANK, ROPE_DIM):
            out_shape.append(jax.ShapeDtypeStruct((stream.batch, DEPTH, stream.seq, rank), F32))
            out_specs.append(pl.BlockSpec((tm // stream.seq, None, stream.seq, rank),
                                          lambda i: (i, layer, 0, 0)))
        for k, buf in enumerate(caches or ()):
            aliases[len(ins)] = len(out_shape) - 2 + k
            ins.append(buf)
            specs.append(pl.BlockSpec(memory_space=pl.ANY))
    return pl.pallas_call(
        functools.partial(_mixer_in_body, latent, tm, len(aliases)),
        grid=(t // tm,),
        in_specs=specs,
        out_specs=out_specs,
        out_shape=out_shape,
        input_output_aliases=aliases,
        compiler_params=_params("parallel"),
        name="mixer_in",
    )(*ins)


KEY_CHUNK = 256


def _attend_body(has_ctx, *refs):
    if has_ctx:
        q_ref, k_ref, vext_ref, cckv_ref, ckr_ref, wkv_ref, o_ref, kctx_ref, vctx_ref = refs

        @pl.when(pl.program_id(1) == 0)
        def _():
            kv = _dot(cckv_ref[...].astype(BF16), wkv_ref[...])
            _store_keys_values(kv, ckr_ref[...].astype(BF16), kctx_ref, vctx_ref)

        sources = [(kctx_ref, vctx_ref), (k_ref, vext_ref)]
    else:
        q_ref, k_ref, vext_ref, o_ref = refs
        sources = [(k_ref, vext_ref)]
    chunks = [(kr, vr, slice(c * KEY_CHUNK, (c + 1) * KEY_CHUNK))
              for kr, vr in sources for c in range(kr.shape[0] // KEY_CHUNK)]

    def scores(hd):
        qk = slice(hd * HEAD_PAD, (hd + 1) * HEAD_PAD)
        return [_dot_nt(q_ref[:, qk], kr[rows, qk]) for kr, _, rows in chunks]

    s = scores(0)
    for hd in range(N_HEADS):
        m = jnp.max(functools.reduce(jnp.maximum, s), axis=-1, keepdims=True)
        s_next, acc = [], None
        for c, (kr, vr, rows) in enumerate(chunks):
            if hd + 1 < N_HEADS:
                qk = slice((hd + 1) * HEAD_PAD, (hd + 2) * HEAD_PAD)
                s_next.append(_dot_nt(q_ref[:, qk], kr[rows, qk]))
            p = jnp.exp2(s[c] - m).astype(BF16)
            part = _dot(p, vr[rows, hd * V_EXT:(hd + 1) * V_EXT])
            acc = part if acc is None else acc + part
        s = s_next
        o_ref[:, hd * V_DIM:(hd + 1) * V_DIM] = (acc[:, :V_DIM] / acc[:, V_DIM:]).astype(BF16)


def _attend(stream, layer, q, kcat, vext, lw, ctx, tq):
    n = stream.seq
    nq = n // tq
    has_ctx = stream.latent
    ins = [q, kcat, vext]
    specs = [pl.BlockSpec((tq, QK_W), lambda b, i: (b * nq + i, 0)),
             pl.BlockSpec((n, QK_W), lambda b, i: (b, 0)),
             pl.BlockSpec((n, VEXT_W), lambda b, i: (b, 0))]
    scratch = []
    if has_ctx:
        cache_ckv, cache_kr = ctx
        past = cache_ckv.shape[2]
        ins += [cache_ckv, cache_kr, lw["wkv"]]
        specs += [pl.BlockSpec((None, None, past, KV_RANK), lambda b, i: (b, layer, 0, 0)),
                  pl.BlockSpec((None, None, past, LANES), lambda b, i: (b, layer, 0, 0)),
                  _const_spec(lw["wkv"].shape)]
        scratch = [pltpu.VMEM((past, QK_W), BF16), pltpu.VMEM((past, VEXT_W), BF16)]
    return pl.pallas_call(
        functools.partial(_attend_body, has_ctx),
        grid=(stream.batch, nq),
        in_specs=specs,
        out_specs=pl.BlockSpec((tq, MLA_W), lambda b, i: (b * nq + i, 0)),
        out_shape=jax.ShapeDtypeStruct((stream.tokens, MLA_W), BF16),
        scratch_shapes=scratch,
        compiler_params=_params("parallel", "arbitrary"),
        name="attend",
    )(*ins)


MIX_PARTS = 4


def _mixer_residual(co_ref, mo_ref, wo1_ref, wo2_ref, x_ref, ga_ref, g_ref, b_ref):
    rp = x_ref.shape[0] // MIX_PARTS
    parts = [slice(k * rp, (k + 1) * rp) for k in range(MIX_PARTS)]
    mix = [_dot(co_ref[r], wo1_ref[...]) + _dot(mo_ref[r], wo2_ref[...]) for r in parts]
    x1 = [_layernorm(ALPHA * x_ref[r] + ga_ref[...] * mix[k], g_ref[...], b_ref[...])
          for k, r in enumerate(parts)]
    return parts, x1


def _mix_out_body(co_ref, mo_ref, wo1_ref, wo2_ref, x_ref, ga_ref, scf_ref, shf_ref, g_ref, b_ref,
                  rcat_ref, x1_ref, h2_ref, meta_ref, cnt_ref):
    parts, x1 = _mixer_residual(co_ref, mo_ref, wo1_ref, wo2_ref, x_ref, ga_ref, g_ref, b_ref)
    rp = x_ref.shape[0] // MIX_PARTS
    for k, r in enumerate(parts):
        x1_ref[r] = x1[k]
    h2 = [v * (1.0 + scf_ref[...]) + shf_ref[...] for v in x1]
    hh = [v.astype(BF16) for v in h2]
    for k, r in enumerate(parts):
        h2_ref[r] = hh[k]
    hl = [(h2[k] - hh[k].astype(F32)).astype(BF16) for k in range(MIX_PARTS)]
    ra = [_dot(v, rcat_ref[...]) for v in hh]
    rb = [_dot(v, rcat_ref[...]) for v in hl]
    logits = [ra[k][:, :LANES] + (ra[k][:, LANES:] + (rb[k][:, :LANES] + rb[k][:, LANES:]))
              for k in range(MIX_PARTS)]
    lane = lax.broadcasted_iota(jnp.int32, (rp, LANES), 1).astype(F32)
    neg = -jnp.inf
    lg = [jnp.where(lane < N_EXPERTS, v, neg) for v in logits]
    m1 = [jnp.max(v, axis=-1, keepdims=True) for v in lg]
    i1 = [jnp.min(jnp.where(lg[k] == m1[k], lane, float(LANES)), axis=-1, keepdims=True)
          for k in range(MIX_PARTS)]
    lg2 = [jnp.where(lane == i1[k], neg, lg[k]) for k in range(MIX_PARTS)]
    m2 = [jnp.max(v, axis=-1, keepdims=True) for v in lg2]
    i2 = [jnp.min(jnp.where(lg2[k] == m2[k], lane, float(LANES)), axis=-1, keepdims=True)
          for k in range(MIX_PARTS)]
    picked = [jnp.where(lane == i1[k], 1.0, jnp.where(lane == i2[k], 1.0, 0.0)) for k in range(MIX_PARTS)]
    earlier = jnp.where(lax.broadcasted_iota(jnp.int32, (rp, rp), 0)
                        > lax.broadcasted_iota(jnp.int32, (rp, rp), 1), 1.0, 0.0).astype(BF16)
    before = jnp.zeros((1, LANES), F32)
    for k, r in enumerate(parts):
        rank = _dot(earlier, picked[k].astype(BF16)) + before
        before = before + jnp.sum(picked[k], axis=0, keepdims=True)
        rank1 = jnp.sum(jnp.where(lane == i1[k], rank, 0.0), axis=-1, keepdims=True)
        rank2 = jnp.sum(jnp.where(lane == i2[k], rank, 0.0), axis=-1, keepdims=True)
        e2 = jnp.exp(m2[k] - m1[k])
        den = 1.0 + e2
        meta = jnp.zeros((rp, LANES), F32)
        for j, val in enumerate((i1[k], i2[k], 1.0 / den, e2 / den, rank1, rank2)):
            meta = jnp.where(lane == float(j), val, meta)
        meta_ref[r] = meta
    cnt_ref[...] = before


def _mix_out(stream, layer, co, mo, x, mod, lw, tm):
    t = stream.tokens
    row = lambda w: pl.BlockSpec((tm, w), lambda i: (i, 0))
    vec = _const_spec((1, D_MODEL))
    return pl.pallas_call(
        _mix_out_body,
        grid=(t // tm,),
        in_specs=[row(CHUNK_W), row(MLA_W), _const_spec(lw["wo_chunk"].shape), _const_spec(lw["wo_mla"].shape),
                  row(D_MODEL), stream.mod_spec(layer, G_A, tm), stream.mod_spec(layer, SC_F, tm),
                  stream.mod_spec(layer, SH_F, tm), vec, vec, _const_spec(lw["router"].shape)],
        out_specs=[row(D_MODEL), row(D_MODEL), row(LANES), pl.BlockSpec((None, 1, LANES), lambda i: (i, 0, 0))],
        out_shape=[jax.ShapeDtypeStruct((t, D_MODEL), F32), jax.ShapeDtypeStruct((t, D_MODEL), BF16),
                   jax.ShapeDtypeStruct((t, LANES), F32), jax.ShapeDtypeStruct((t // tm, 1, LANES), F32)],
        compiler_params=_params("parallel"),
        name="mix_out",
    )(co, mo, lw["wo_chunk"], lw["wo_mla"], x, mod, mod, mod, lw["ln_mix_g"], lw["ln_mix_b"], lw["router"])


def _mix_ffn_body(co_ref, mo_ref, wo1_ref, wo2_ref, x_ref, ga_ref, scf_ref, shf_ref, gf_ref,
                  gm_ref, bm_ref, g_ref, b_ref, wg_ref, wu_ref, wd_ref, o_ref):
    _, x1 = _mixer_residual(co_ref, mo_ref, wo1_ref, wo2_ref, x_ref, ga_ref, gm_ref, bm_ref)
    x1 = jnp.concatenate(x1, axis=0)
    h = (x1 * (1.0 + scf_ref[...]) + shf_ref[...]).astype(BF16)
    gate = _dot(h, wg_ref[...])
    up = _dot(h, wu_ref[...])
    a = (gate * _sigmoid(gate) * up).astype(BF16)
    f = _dot(a, wd_ref[...])
    o_ref[...] = _layernorm(ALPHA * x1 + gf_ref[...] * f, g_ref[...], b_ref[...])


def _mix_ffn(stream, layer, co, mo, x, mod, lw, tm):
    t = stream.tokens
    row = lambda w: pl.BlockSpec((tm, w), lambda i: (i, 0))
    vec = _const_spec((1, D_MODEL))
    resident = lambda a: pl.BlockSpec(a.shape, lambda i: (0, 0), pipeline_mode=pl.Buffered(1))
    mods = [stream.mod_spec(layer, which, tm) for which in (G_A, SC_F, SH_F, G_F)]
    return pl.pallas_call(
        _mix_ffn_body,
        grid=(t // tm,),
        in_specs=[row(CHUNK_W), row(MLA_W), resident(lw["wo_chunk"]), resident(lw["wo_mla"]), row(D_MODEL),
                  *mods, vec, vec, vec, vec,
                  resident(lw["ffn_gate"]), resident(lw["ffn_up"]), resident(lw["ffn_down"])],
        out_specs=row(D_MODEL),
        out_shape=jax.ShapeDtypeStruct((t, D_MODEL), F32),
        compiler_params=_params("parallel"),
        name="mix_ffn",
    )(co, mo, lw["wo_chunk"], lw["wo_mla"], x, mod, mod, mod, mod,
      lw["ln_mix_g"], lw["ln_mix_b"], lw["ln_ffn_g"], lw["ln_ffn_b"],
      lw["ffn_gate"], lw["ffn_up"], lw["ffn_down"])


RUN_ALIGN = 16
ROUTE_TILE = 512
EXPERT_TILE = 512


def _plan_body(cap, tile, cnt_ref, off_ref, run_ref, base_ref, te_ref, tb_ref, rows_ref, nv_ref):
    nw = cnt_ref.shape[0]
    lane = lax.broadcasted_iota(jnp.int32, (1, LANES), 1).astype(F32)
    run = jnp.floor((cnt_ref[...] + (RUN_ALIGN - 1)) * (1.0 / RUN_ALIGN)) * RUN_ALIGN
    before = jnp.where(lax.broadcasted_iota(jnp.int32, (LANES, LANES), 0)
                       < lax.broadcasted_iota(jnp.int32, (LANES, LANES), 1), 1.0, 0.0).astype(BF16)
    upto = jnp.where(lax.broadcasted_iota(jnp.int32, (LANES, LANES), 0)
                     <= lax.broadcasted_iota(jnp.int32, (LANES, LANES), 1), 1.0, 0.0).astype(BF16)
    above = jnp.where(lax.broadcasted_iota(jnp.int32, (nw, nw), 0)
                      > lax.broadcasted_iota(jnp.int32, (nw, nw), 1), 1.0, 0.0).astype(BF16)
    run_b = run.astype(BF16)
    off_ref[...] = _dot(run_b, before).astype(jnp.int32)
    run_ref[...] = run.astype(jnp.int32)
    base_ref[...] = (lane * cap + _dot(above, run_b)).astype(jnp.int32)
    totals = jnp.sum(run, axis=0, keepdims=True)
    tiles = jnp.floor((totals + (tile - 1)) * (1.0 / tile))
    cum = _dot(tiles.astype(BF16), upto)
    start = cum - tiles
    n_valid = jnp.max(cum, axis=-1, keepdims=True)
    step = jnp.minimum(lane, n_valid - 1.0)
    pick = lambda v, e: jnp.sum(jnp.where(lane == float(e), v, 0.0), axis=-1, keepdims=True)
    te = jnp.zeros_like(step)
    for e in range(N_EXPERTS):
        te = te + jnp.where(pick(cum, e) <= step, 1.0, 0.0)
    start_te, total_te = jnp.zeros_like(step), jnp.zeros_like(step)
    for e in range(N_EXPERTS):
        start_te = jnp.where(te == float(e), pick(start, e), start_te)
        total_te = jnp.where(te == float(e), pick(totals, e), total_te)
    tj = step - start_te
    te_ref[...] = te.astype(jnp.int32)
    tb_ref[...] = (te * (cap // tile) + tj).astype(jnp.int32)
    rows_ref[...] = jnp.minimum(float(tile), total_te - tj * tile).astype(jnp.int32)
    nv_ref[...] = jnp.broadcast_to(n_valid, (1, LANES)).astype(jnp.int32)


def _route_plan(counts, tokens, tm, tile):
    nw = tokens // tm
    cnt = jnp.concatenate([c.reshape(-1, LANES) for c in counts], axis=0)
    worst_pad = (RUN_ALIGN - 1) * nw
    cap = pl.cdiv(tokens + worst_pad, tile) * tile
    n_tiles = (2 * tokens + N_EXPERTS * worst_pad) // tile + N_EXPERTS
    assert n_tiles <= LANES and tm % RUN_ALIGN == 0
    per_run = jax.ShapeDtypeStruct((nw, LANES), jnp.int32)
    per_step = jax.ShapeDtypeStruct((1, LANES), jnp.int32)
    off, run, base, te, tb, rows, nv = pl.pallas_call(
        functools.partial(_plan_body, cap, tile),
        out_shape=[per_run, per_run, per_run, per_step, per_step, per_step, per_step],
        name="moe_plan",
    )(cnt)
    return dict(
        tm=tm, cap=cap, tile=tile, n_tiles=n_tiles, buf_rows=2 * tm + N_EXPERTS * RUN_ALIGN,
        off=off.reshape(-1), run=run.reshape(-1), base=base.reshape(-1),
        tile_expert=te.reshape(-1), tile_block=tb.reshape(-1), tile_rows=rows.reshape(-1),
        n_valid=nv.reshape(-1))


def _run_piece_sizes(tm):
    sizes, b = [], RUN_ALIGN
    while b <= tm:
        sizes.append(b)
        b *= 2
    return sizes


def _row_run_copies(src, dst, src_off, dst_off, n_rows, sem, sizes, action):
    for b in sizes:
        done = n_rows & (-2 * b)
        cp = pltpu.make_async_copy(
            src.at[pl.ds(pl.multiple_of(src_off + done, RUN_ALIGN), b)],
            dst.at[pl.ds(pl.multiple_of(dst_off + done, RUN_ALIGN), b)], sem)

        @pl.when((n_rows & b) != 0)
        def _():
            action(cp)


def _local_rows(meta, off_ref, w):
    i1, i2 = meta[:, 0:1], meta[:, 1:2]
    o1, o2 = jnp.zeros_like(i1), jnp.zeros_like(i2)
    for e in range(N_EXPERTS):
        off_e = off_ref[w * LANES + e].astype(F32)
        o1 = jnp.where(i1 == float(e), off_e, o1)
        o2 = jnp.where(i2 == float(e), off_e, o2)
    return (o1 + meta[:, 4:5]).astype(jnp.int32), (o2 + meta[:, 5:6]).astype(jnp.int32)


def _start(cp):
    cp.start()


def _wait(cp):
    cp.wait()


def _dispatch_body(tm, nw, tile0, off_ref, run_ref, base_ref, h_ref, meta_ref, *refs):
    xs_ref, xw_ref, sems = refs[-3:]
    w = pl.program_id(0)
    slot = w % 2
    sizes = _run_piece_sizes(tm)

    def push(win, buf, action):
        for e in range(N_EXPERTS):
            k = (tile0 + win) * LANES + e
            _row_run_copies(xw_ref.at[buf], xs_ref, off_ref[k], base_ref[k], run_ref[k],
                            sems.at[buf], sizes, action)

    r1, r2 = _local_rows(meta_ref[...], off_ref, tile0 + w)
    col = lax.broadcasted_iota(jnp.int32, (tm, xw_ref.shape[1]), 1)
    sel = jnp.where(col == r1, 1.0, jnp.where(col == r2, 1.0, 0.0)).astype(BF16)
    xw = lax.dot_general(sel, h_ref[...], (((0,), (0,)), ((), ())), preferred_element_type=F32)
    xw_ref[slot] = xw.astype(BF16)
    push(w, slot, _start)

    @pl.when(w > 0)
    def _():
        push(w - 1, 1 - slot, _wait)

    @pl.when(w == nw - 1)
    def _():
        push(w, slot, _wait)


def _dispatch(h2, meta, plan, tile0, xs=None):
    tm = plan["tm"]
    nw = h2.shape[0] // tm
    ins = [plan["off"], plan["run"], plan["base"], h2, meta]
    specs = [pl.BlockSpec((tm, D_MODEL), lambda i, *_: (i, 0)),
             pl.BlockSpec((tm, LANES), lambda i, *_: (i, 0))]
    aliases = {}
    if xs is not None:
        aliases = {len(ins): 0}
        ins.append(xs)
        specs.append(pl.BlockSpec(memory_space=pl.ANY))
    return pl.pallas_call(
        functools.partial(_dispatch_body, tm, nw, tile0),
        grid_spec=pltpu.PrefetchScalarGridSpec(
            num_scalar_prefetch=3,
            grid=(nw,),
            in_specs=specs,
            out_specs=pl.BlockSpec(memory_space=pl.ANY),
            scratch_shapes=[pltpu.VMEM((2, plan["buf_rows"], D_MODEL), BF16),
                            pltpu.SemaphoreType.DMA((2,))]),
        out_shape=jax.ShapeDtypeStruct((N_EXPERTS * plan["cap"], D_MODEL), BF16),
        input_output_aliases=aliases,
        compiler_params=_params("arbitrary"),
        name="moe_dispatch",
    )(*ins)


def _experts_body(te_ref, tb_ref, rows_ref, nv_ref, x_ref, wg_ref, wu_ref, wd_ref, o_ref):
    i = pl.program_id(0)
    rows = rows_ref[i]
    valid = i < nv_ref[0]
    half = x_ref.shape[0] // 2

    def swiglu(n):
        x = x_ref[0:n, :]
        live = lax.broadcasted_iota(jnp.int32, (n, 1), 0) < rows
        x = jnp.where(live, x, jnp.zeros_like(x))
        gate = _dot(x, wg_ref[...])
        up = _dot(x, wu_ref[...])
        a = (gate * _sigmoid(gate) * up).astype(BF16)
        o_ref[0:n, :] = _dot(a, wd_ref[...]).astype(BF16)

    @pl.when(jnp.logical_and(valid, rows > half))
    def _():
        swiglu(2 * half)

    @pl.when(jnp.logical_and(valid, rows <= half))
    def _():
        swiglu(half)


def _experts(xs, plan, lw):
    dff = lw["moe_gate"].shape[2]
    rows = pl.BlockSpec((plan["tile"], D_MODEL), lambda i, te, tb, *_: (tb[i], 0))
    return pl.pallas_call(
        _experts_body,
        grid_spec=pltpu.PrefetchScalarGridSpec(
            num_scalar_prefetch=4,
            grid=(plan["n_tiles"],),
            in_specs=[rows,
                      pl.BlockSpec((None, D_MODEL, dff), lambda i, te, *_: (te[i], 0, 0)),
                      pl.BlockSpec((None, D_MODEL, dff), lambda i, te, *_: (te[i], 0, 0)),
                      pl.BlockSpec((None, dff, D_MODEL), lambda i, te, *_: (te[i], 0, 0))],
            out_specs=rows),
        out_shape=jax.ShapeDtypeStruct(xs.shape, BF16),
        compiler_params=_params("arbitrary"),
        name="moe_experts",
    )(plan["tile_expert"], plan["tile_block"], plan["tile_rows"], plan["n_valid"],
      xs, lw["moe_gate"], lw["moe_up"], lw["moe_down"])


def _combine_body(tm, nw, tile0, off_ref, run_ref, base_ref, ys_ref, meta_ref, x_ref, gf_ref, g_ref,
                  b_ref, o_ref, yw_ref, sems):
    w = pl.program_id(0)
    slot = w % 2
    sizes = _run_piece_sizes(tm)

    def fetch(win, buf, action):
        for e in range(N_EXPERTS):
            k = (tile0 + win) * LANES + e
            _row_run_copies(ys_ref, yw_ref.at[buf], base_ref[k], off_ref[k], run_ref[k],
                            sems.at[buf], sizes, action)

    @pl.when(w == 0)
    def _():
        yw_ref[...] = jnp.zeros_like(yw_ref)
        fetch(0, 0, _start)

    @pl.when(w + 1 < nw)
    def _():
        fetch(w + 1, 1 - slot, _start)

    fetch(w, slot, _wait)
    rp = tm // MIX_PARTS
    parts = [slice(k * rp, (k + 1) * rp) for k in range(MIX_PARTS)]
    meta = [meta_ref[r] for r in parts]
    picks = [_local_rows(v, off_ref, tile0 + w) for v in meta]
    col = lax.broadcasted_iota(jnp.int32, (rp, yw_ref.shape[1]), 1)
    yw = yw_ref[slot]
    gate = [jnp.where(col == r1, v[:, 2:3], jnp.where(col == r2, v[:, 3:4], 0.0)).astype(BF16)
            for v, (r1, r2) in zip(meta, picks)]
    f = [_dot(g, yw) for g in gate]
    for k, r in enumerate(parts):
        o_ref[r] = _layernorm(ALPHA * x_ref[r] + gf_ref[...] * f[k], g_ref[...], b_ref[...])


def _combine(stream, layer, ys, meta, x1, mod, plan, lw, tile0):
    tm = plan["tm"]
    nw = stream.tokens // tm
    row = lambda width: pl.BlockSpec((tm, width), lambda i, *_: (i, 0))
    vec = pl.BlockSpec((1, D_MODEL), lambda i, *_: (0, 0))
    mod_spec = stream.mod_spec(layer, G_F, tm)
    mod_spec = pl.BlockSpec(mod_spec.block_shape, lambda i, *_, f=mod_spec.index_map: f(i))
    return pl.pallas_call(
        functools.partial(_combine_body, tm, nw, tile0),
        grid_spec=pltpu.PrefetchScalarGridSpec(
            num_scalar_prefetch=3,
            grid=(nw,),
            in_specs=[pl.BlockSpec(memory_space=pl.ANY), row(LANES), row(D_MODEL), mod_spec, vec, vec],
            out_specs=row(D_MODEL),
            scratch_shapes=[pltpu.VMEM((2, plan["buf_rows"], D_MODEL), BF16),
                            pltpu.SemaphoreType.DMA((2,))]),
        out_shape=jax.ShapeDtypeStruct((stream.tokens, D_MODEL), F32),
        compiler_params=_params("arbitrary"),
        name="moe_combine",
    )(plan["off"], plan["run"], plan["base"], ys, meta, x1, mod, lw["ln_ffn_g"], lw["ln_ffn_b"])


def _rope_tables(n_tokens):
    rows = n_tokens // GRID_W
    row = np.repeat(np.arange(rows, dtype=np.float64), GRID_W)
    col = np.tile(np.arange(GRID_W, dtype=np.float64), rows)
    half = ROPE_DIM // 2
    inv_freq = ROPE_BASE ** (-np.arange(0, half, 2, dtype=np.float64) / half)
    ar = row[:, None] * inv_freq[None, :]
    ac = col[:, None] * inv_freq[None, :]
    cos = np.concatenate([np.cos(ar), np.cos(ar), np.cos(ac), np.cos(ac)], axis=-1)
    sin = np.concatenate([-np.sin(ar), np.sin(ar), -np.sin(ac), np.sin(ac)], axis=-1)
    return jnp.asarray(np.tile(cos, (1, N_HEADS)), F32), jnp.asarray(np.tile(sin, (1, N_HEADS)), F32)


def _swap_rope_halves(a):
    q = ROPE_DIM // 4
    return jnp.concatenate([a[..., q:2 * q], a[..., :q], a[..., 3 * q:], a[..., 2 * q:3 * q]], axis=-1)


def _layer_weights(l, p):
    w_in = p["w_in"][l].astype(BF16)
    o_kr = 2 * CHUNK_W + Q_RANK + KV_RANK
    zpad = jnp.zeros((D_MODEL, LANES - ROPE_DIM), BF16)
    w_in_ext = jnp.concatenate(
        [w_in, zpad, _swap_rope_halves(w_in[:, o_kr:o_kr + ROPE_DIM]), zpad], axis=1)

    w_uq = p["w_uq"][l].astype(BF16).reshape(Q_RANK, N_HEADS, QK_NOPE + ROPE_DIM)
    wq_rope = w_uq[:, :, QK_NOPE:]
    wq = jnp.concatenate([w_uq[:, :, :QK_NOPE].reshape(Q_RANK, N_HEADS * QK_NOPE),
                          wq_rope.reshape(Q_RANK, N_HEADS * ROPE_DIM),
                          _swap_rope_halves(wq_rope).reshape(Q_RANK, N_HEADS * ROPE_DIM)], axis=1)

    w_ukv = p["w_ukv"][l].astype(BF16).reshape(KV_RANK, N_HEADS, QK_NOPE + V_DIM)
    wkv = jnp.concatenate([w_ukv[:, :, :QK_NOPE].reshape(KV_RANK, N_HEADS * QK_NOPE),
                           w_ukv[:, :, QK_NOPE:].reshape(KV_RANK, MLA_W)], axis=1)
    w_out = p["w_out"][l]
    lw = {
        "w_in": w_in_ext,
        "q_g": p["q_norm_g"][l].reshape(1, Q_RANK),
        "kv_g": p["kv_norm_g"][l].reshape(1, KV_RANK),
        "wq": wq,
        "wkv": wkv,
        "ln_v_g": p["chunk_ln_g"][l].reshape(1, CHUNK_W),
        "w_s": p["w_spatial"][l].astype(BF16),
        "b_s": jnp.broadcast_to(p["b_spatial"][l][:, :, None], (N_GROUPS, CHUNK, GROUP_W)),
        "wo_chunk": w_out[:CHUNK_W].astype(BF16),
        "wo_mla": w_out[CHUNK_W:].astype(BF16),
        "ln_mix_g": p["ln_mix_g"][l].reshape(1, D_MODEL),
        "ln_mix_b": p["ln_mix_b"][l].reshape(1, D_MODEL),
        "ln_ffn_g": p["ln_ffn_g"][l].reshape(1, D_MODEL),
        "ln_ffn_b": p["ln_ffn_b"][l].reshape(1, D_MODEL),
    }
    if l % 2 == 0:
        lw["ffn_gate"] = p["ffn_w_gate"][l // 2].astype(BF16)
        lw["ffn_up"] = p["ffn_w_up"][l // 2].astype(BF16)
        lw["ffn_down"] = p["ffn_w_down"][l // 2].astype(BF16)
    else:
        r = jnp.pad(p["router_w"][l // 2], ((0, 0), (0, LANES - N_EXPERTS)))
        r_hi = r.astype(BF16)
        lw["router"] = jnp.concatenate([r_hi, (r - r_hi.astype(F32)).astype(BF16)], axis=1)
        lw["moe_gate"] = p["moe_w_gate"][l // 2].astype(BF16)
        lw["moe_up"] = p["moe_w_up"][l // 2].astype(BF16)
        lw["moe_down"] = p["moe_w_down"][l // 2].astype(BF16)
    return lw


def _trunk(streams, xs, mod, weights, rope, ctx):
    caches = None
    for l in range(DEPTH):
        lw = weights[l]
        mixed = []
        for stream, x in zip(streams, xs):
            outs = _mixer_in(stream, l, x, mod, lw, rope, stream.tiles["mixer_in"],
                             None if stream.latent else caches)
            co, q, kcat, vext = outs[:4]
            if not stream.latent:
                caches = outs[4:]
            mixed.append((co, _attend(stream, l, q, kcat, vext, lw, ctx, stream.tiles["attend"])))
        if l % 2 == 0:
            xs = [_mix_ffn(stream, l, co, mo, x, mod, lw, stream.tiles["ffn"])
                  for stream, x, (co, mo) in zip(streams, xs, mixed)]
            continue
        routed = [_mix_out(stream, l, co, mo, x, mod, lw, ROUTE_TILE)
                  for stream, x, (co, mo) in zip(streams, xs, mixed)]
        plan = _route_plan([r[3] for r in routed], sum(s.tokens for s in streams), ROUTE_TILE, EXPERT_TILE)
        tile0, sorted_rows = [], None
        for stream, (_, h2, meta, _) in zip(streams, routed):
            tile0.append(sum(s.tokens for s in streams[:len(tile0)]) // ROUTE_TILE)
            sorted_rows = _dispatch(h2, meta, plan, tile0[-1], sorted_rows)
        ys = _experts(sorted_rows, plan, lw)
        xs = [_combine(stream, l, ys, meta, x1, mod, plan, lw, t0)
              for stream, (x1, _, meta, _), t0 in zip(streams, routed, tile0)]
    return xs, caches


def kernel(x_prompt, x_sample, c, cache_ckv, cache_krope, c_ctx, w_mod, b_mod, w_in, q_norm_g, kv_norm_g, w_uq, w_ukv, chunk_ln_g, w_spatial, b_spatial, w_out, ln_mix_g, ln_mix_b, ln_ffn_g, ln_ffn_b, ffn_w_gate, ffn_w_up, ffn_w_down, router_w, moe_w_gate, moe_w_up, moe_w_down):
    p = dict(w_in=w_in, q_norm_g=q_norm_g, kv_norm_g=kv_norm_g, w_uq=w_uq, w_ukv=w_ukv,
             chunk_ln_g=chunk_ln_g, w_spatial=w_spatial, b_spatial=b_spatial, w_out=w_out,
             ln_mix_g=ln_mix_g, ln_mix_b=ln_mix_b, ln_ffn_g=ln_ffn_g, ln_ffn_b=ln_ffn_b,
             ffn_w_gate=ffn_w_gate, ffn_w_up=ffn_w_up, ffn_w_down=ffn_w_down, router_w=router_w,
             moe_w_gate=moe_w_gate, moe_w_up=moe_w_up, moe_w_down=moe_w_down)
    weights = [_layer_weights(l, p) for l in range(DEPTH)]

    batch, seq, _ = x_prompt.shape
    dec_batch, dec_seq, _ = x_sample.shape
    cond_rows = jnp.concatenate(
        [c_ctx[None, :], c, jnp.zeros((MOD_ROWS - 1 - dec_batch, D_MODEL), F32)], axis=0)
    mod = _modulation(cond_rows, w_mod, b_mod).reshape(DEPTH, MOD_ROWS, 6, 1, D_MODEL)

    prompt = _Stream(batch, seq, mod_row0=0, per_row_mod=False, latent=False,
                     tiles=dict(mixer_in=512, attend=256, ffn=512))
    sample = _Stream(dec_batch, dec_seq, mod_row0=1, per_row_mod=True, latent=True,
                     tiles=dict(mixer_in=1024, attend=1024, ffn=512))
    rope = _rope_tables(dec_seq)
    ctx = (cache_ckv, jnp.pad(cache_krope, ((0, 0), (0, 0), (0, 0), (0, LANES - ROPE_DIM))))
    (y_prompt, y_sample), (new_ckv, new_krope) = _trunk(
        [prompt, sample],
        [x_prompt.reshape(batch * seq, D_MODEL), x_sample.reshape(dec_batch * dec_seq, D_MODEL)],
        mod, weights, rope, ctx)
    return (y_prompt.reshape(batch, seq, D_MODEL), y_sample.reshape(dec_batch, dec_seq, D_MODEL),
            new_ckv, new_krope)
```

```python
import functools
import math

import jax
import jax.numpy as jnp
import numpy as np
from jax import lax
from jax.experimental import pallas as pl
from jax.experimental.pallas import tpu as pltpu

F32 = jnp.float32
BF16 = jnp.bfloat16

D_MODEL = 1024
DEPTH = 2
GRID_W = 64
CHUNK = 128
N_GROUPS = 4
GROUP_W = 128
CHUNK_W = N_GROUPS * GROUP_W
N_HEADS = 4
QK_NOPE = 128
ROPE_DIM = 64
V_DIM = 128
Q_RANK = 384
KV_RANK = 256
MLA_W = N_HEADS * V_DIM
HEAD_PAD = 256
QK_W = N_HEADS * HEAD_PAD
V_EXT = 2 * V_DIM
VEXT_W = N_HEADS * V_EXT
ROPE_BASE = 10000.0
N_EXPERTS = 8
ALPHA = (2 * DEPTH) ** 0.25
EPS = 1e-6
ATTN_SCALE = math.log2(math.e) / math.sqrt(QK_NOPE + ROPE_DIM)
MOD_ROWS = 16
LANES = 128
V7X_VMEM_BYTES = 64 * 1024 * 1024
VMEM_LIMIT = V7X_VMEM_BYTES * 7 // 8

SH_A, SC_A, G_A, SH_F, SC_F, G_F = range(6)


def _sigmoid(x):
    return 1.0 / (1.0 + jnp.exp(-x))


def _gelu_tanh(x):
    return 0.5 * x * (1.0 + jnp.tanh(math.sqrt(2.0 / math.pi) * (x + 0.044715 * (x * x * x))))


def _layernorm(y, g, b):
    mu = jnp.mean(y, axis=-1, keepdims=True)
    d = y - mu
    var = jnp.mean(d * d, axis=-1, keepdims=True)
    return d * lax.rsqrt(var + EPS) * g + b


def _rmsnorm(y, g):
    return y * lax.rsqrt(jnp.mean(y * y, axis=-1, keepdims=True) + EPS) * g


def _dot(a, b):
    return jnp.dot(a, b, preferred_element_type=F32)


def _dot_nt(a, b):
    return lax.dot_general(a, b, (((1,), (1,)), ((), ())), preferred_element_type=F32)


def _params(*sem):
    return pltpu.CompilerParams(dimension_semantics=sem, vmem_limit_bytes=VMEM_LIMIT)


def _const_spec(shape):
    nd = len(shape)
    return pl.BlockSpec(shape, lambda *_: (0,) * nd)


class _Stream:
    def __init__(self, batch, seq, mod_row0, per_row_mod, latent, tiles):
        self.tiles = tiles
        self.batch = batch
        self.seq = seq
        self.tokens = batch * seq
        self.mod_row0 = mod_row0
        self.per_row_mod = per_row_mod
        self.latent = latent

    def mod_spec(self, layer, which, tm):
        tiles_per_row = self.seq // tm
        row0, per_row = self.mod_row0, self.per_row_mod

        def index(i):
            r = row0 + (i // tiles_per_row if per_row else 0)
            return (layer, r, which, 0, 0)

        return pl.BlockSpec((None, None, None, 1, D_MODEL), index)


def _mod_body(c_ref, w_ref, b_ref, o_ref):
    a = c_ref[...]
    a = (a * _sigmoid(a)).astype(BF16)
    o_ref[:, 0, :] = _dot(a, w_ref[...].astype(BF16)) + b_ref[...]


def _modulation(cond_rows, w_mod, b_mod):
    depth, _, width = w_mod.shape
    n_roles = width // D_MODEL
    return pl.pallas_call(
        _mod_body,
        grid=(depth, n_roles),
        in_specs=[
            pl.BlockSpec((MOD_ROWS, D_MODEL), lambda l, j: (0, 0)),
            pl.BlockSpec((None, D_MODEL, D_MODEL), lambda l, j: (l, 0, j)),
            pl.BlockSpec((None, 1, D_MODEL), lambda l, j: (l, 0, j)),
        ],
        out_specs=pl.BlockSpec((None, MOD_ROWS, None, 1, D_MODEL), lambda l, j: (l, 0, j, 0, 0)),
        out_shape=jax.ShapeDtypeStruct((depth, MOD_ROWS, n_roles, 1, D_MODEL), F32),
        compiler_params=_params("parallel", "parallel"),
        name="modulation",
    )(cond_rows, w_mod, b_mod.reshape(depth, 1, width))


def _store_keys_values(kv, kr, kcat_ref, vext_ref):
    ones = jnp.ones((kv.shape[0], V_EXT - V_DIM), BF16)
    for hd in range(N_HEADS):
        a = hd * HEAD_PAD
        kcat_ref[:, a:a + QK_NOPE] = kv[:, hd * QK_NOPE:(hd + 1) * QK_NOPE].astype(BF16)
        kcat_ref[:, a + QK_NOPE:a + HEAD_PAD] = kr
        b = hd * V_EXT
        vext_ref[:, b:b + V_DIM] = kv[:, (N_HEADS + hd) * V_DIM:(N_HEADS + hd + 1) * V_DIM].astype(BF16)
        vext_ref[:, b + V_DIM:b + V_EXT] = ones


def _mixer_in_body(latent, tm, n_aliased, *refs):
    it = iter(refs)
    x_ref, sc_ref, sh_ref, win_ref, qg_ref, kvg_ref, wq_ref = (next(it) for _ in range(7))
    wkv_ref, lng_ref, ws_ref, bs_ref = (next(it) for _ in range(4))
    cos_ref, sin_ref = (next(it), next(it)) if latent else (None, None)
    for _ in range(n_aliased):
        next(it)
    co_ref, q_ref, kcat_ref, vext_ref = (next(it) for _ in range(4))
    ckv_ref, kr_ref = (None, None) if latent else (next(it), next(it))

    h = (x_ref[...] * (1.0 + sc_ref[...]) + sh_ref[...]).astype(BF16)
    p = _dot(h, win_ref[...])

    for g in range(N_GROUPS):
        cols = slice(g * GROUP_W, (g + 1) * GROUP_W)
        vg = _gelu_tanh(p[:, CHUNK_W + g * GROUP_W:CHUNK_W + (g + 1) * GROUP_W])
        mu = jnp.mean(vg, axis=-1, keepdims=True)
        d = vg - mu
        var = jnp.mean(d * d, axis=-1, keepdims=True)
        vn = (d * lax.rsqrt(var + EPS) * lng_ref[:, cols]).astype(BF16)
        ug = _gelu_tanh(p[:, cols])
        for c in range(tm // CHUNK):
            rows = slice(c * CHUNK, (c + 1) * CHUNK)
            z = _dot(ws_ref[g], vn[rows]) + bs_ref[g]
            co_ref[rows, cols] = (ug[rows] * z).astype(BF16)

    o = 2 * CHUNK_W
    cqn = _rmsnorm(p[:, o:o + Q_RANK], qg_ref[...]).astype(BF16)
    qc = _dot(cqn, wq_ref[...])
    q_rope = qc[:, N_HEADS * QK_NOPE:N_HEADS * (QK_NOPE + ROPE_DIM)]
    if latent:
        cos = cos_ref[...]
        sin = sin_ref[...]
        q_rope = q_rope * cos + qc[:, N_HEADS * (QK_NOPE + ROPE_DIM):] * sin
    for hd in range(N_HEADS):
        a = hd * HEAD_PAD
        q_ref[:, a:a + QK_NOPE] = (qc[:, hd * QK_NOPE:(hd + 1) * QK_NOPE] * ATTN_SCALE).astype(BF16)
        pair = q_rope[:, (hd // 2) * LANES:(hd // 2 + 1) * LANES]
        if hd % 2:
            pair = pltpu.roll(pair, ROPE_DIM, axis=1)
        q_ref[:, a + QK_NOPE:a + HEAD_PAD] = (pair * ATTN_SCALE).astype(BF16)

    o += Q_RANK
    ckvn = _rmsnorm(p[:, o:o + KV_RANK], kvg_ref[...])
    o += KV_RANK
    kr = p[:, o:o + LANES]
    if not latent:
        seq = ckv_ref.shape[1]
        for b in range(ckv_ref.shape[0]):
            ckv_ref[b] = ckvn[b * seq:(b + 1) * seq]
            kr_ref[b] = kr[b * seq:(b + 1) * seq, :ROPE_DIM]
    else:
        kr = kr * cos[:, :LANES] + p[:, o + LANES:o + 2 * LANES] * sin[:, :LANES]
    kr = kr.astype(BF16)
    kv = _dot(ckvn.astype(BF16), wkv_ref[...])
    _store_keys_values(kv, kr, kcat_ref, vext_ref)


def _mixer_in(stream, layer, x, mod, lw, rope, tm, caches=None):
    latent = stream.latent
    t = stream.tokens
    row = lambda w: pl.BlockSpec((tm, w), lambda i: (i, 0))
    win = lw["w_in"]
    win_cols = win.shape[1] if latent else win.shape[1] - LANES
    wq = lw["wq"]
    wq_cols = wq.shape[1] if latent else wq.shape[1] - N_HEADS * ROPE_DIM
    ins = [x, mod, mod, win, lw["q_g"], lw["kv_g"], wq]
    specs = [row(D_MODEL), stream.mod_spec(layer, SC_A, tm), stream.mod_spec(layer, SH_A, tm),
             _const_spec((D_MODEL, win_cols)), _const_spec(lw["q_g"].shape), _const_spec(lw["kv_g"].shape),
             _const_spec((Q_RANK, wq_cols))]
    for name in ("wkv", "ln_v_g", "w_s", "b_s"):
        ins.append(lw[name])
        specs.append(_const_spec(lw[name].shape))
    if latent:
        tiles_per_seq = stream.seq // tm
        pos = pl.BlockSpec((tm, N_HEADS * ROPE_DIM), lambda i: (i % tiles_per_seq, 0))
        ins += [rope[0], rope[1]]
        specs += [pos, pos]
    out_shape = [jax.ShapeDtypeStruct((t, CHUNK_W), BF16), jax.ShapeDtypeStruct((t, QK_W), BF16),
                 jax.ShapeDtypeStruct((t, QK_W), BF16), jax.ShapeDtypeStruct((t, VEXT_W), BF16)]
    out_specs = [row(CHUNK_W), row(QK_W), row(QK_W), row(VEXT_W)]
    aliases = {}
    if not latent:
        for rank in (KV_RANK, ROPE_DIM):
            out_shape.append(jax.ShapeDtypeStruct((stream.batch, DEPTH, stream.seq, rank), F32))
            out_specs.append(pl.BlockSpec((tm // stream.seq, None, stream.seq, rank),
                                          lambda i: (i, layer, 0, 0)))
        for k, buf in enumerate(caches or ()):
            aliases[len(ins)] = len(out_shape) - 2 + k
            ins.append(buf)
            specs.append(pl.BlockSpec(memory_space=pl.ANY))
    return pl.pallas_call(
        functools.partial(_mixer_in_body, latent, tm, len(aliases)),
        grid=(t // tm,),
        in_specs=specs,
        out_specs=out_specs,
        out_shape=out_shape,
        input_output_aliases=aliases,
        compiler_params=_params("parallel"),
        name="mixer_in",
    )(*ins)


KEY_CHUNK = 256


def _attend_body(has_ctx, *refs):
    if has_ctx:
        q_ref, k_ref, vext_ref, cckv_ref, ckr_ref, wkv_ref, o_ref, kctx_ref, vctx_ref = refs

        @pl.when(pl.program_id(1) == 0)
        def _():
            kv = _dot(cckv_ref[...].astype(BF16), wkv_ref[...])
            _store_keys_values(kv, ckr_ref[...].astype(BF16), kctx_ref, vctx_ref)

        sources = [(kctx_ref, vctx_ref), (k_ref, vext_ref)]
    else:
        q_ref, k_ref, vext_ref, o_ref = refs
        sources = [(k_ref, vext_ref)]
    chunks = [(kr, vr, slice(c * KEY_CHUNK, (c + 1) * KEY_CHUNK))
              for kr, vr in sources for c in range(kr.shape[0] // KEY_CHUNK)]

    def scores(hd):
        qk = slice(hd * HEAD_PAD, (hd + 1) * HEAD_PAD)
        return [_dot_nt(q_ref[:, qk], kr[rows, qk]) for kr, _, rows in chunks]

    s = scores(0)
    for hd in range(N_HEADS):
        m = jnp.max(functools.reduce(jnp.maximum, s), axis=-1, keepdims=True)
        s_next, acc = [], None
        for c, (kr, vr, rows) in enumerate(chunks):
            if hd + 1 < N_HEADS:
                qk = slice((hd + 1) * HEAD_PAD, (hd + 2) * HEAD_PAD)
                s_next.append(_dot_nt(q_ref[:, qk], kr[rows, qk]))
            p = jnp.exp2(s[c] - m).astype(BF16)
            part = _dot(p, vr[rows, hd * V_EXT:(hd + 1) * V_EXT])
            acc = part if acc is None else acc + part
        s = s_next
        o_ref[:, hd * V_DIM:(hd + 1) * V_DIM] = (acc[:, :V_DIM] / acc[:, V_DIM:]).astype(BF16)


def _attend(stream, layer, q, kcat, vext, lw, ctx, tq):
    n = stream.seq
    nq = n // tq
    has_ctx = stream.latent
    ins = [q, kcat, vext]
    specs = [pl.BlockSpec((tq, QK_W), lambda b, i: (b * nq + i, 0)),
             pl.BlockSpec((n, QK_W), lambda b, i: (b, 0)),
             pl.BlockSpec((n, VEXT_W), lambda b, i: (b, 0))]
    scratch = []
    if has_ctx:
        cache_ckv, cache_kr = ctx
        past = cache_ckv.shape[2]
        ins += [cache_ckv, cache_kr, lw["wkv"]]
        specs += [pl.BlockSpec((None, None, past, KV_RANK), lambda b, i: (b, layer, 0, 0)),
                  pl.BlockSpec((None, None, past, LANES), lambda b, i: (b, layer, 0, 0)),
                  _const_spec(lw["wkv"].shape)]
        scratch = [pltpu.VMEM((past, QK_W), BF16), pltpu.VMEM((past, VEXT_W), BF16)]
    return pl.pallas_call(
        functools.partial(_attend_body, has_ctx),
        grid=(stream.batch, nq),
        in_specs=specs,
        out_specs=pl.BlockSpec((tq, MLA_W), lambda b, i: (b * nq + i, 0)),
        out_shape=jax.ShapeDtypeStruct((stream.tokens, MLA_W), BF16),
        scratch_shapes=scratch,
        compiler_params=_params("parallel", "arbitrary"),
        name="attend",
    )(*ins)


MIX_PARTS = 4


def _mixer_residual(co_ref, mo_ref, wo1_ref, wo2_ref, x_ref, ga_ref, g_ref, b_ref):
    rp = x_ref.shape[0] // MIX_PARTS
    parts = [slice(k * rp, (k + 1) * rp) for k in range(MIX_PARTS)]
    mix = [_dot(co_ref[r], wo1_ref[...]) + _dot(mo_ref[r], wo2_ref[...]) for r in parts]
    x1 = [_layernorm(ALPHA * x_ref[r] + ga_ref[...] * mix[k], g_ref[...], b_ref[...])
          for k, r in enumerate(parts)]
    return parts, x1


def _mix_out_body(co_ref, mo_ref, wo1_ref, wo2_ref, x_ref, ga_ref, scf_ref, shf_ref, g_ref, b_ref,
                  rcat_ref, x1_ref, h2_ref, meta_ref, cnt_ref):
    parts, x1 = _mixer_residual(co_ref, mo_ref, wo1_ref, wo2_ref, x_ref, ga_ref, g_ref, b_ref)
    rp = x_ref.shape[0] // MIX_PARTS
    for k, r in enumerate(parts):
        x1_ref[r] = x1[k]
    h2 = [v * (1.0 + scf_ref[...]) + shf_ref[...] for v in x1]
    hh = [v.astype(BF16) for v in h2]
    for k, r in enumerate(parts):
        h2_ref[r] = hh[k]
    hl = [(h2[k] - hh[k].astype(F32)).astype(BF16) for k in range(MIX_PARTS)]
    ra = [_dot(v, rcat_ref[...]) for v in hh]
    rb = [_dot(v, rcat_ref[...]) for v in hl]
    logits = [ra[k][:, :LANES] + (ra[k][:, LANES:] + (rb[k][:, :LANES] + rb[k][:, LANES:]))
              for k in range(MIX_PARTS)]
    lane = lax.broadcasted_iota(jnp.int32, (rp, LANES), 1).astype(F32)
    neg = -jnp.inf
    lg = [jnp.where(lane < N_EXPERTS, v, neg) for v in logits]
    m1 = [jnp.max(v, axis=-1, keepdims=True) for v in lg]
    i1 = [jnp.min(jnp.where(lg[k] == m1[k], lane, float(LANES)), axis=-1, keepdims=True)
          for k in range(MIX_PARTS)]
    lg2 = [jnp.where(lane == i1[k], neg, lg[k]) for k in range(MIX_PARTS)]
    m2 = [jnp.max(v, axis=-1, keepdims=True) for v in lg2]
    i2 = [jnp.min(jnp.where(lg2[k] == m2[k], lane, float(LANES)), axis=-1, keepdims=True)
          for k in range(MIX_PARTS)]
    picked = [jnp.where(lane == i1[k], 1.0, jnp.where(lane == i2[k], 1.0, 0.0)) for k in range(MIX_PARTS)]
    earlier = jnp.where(lax.broadcasted_iota(jnp.int32, (rp, rp), 0)
                        > lax.broadcasted_iota(jnp.int32, (rp, rp), 1), 1.0, 0.0).astype(BF16)
    before = jnp.zeros((1, LANES), F32)
    for k, r in enumerate(parts):
        rank = _dot(earlier, picked[k].astype(BF16)) + before
        before = before + jnp.sum(picked[k], axis=0, keepdims=True)
        rank1 = jnp.sum(jnp.where(lane == i1[k], rank, 0.0), axis=-1, keepdims=True)
        rank2 = jnp.sum(jnp.where(lane == i2[k], rank, 0.0), axis=-1, keepdims=True)
        e2 = jnp.exp(m2[k] - m1[k])
        den = 1.0 + e2
        meta = jnp.zeros((rp, LANES), F32)
        for j, val in enumerate((i1[k], i2[k], 1.0 / den, e2 / den, rank1, rank2)):
            meta = jnp.where(lane == float(j), val, meta)
        meta_ref[r] = meta
    cnt_ref[...] = before


def _mix_out(stream, layer, co, mo, x, mod, lw, tm):
    t = stream.tokens
    row = lambda w: pl.BlockSpec((tm, w), lambda i: (i, 0))
    vec = _const_spec((1, D_MODEL))
    return pl.pallas_call(
        _mix_out_body,
        grid=(t // tm,),
        in_specs=[row(CHUNK_W), row(MLA_W), _const_spec(lw["wo_chunk"].shape), _const_spec(lw["wo_mla"].shape),
                  row(D_MODEL), stream.mod_spec(layer, G_A, tm), stream.mod_spec(layer, SC_F, tm),
                  stream.mod_spec(layer, SH_F, tm), vec, vec, _const_spec(lw["router"].shape)],
        out_specs=[row(D_MODEL), row(D_MODEL), row(LANES), pl.BlockSpec((None, 1, LANES), lambda i: (i, 0, 0))],
        out_shape=[jax.ShapeDtypeStruct((t, D_MODEL), F32), jax.ShapeDtypeStruct((t, D_MODEL), BF16),
                   jax.ShapeDtypeStruct((t, LANES), F32), jax.ShapeDtypeStruct((t // tm, 1, LANES), F32)],
        compiler_params=_params("parallel"),
        name="mix_out",
    )(co, mo, lw["wo_chunk"], lw["wo_mla"], x, mod, mod, mod, lw["ln_mix_g"], lw["ln_mix_b"], lw["router"])


def _mix_ffn_body(co_ref, mo_ref, wo1_ref, wo2_ref, x_ref, ga_ref, scf_ref, shf_ref, gf_ref,
                  gm_ref, bm_ref, g_ref, b_ref, wg_ref, wu_ref, wd_ref, o_ref):
    _, x1 = _mixer_residual(co_ref, mo_ref, wo1_ref, wo2_ref, x_ref, ga_ref, gm_ref, bm_ref)
    x1 = jnp.concatenate(x1, axis=0)
    h = (x1 * (1.0 + scf_ref[...]) + shf_ref[...]).astype(BF16)
    gate = _dot(h, wg_ref[...])
    up = _dot(h, wu_ref[...])
    a = (gate * _sigmoid(gate) * up).astype(BF16)
    f = _dot(a, wd_ref[...])
    o_ref[...] = _layernorm(ALPHA * x1 + gf_ref[...] * f, g_ref[...], b_ref[...])


def _mix_ffn(stream, layer, co, mo, x, mod, lw, tm):
    t = stream.tokens
    row = lambda w: pl.BlockSpec((tm, w), lambda i: (i, 0))
    vec = _const_spec((1, D_MODEL))
    resident = lambda a: pl.BlockSpec(a.shape, lambda i: (0, 0), pipeline_mode=pl.Buffered(1))
    mods = [stream.mod_spec(layer, which, tm) for which in (G_A, SC_F, SH_F, G_F)]
    return pl.pallas_call(
        _mix_ffn_body,
        grid=(t // tm,),
        in_specs=[row(CHUNK_W), row(MLA_W), resident(lw["wo_chunk"]), resident(lw["wo_mla"]), row(D_MODEL),
                  *mods, vec, vec, vec, vec,
                  resident(lw["ffn_gate"]), resident(lw["ffn_up"]), resident(lw["ffn_down"])],
        out_specs=row(D_MODEL),
        out_shape=jax.ShapeDtypeStruct((t, D_MODEL), F32),
        compiler_params=_params("parallel"),
        name="mix_ffn",
    )(co, mo, lw["wo_chunk"], lw["wo_mla"], x, mod, mod, mod, mod,
      lw["ln_mix_g"], lw["ln_mix_b"], lw["ln_ffn_g"], lw["ln_ffn_b"],
      lw["ffn_gate"], lw["ffn_up"], lw["ffn_down"])


RUN_ALIGN = 16
ROUTE_TILE = 512
EXPERT_TILE = 512


def _plan_body(cap, tile, cnt_ref, off_ref, run_ref, base_ref, te_ref, tb_ref, rows_ref, nv_ref):
    nw = cnt_ref.shape[0]
    lane = lax.broadcasted_iota(jnp.int32, (1, LANES), 1).astype(F32)
    run = jnp.floor((cnt_ref[...] + (RUN_ALIGN - 1)) * (1.0 / RUN_ALIGN)) * RUN_ALIGN
    before = jnp.where(lax.broadcasted_iota(jnp.int32, (LANES, LANES), 0)
                       < lax.broadcasted_iota(jnp.int32, (LANES, LANES), 1), 1.0, 0.0).astype(BF16)
    upto = jnp.where(lax.broadcasted_iota(jnp.int32, (LANES, LANES), 0)
                     <= lax.broadcasted_iota(jnp.int32, (LANES, LANES), 1), 1.0, 0.0).astype(BF16)
    above = jnp.where(lax.broadcasted_iota(jnp.int32, (nw, nw), 0)
                      > lax.broadcasted_iota(jnp.int32, (nw, nw), 1), 1.0, 0.0).astype(BF16)
    run_b = run.astype(BF16)
    off_ref[...] = _dot(run_b, before).astype(jnp.int32)
    run_ref[...] = run.astype(jnp.int32)
    base_ref[...] = (lane * cap + _dot(above, run_b)).astype(jnp.int32)
    totals = jnp.sum(run, axis=0, keepdims=True)
    tiles = jnp.floor((totals + (tile - 1)) * (1.0 / tile))
    cum = _dot(tiles.astype(BF16), upto)
    start = cum - tiles
    n_valid = jnp.max(cum, axis=-1, keepdims=True)
    step = jnp.minimum(lane, n_valid - 1.0)
    pick = lambda v, e: jnp.sum(jnp.where(lane == float(e), v, 0.0), axis=-1, keepdims=True)
    te = jnp.zeros_like(step)
    for e in range(N_EXPERTS):
        te = te + jnp.where(pick(cum, e) <= step, 1.0, 0.0)
    start_te, total_te = jnp.zeros_like(step), jnp.zeros_like(step)
    for e in range(N_EXPERTS):
        start_te = jnp.where(te == float(e), pick(start, e), start_te)
        total_te = jnp.where(te == float(e), pick(totals, e), total_te)
    tj = step - start_te
    te_ref[...] = te.astype(jnp.int32)
    tb_ref[...] = (te * (cap // tile) + tj).astype(jnp.int32)
    rows_ref[...] = jnp.minimum(float(tile), total_te - tj * tile).astype(jnp.int32)
    nv_ref[...] = jnp.broadcast_to(n_valid, (1, LANES)).astype(jnp.int32)


def _route_plan(counts, tokens, tm, tile):
    nw = tokens // tm
    cnt = jnp.concatenate([c.reshape(-1, LANES) for c in counts], axis=0)
    worst_pad = (RUN_ALIGN - 1) * nw
    cap = pl.cdiv(tokens + worst_pad, tile) * tile
    n_tiles = (2 * tokens + N_EXPERTS * worst_pad) // tile + N_EXPERTS
    assert n_tiles <= LANES and tm % RUN_ALIGN == 0
    per_run = jax.ShapeDtypeStruct((nw, LANES), jnp.int32)
    per_step = jax.ShapeDtypeStruct((1, LANES), jnp.int32)
    off, run, base, te, tb, rows, nv = pl.pallas_call(
        functools.partial(_plan_body, cap, tile),
        out_shape=[per_run, per_run, per_run, per_step, per_step, per_step, per_step],
        name="moe_plan",
    )(cnt)
    return dict(
        tm=tm, cap=cap, tile=tile, n_tiles=n_tiles, buf_rows=2 * tm + N_EXPERTS * RUN_ALIGN,
        off=off.reshape(-1), run=run.reshape(-1), base=base.reshape(-1),
        tile_expert=te.reshape(-1), tile_block=tb.reshape(-1), tile_rows=rows.reshape(-1),
        n_valid=nv.reshape(-1))


def _run_piece_sizes(tm):
    sizes, b = [], RUN_ALIGN
    while b <= tm:
        sizes.append(b)
        b *= 2
    return sizes


def _row_run_copies(src, dst, src_off, dst_off, n_rows, sem, sizes, action):
    for b in sizes:
        done = n_rows & (-2 * b)
        cp = pltpu.make_async_copy(
            src.at[pl.ds(pl.multiple_of(src_off + done, RUN_ALIGN), b)],
            dst.at[pl.ds(pl.multiple_of(dst_off + done, RUN_ALIGN), b)], sem)

        @pl.when((n_rows & b) != 0)
        def _():
            action(cp)


def _local_rows(meta, off_ref, w):
    i1, i2 = meta[:, 0:1], meta[:, 1:2]
    o1, o2 = jnp.zeros_like(i1), jnp.zeros_like(i2)
    for e in range(N_EXPERTS):
        off_e = off_ref[w * LANES + e].astype(F32)
        o1 = jnp.where(i1 == float(e), off_e, o1)
        o2 = jnp.where(i2 == float(e), off_e, o2)
    return (o1 + meta[:, 4:5]).astype(jnp.int32), (o2 + meta[:, 5:6]).astype(jnp.int32)


def _start(cp):
    cp.start()


def _wait(cp):
    cp.wait()


def _dispatch_body(tm, nw, tile0, off_ref, run_ref, base_ref, h_ref, meta_ref, *refs):
    xs_ref, xw_ref, sems = refs[-3:]
    w = pl.program_id(0)
    slot = w % 2
    sizes = _run_piece_sizes(tm)

    def push(win, buf, action):
        for e in range(N_EXPERTS):
            k = (tile0 + win) * LANES + e
            _row_run_copies(xw_ref.at[buf], xs_ref, off_ref[k], base_ref[k], run_ref[k],
                            sems.at[buf], sizes, action)

    r1, r2 = _local_rows(meta_ref[...], off_ref, tile0 + w)
    col = lax.broadcasted_iota(jnp.int32, (tm, xw_ref.shape[1]), 1)
    sel = jnp.where(col == r1, 1.0, jnp.where(col == r2, 1.0, 0.0)).astype(BF16)
    xw = lax.dot_general(sel, h_ref[...], (((0,), (0,)), ((), ())), preferred_element_type=F32)
    xw_ref[slot] = xw.astype(BF16)
    push(w, slot, _start)

    @pl.when(w > 0)
    def _():
        push(w - 1, 1 - slot, _wait)

    @pl.when(w == nw - 1)
    def _():
        push(w, slot, _wait)


def _dispatch(h2, meta, plan, tile0, xs=None):
    tm = plan["tm"]
    nw = h2.shape[0] // tm
    ins = [plan["off"], plan["run"], plan["base"], h2, meta]
    specs = [pl.BlockSpec((tm, D_MODEL), lambda i, *_: (i, 0)),
             pl.BlockSpec((tm, LANES), lambda i, *_: (i, 0))]
    aliases = {}
    if xs is not None:
        aliases = {len(ins): 0}
        ins.append(xs)
        specs.append(pl.BlockSpec(memory_space=pl.ANY))
    return pl.pallas_call(
        functools.partial(_dispatch_body, tm, nw, tile0),
        grid_spec=pltpu.PrefetchScalarGridSpec(
            num_scalar_prefetch=3,
            grid=(nw,),
            in_specs=specs,
            out_specs=pl.BlockSpec(memory_space=pl.ANY),
            scratch_shapes=[pltpu.VMEM((2, plan["buf_rows"], D_MODEL), BF16),
                            pltpu.SemaphoreType.DMA((2,))]),
        out_shape=jax.ShapeDtypeStruct((N_EXPERTS * plan["cap"], D_MODEL), BF16),
        input_output_aliases=aliases,
        compiler_params=_params("arbitrary"),
        name="moe_dispatch",
    )(*ins)


def _experts_body(te_ref, tb_ref, rows_ref, nv_ref, x_ref, wg_ref, wu_ref, wd_ref, o_ref):
    i = pl.program_id(0)
    rows = rows_ref[i]
    valid = i < nv_ref[0]
    half = x_ref.shape[0] // 2

    def swiglu(n):
        x = x_ref[0:n, :]
        live = lax.broadcasted_iota(jnp.int32, (n, 1), 0) < rows
        x = jnp.where(live, x, jnp.zeros_like(x))
        gate = _dot(x, wg_ref[...])
        up = _dot(x, wu_ref[...])
        a = (gate * _sigmoid(gate) * up).astype(BF16)
        o_ref[0:n, :] = _dot(a, wd_ref[...]).astype(BF16)

    @pl.when(jnp.logical_and(valid, rows > half))
    def _():
        swiglu(2 * half)

    @pl.when(jnp.logical_and(valid, rows <= half))
    def _():
        swiglu(half)


def _experts(xs, plan, lw):
    dff = lw["moe_gate"].shape[2]
    rows = pl.BlockSpec((plan["tile"], D_MODEL), lambda i, te, tb, *_: (tb[i], 0))
    return pl.pallas_call(
        _experts_body,
        grid_spec=pltpu.PrefetchScalarGridSpec(
            num_scalar_prefetch=4,
            grid=(plan["n_tiles"],),
            in_specs=[rows,
                      pl.BlockSpec((None, D_MODEL, dff), lambda i, te, *_: (te[i], 0, 0)),
                      pl.BlockSpec((None, D_MODEL, dff), lambda i, te, *_: (te[i], 0, 0)),
                      pl.BlockSpec((None, dff, D_MODEL), lambda i, te, *_: (te[i], 0, 0))],
            out_specs=rows),
        out_shape=jax.ShapeDtypeStruct(xs.shape, BF16),
        compiler_params=_params("arbitrary"),
        name="moe_experts",
    )(plan["tile_expert"], plan["tile_block"], plan["tile_rows"], plan["n_valid"],
      xs, lw["moe_gate"], lw["moe_up"], lw["moe_down"])


def _combine_body(tm, nw, tile0, off_ref, run_ref, base_ref, ys_ref, meta_ref, x_ref, gf_ref, g_ref,
                  b_ref, o_ref, yw_ref, sems):
    w = pl.program_id(0)
    slot = w % 2
    sizes = _run_piece_sizes(tm)

    def fetch(win, buf, action):
        for e in range(N_EXPERTS):
            k = (tile0 + win) * LANES + e
            _row_run_copies(ys_ref, yw_ref.at[buf], base_ref[k], off_ref[k], run_ref[k],
                            sems.at[buf], sizes, action)

    @pl.when(w == 0)
    def _():
        yw_ref[...] = jnp.zeros_like(yw_ref)
        fetch(0, 0, _start)

    @pl.when(w + 1 < nw)
    def _():
        fetch(w + 1, 1 - slot, _start)

    fetch(w, slot, _wait)
    rp = tm // MIX_PARTS
    parts = [slice(k * rp, (k + 1) * rp) for k in range(MIX_PARTS)]
    meta = [meta_ref[r] for r in parts]
    picks = [_local_rows(v, off_ref, tile0 + w) for v in meta]
    col = lax.broadcasted_iota(jnp.int32, (rp, yw_ref.shape[1]), 1)
    yw = yw_ref[slot]
    gate = [jnp.where(col == r1, v[:, 2:3], jnp.where(col == r2, v[:, 3:4], 0.0)).astype(BF16)
            for v, (r1, r2) in zip(meta, picks)]
    f = [_dot(g, yw) for g in gate]
    for k, r in enumerate(parts):
        o_ref[r] = _layernorm(ALPHA * x_ref[r] + gf_ref[...] * f[k], g_ref[...], b_ref[...])


def _combine(stream, layer, ys, meta, x1, mod, plan, lw, tile0):
    tm = plan["tm"]
    nw = stream.tokens // tm
    row = lambda width: pl.BlockSpec((tm, width), lambda i, *_: (i, 0))
    vec = pl.BlockSpec((1, D_MODEL), lambda i, *_: (0, 0))
    mod_spec = stream.mod_spec(layer, G_F, tm)
    mod_spec = pl.BlockSpec(mod_spec.block_shape, lambda i, *_, f=mod_spec.index_map: f(i))
    return pl.pallas_call(
        functools.partial(_combine_body, tm, nw, tile0),
        grid_spec=pltpu.PrefetchScalarGridSpec(
            num_scalar_prefetch=3,
            grid=(nw,),
            in_specs=[pl.BlockSpec(memory_space=pl.ANY), row(LANES), row(D_MODEL), mod_spec, vec, vec],
            out_specs=row(D_MODEL),
            scratch_shapes=[pltpu.VMEM((2, plan["buf_rows"], D_MODEL), BF16),
                            pltpu.SemaphoreType.DMA((2,))]),
        out_shape=jax.ShapeDtypeStruct((stream.tokens, D_MODEL), F32),
        compiler_params=_params("arbitrary"),
        name="moe_combine",
    )(plan["off"], plan["run"], plan["base"], ys, meta, x1, mod, lw["ln_ffn_g"], lw["ln_ffn_b"])


def _rope_tables(n_tokens):
    rows = n_tokens // GRID_W
    row = np.repeat(np.arange(rows, dtype=np.float64), GRID_W)
    col = np.tile(np.arange(GRID_W, dtype=np.float64), rows)
    half = ROPE_DIM // 2
    inv_freq = ROPE_BASE ** (-np.arange(0, half, 2, dtype=np.float64) / half)
    ar = row[:, None] * inv_freq[None, :]
    ac = col[:, None] * inv_freq[None, :]
    cos = np.concatenate([np.cos(ar), np.cos(ar), np.cos(ac), np.cos(ac)], axis=-1)
    sin = np.concatenate([-np.sin(ar), np.sin(ar), -np.sin(ac), np.sin(ac)], axis=-1)
    return jnp.asarray(np.tile(cos, (1, N_HEADS)), F32), jnp.asarray(np.tile(sin, (1, N_HEADS)), F32)


def _swap_rope_halves(a):
    q = ROPE_DIM // 4
    return jnp.concatenate([a[..., q:2 * q], a[..., :q], a[..., 3 * q:], a[..., 2 * q:3 * q]], axis=-1)


def _layer_weights(l, p):
    w_in = p["w_in"][l].astype(BF16)
    o_kr = 2 * CHUNK_W + Q_RANK + KV_RANK
    zpad = jnp.zeros((D_MODEL, LANES - ROPE_DIM), BF16)
    w_in_ext = jnp.concatenate(
        [w_in, zpad, _swap_rope_halves(w_in[:, o_kr:o_kr + ROPE_DIM]), zpad], axis=1)

    w_uq = p["w_uq"][l].astype(BF16).reshape(Q_RANK, N_HEADS, QK_NOPE + ROPE_DIM)
    wq_rope = w_uq[:, :, QK_NOPE:]
    wq = jnp.concatenate([w_uq[:, :, :QK_NOPE].reshape(Q_RANK, N_HEADS * QK_NOPE),
                          wq_rope.reshape(Q_RANK, N_HEADS * ROPE_DIM),
                          _swap_rope_halves(wq_rope).reshape(Q_RANK, N_HEADS * ROPE_DIM)], axis=1)

    w_ukv = p["w_ukv"][l].astype(BF16).reshape(KV_RANK, N_HEADS, QK_NOPE + V_DIM)
    wkv = jnp.concatenate([w_ukv[:, :, :QK_NOPE].reshape(KV_RANK, N_HEADS * QK_NOPE),
                           w_ukv[:, :, QK_NOPE:].reshape(KV_RANK, MLA_W)], axis=1)
    w_out = p["w_out"][l]
    lw = {
        "w_in": w_in_ext,
        "q_g": p["q_norm_g"][l].reshape(1, Q_RANK),
        "kv_g": p["kv_norm_g"][l].reshape(1, KV_RANK),
        "wq": wq,
        "wkv": wkv,
        "ln_v_g": p["chunk_ln_g"][l].reshape(1, CHUNK_W),
        "w_s": p["w_spatial"][l].astype(BF16),
        "b_s": jnp.broadcast_to(p["b_spatial"][l][:, :, None], (N_GROUPS, CHUNK, GROUP_W)),
        "wo_chunk": w_out[:CHUNK_W].astype(BF16),
        "wo_mla": w_out[CHUNK_W:].astype(BF16),
        "ln_mix_g": p["ln_mix_g"][l].reshape(1, D_MODEL),
        "ln_mix_b": p["ln_mix_b"][l].reshape(1, D_MODEL),
        "ln_ffn_g": p["ln_ffn_g"][l].reshape(1, D_MODEL),
        "ln_ffn_b": p["ln_ffn_b"][l].reshape(1, D_MODEL),
    }
    if l % 2 == 0:
        lw["ffn_gate"] = p["ffn_w_gate"][l // 2].astype(BF16)
        lw["ffn_up"] = p["ffn_w_up"][l // 2].astype(BF16)
        lw["ffn_down"] = p["ffn_w_down"][l // 2].astype(BF16)
    else:
        r = jnp.pad(p["router_w"][l // 2], ((0, 0), (0, LANES - N_EXPERTS)))
        r_hi = r.astype(BF16)
        lw["router"] = jnp.concatenate([r_hi, (r - r_hi.astype(F32)).astype(BF16)], axis=1)
        lw["moe_gate"] = p["moe_w_gate"][l // 2].astype(BF16)
        lw["moe_up"] = p["moe_w_up"][l // 2].astype(BF16)
        lw["moe_down"] = p["moe_w_down"][l // 2].astype(BF16)
    return lw


def _trunk(streams, xs, mod, weights, rope, ctx):
    caches = None
    for l in range(DEPTH):
        lw = weights[l]
        mixed = []
        for stream, x in zip(streams, xs):
            outs = _mixer_in(stream, l, x, mod, lw, rope, stream.tiles["mixer_in"],
                             None if stream.latent else caches)
            co, q, kcat, vext = outs[:4]
            if not stream.latent:
                caches = outs[4:]
            mixed.append((co, _attend(stream, l, q, kcat, vext, lw, ctx, stream.tiles["attend"])))
        if l % 2 == 0:
            xs = [_mix_ffn(stream, l, co, mo, x, mod, lw, stream.tiles["ffn"])
                  for stream, x, (co, mo) in zip(streams, xs, mixed)]
            continue
        routed = [_mix_out(stream, l, co, mo, x, mod, lw, ROUTE_TILE)
                  for stream, x, (co, mo) in zip(streams, xs, mixed)]
        plan = _route_plan([r[3] for r in routed], sum(s.tokens for s in streams), ROUTE_TILE, EXPERT_TILE)
        tile0, sorted_rows = [], None
        for stream, (_, h2, meta, _) in zip(streams, routed):
            tile0.append(sum(s.tokens for s in streams[:len(tile0)]) // ROUTE_TILE)
            sorted_rows = _dispatch(h2, meta, plan, tile0[-1], sorted_rows)
        ys = _experts(sorted_rows, plan, lw)
        xs = [_combine(stream, l, ys, meta, x1, mod, plan, lw, t0)
              for stream, (x1, _, meta, _), t0 in zip(streams, routed, tile0)]
    return xs, caches


def kernel(x_prompt, x_sample, c, cache_ckv, cache_krope, c_ctx, w_mod, b_mod, w_in, q_norm_g, kv_norm_g, w_uq, w_ukv, chunk_ln_g, w_spatial, b_spatial, w_out, ln_mix_g, ln_mix_b, ln_ffn_g, ln_ffn_b, ffn_w_gate, ffn_w_up, ffn_w_down, router_w, moe_w_gate, moe_w_up, moe_w_down):
    p = dict(w_in=w_in, q_norm_g=q_norm_g, kv_norm_g=kv_norm_g, w_uq=w_uq, w_ukv=w_ukv,
             chunk_ln_g=chunk_ln_g, w_spatial=w_spatial, b_spatial=b_spatial, w_out=w_out,
             ln_mix_g=ln_mix_g, ln_mix_b=ln_mix_b, ln_ffn_g=ln_ffn_g, ln_ffn_b=ln_ffn_b,
             ffn_w_gate=ffn_w_gate, ffn_w_up=ffn_w_up, ffn_w_down=ffn_w_down, router_w=router_w,
             moe_w_gate=moe_w_gate, moe_w_up=moe_w_up, moe_w_down=moe_w_down)
    weights = [_layer_weights(l, p) for l in range(DEPTH)]

    batch, seq, _ = x_prompt.shape
    dec_batch, dec_seq, _ = x_sample.shape
    cond_rows = jnp.concatenate(
        [c_ctx[None, :], c, jnp.zeros((MOD_ROWS - 1 - dec_batch, D_MODEL), F32)], axis=0)
    mod = _modulation(cond_rows, w_mod, b_mod)

    prompt = _Stream(batch, seq, mod_row0=0, per_row_mod=False, latent=False,
                     tiles=dict(mixer_in=512, attend=256, ffn=512))
    sample = _Stream(dec_batch, dec_seq, mod_row0=1, per_row_mod=True, latent=True,
                     tiles=dict(mixer_in=1024, attend=1024, ffn=512))
    rope = _rope_tables(dec_seq)
    ctx = (cache_ckv, jnp.pad(cache_krope, ((0, 0), (0, 0), (0, 0), (0, LANES - ROPE_DIM))))
    (y_prompt, y_sample), (new_ckv, new_krope) = _trunk(
        [prompt, sample],
        [x_prompt.reshape(batch * seq, D_MODEL), x_sample.reshape(dec_batch * dec_seq, D_MODEL)],
        mod, weights, rope, ctx)
    return (y_prompt.reshape(batch, seq, D_MODEL), y_sample.reshape(dec_batch, dec_seq, D_MODEL),
            new_ckv, new_krope)
```

```python
import functools
import math

import jax
import jax.numpy as jnp
import numpy as np
from jax import lax
from jax.experimental import pallas as pl
from jax.experimental.pallas import tpu as pltpu

F32 = jnp.float32
BF16 = jnp.bfloat16

D_MODEL = 1024
DEPTH = 2
GRID_W = 64
CHUNK = 128
N_GROUPS = 4
GROUP_W = 128
CHUNK_W = N_GROUPS * GROUP_W
N_HEADS = 4
QK_NOPE = 128
ROPE_DIM = 64
V_DIM = 128
Q_RANK = 384
KV_RANK = 256
MLA_W = N_HEADS * V_DIM
HEAD_PAD = 256
QK_W = N_HEADS * HEAD_PAD
V_EXT = 2 * V_DIM
VEXT_W = N_HEADS * V_EXT
ROPE_BASE = 10000.0
N_EXPERTS = 8
ALPHA = (2 * DEPTH) ** 0.25
EPS = 1e-6
ATTN_SCALE = math.log2(math.e) / math.sqrt(QK_NOPE + ROPE_DIM)
MOD_ROWS = 16
LANES = 128
V7X_VMEM_BYTES = 64 * 1024 * 1024
VMEM_LIMIT = V7X_VMEM_BYTES * 7 // 8

SH_A, SC_A, G_A, SH_F, SC_F, G_F = range(6)


def _sigmoid(x):
    return 1.0 / (1.0 + jnp.exp(-x))


def _gelu_tanh(x):
    return 0.5 * x * (1.0 + jnp.tanh(math.sqrt(2.0 / math.pi) * (x + 0.044715 * (x * x * x))))


def _layernorm(y, g, b):
    mu = jnp.mean(y, axis=-1, keepdims=True)
    d = y - mu
    var = jnp.mean(d * d, axis=-1, keepdims=True)
    return d * lax.rsqrt(var + EPS) * g + b


def _rmsnorm(y, g):
    return y * lax.rsqrt(jnp.mean(y * y, axis=-1, keepdims=True) + EPS) * g


def _dot(a, b):
    return jnp.dot(a, b, preferred_element_type=F32)


def _dot_nt(a, b):
    return lax.dot_general(a, b, (((1,), (1,)), ((), ())), preferred_element_type=F32)


def _params(*sem):
    return pltpu.CompilerParams(dimension_semantics=sem, vmem_limit_bytes=VMEM_LIMIT)


def _const_spec(shape):
    nd = len(shape)
    return pl.BlockSpec(shape, lambda *_: (0,) * nd)


class _Stream:
    def __init__(self, batch, seq, mod_row0, per_row_mod, latent, tiles):
        self.tiles = tiles
        self.batch = batch
        self.seq = seq
        self.tokens = batch * seq
        self.mod_row0 = mod_row0
        self.per_row_mod = per_row_mod
        self.latent = latent

    def mod_spec(self, layer, which, tm):
        tiles_per_row = self.seq // tm
        row0, per_row = self.mod_row0, self.per_row_mod

        def index(i):
            r = row0 + (i // tiles_per_row if per_row else 0)
            return (layer, r, which, 0, 0)

        return pl.BlockSpec((None, None, None, 1, D_MODEL), index)


def _mod_body(c_ref, w_ref, b_ref, o_ref):
    a = c_ref[...]
    a = (a * _sigmoid(a)).astype(BF16)
    o_ref[:, 0, :] = _dot(a, w_ref[...].astype(BF16)) + b_ref[...]


def _modulation(cond_rows, w_mod, b_mod):
    depth, _, width = w_mod.shape
    n_roles = width // D_MODEL
    return pl.pallas_call(
        _mod_body,
        grid=(depth, n_roles),
        in_specs=[
            pl.BlockSpec((MOD_ROWS, D_MODEL), lambda l, j: (0, 0)),
            pl.BlockSpec((None, D_MODEL, D_MODEL), lambda l, j: (l, 0, j)),
            pl.BlockSpec((None, 1, D_MODEL), lambda l, j: (l, 0, j)),
        ],
        out_specs=pl.BlockSpec((None, MOD_ROWS, None, 1, D_MODEL), lambda l, j: (l, 0, j, 0, 0)),
        out_shape=jax.ShapeDtypeStruct((depth, MOD_ROWS, n_roles, 1, D_MODEL), F32),
        compiler_params=_params("parallel", "parallel"),
        name="modulation",
    )(cond_rows, w_mod, b_mod.reshape(depth, 1, width))


def _store_keys_values(kv, kr, kcat_ref, vext_ref):
    ones = jnp.ones((kv.shape[0], V_EXT - V_DIM), BF16)
    for hd in range(N_HEADS):
        a = hd * HEAD_PAD
        kcat_ref[:, a:a + QK_NOPE] = kv[:, hd * QK_NOPE:(hd + 1) * QK_NOPE].astype(BF16)
        kcat_ref[:, a + QK_NOPE:a + HEAD_PAD] = kr
        b = hd * V_EXT
        vext_ref[:, b:b + V_DIM] = kv[:, (N_HEADS + hd) * V_DIM:(N_HEADS + hd + 1) * V_DIM].astype(BF16)
        vext_ref[:, b + V_DIM:b + V_EXT] = ones


def _mixer_in_body(latent, tm, n_aliased, *refs):
    it = iter(refs)
    x_ref, sc_ref, sh_ref, win_ref, qg_ref, kvg_ref, wq_ref = (next(it) for _ in range(7))
    wkv_ref, lng_ref, ws_ref, bs_ref = (next(it) for _ in range(4))
    cos_ref, sin_ref = (next(it), next(it)) if latent else (None, None)
    for _ in range(n_aliased):
        next(it)
    co_ref, q_ref, kcat_ref, vext_ref = (next(it) for _ in range(4))
    ckv_ref, kr_ref = (None, None) if latent else (next(it), next(it))

    h = (x_ref[...] * (1.0 + sc_ref[...]) + sh_ref[...]).astype(BF16)
    p = _dot(h, win_ref[...])

    for g in range(N_GROUPS):
        cols = slice(g * GROUP_W, (g + 1) * GROUP_W)
        vg = _gelu_tanh(p[:, CHUNK_W + g * GROUP_W:CHUNK_W + (g + 1) * GROUP_W])
        mu = jnp.mean(vg, axis=-1, keepdims=True)
        d = vg - mu
        var = jnp.mean(d * d, axis=-1, keepdims=True)
        vn = (d * lax.rsqrt(var + EPS) * lng_ref[:, cols]).astype(BF16)
        ug = _gelu_tanh(p[:, cols])
        for c in range(tm // CHUNK):
            rows = slice(c * CHUNK, (c + 1) * CHUNK)
            z = _dot(ws_ref[g], vn[rows]) + bs_ref[g]
            co_ref[rows, cols] = (ug[rows] * z).astype(BF16)

    o = 2 * CHUNK_W
    cqn = _rmsnorm(p[:, o:o + Q_RANK], qg_ref[...]).astype(BF16)
    qc = _dot(cqn, wq_ref[...])
    q_rope = qc[:, N_HEADS * QK_NOPE:N_HEADS * (QK_NOPE + ROPE_DIM)]
    if latent:
        cos = cos_ref[...]
        sin = sin_ref[...]
        q_rope = q_rope * cos + qc[:, N_HEADS * (QK_NOPE + ROPE_DIM):] * sin
    for hd in range(N_HEADS):
        a = hd * HEAD_PAD
        q_ref[:, a:a + QK_NOPE] = (qc[:, hd * QK_NOPE:(hd + 1) * QK_NOPE] * ATTN_SCALE).astype(BF16)
        pair = q_rope[:, (hd // 2) * LANES:(hd // 2 + 1) * LANES]
        if hd % 2:
            pair = pltpu.roll(pair, ROPE_DIM, axis=1)
        q_ref[:, a + QK_NOPE:a + HEAD_PAD] = (pair * ATTN_SCALE).astype(BF16)

    o += Q_RANK
    ckvn = _rmsnorm(p[:, o:o + KV_RANK], kvg_ref[...])
    o += KV_RANK
    kr = p[:, o:o + LANES]
    if not latent:
        seq = ckv_ref.shape[1]
        for b in range(ckv_ref.shape[0]):
            ckv_ref[b] = ckvn[b * seq:(b + 1) * seq]
            kr_ref[b] = kr[b * seq:(b + 1) * seq, :ROPE_DIM]
    else:
        kr = kr * cos[:, :LANES] + p[:, o + LANES:o + 2 * LANES] * sin[:, :LANES]
    kr = kr.astype(BF16)
    kv = _dot(ckvn.astype(BF16), wkv_ref[...])
    _store_keys_values(kv, kr, kcat_ref, vext_ref)


def _mixer_in(stream, layer, x, mod, lw, rope, tm, caches=None):
    latent = stream.latent
    t = stream.tokens
    row = lambda w: pl.BlockSpec((tm, w), lambda i: (i, 0))
    win = lw["w_in"]
    win_cols = win.shape[1] if latent else win.shape[1] - LANES
    wq = lw["wq"]
    wq_cols = wq.shape[1] if latent else wq.shape[1] - N_HEADS * ROPE_DIM
    ins = [x, mod, mod, win, lw["q_g"], lw["kv_g"], wq]
    specs = [row(D_MODEL), stream.mod_spec(layer, SC_A, tm), stream.mod_spec(layer, SH_A, tm),
             _const_spec((D_MODEL, win_cols)), _const_spec(lw["q_g"].shape), _const_spec(lw["kv_g"].shape),
             _const_spec((Q_RANK, wq_cols))]
    for name in ("wkv", "ln_v_g", "w_s", "b_s"):
        ins.append(lw[name])
        specs.append(_const_spec(lw[name].shape))
    if latent:
        tiles_per_seq = stream.seq // tm
        pos = pl.BlockSpec((tm, N_HEADS * ROPE_DIM), lambda i: (i % tiles_per_seq, 0))
        ins += [rope[0], rope[1]]
        specs += [pos, pos]
    out_shape = [jax.ShapeDtypeStruct((t, CHUNK_W), BF16), jax.ShapeDtypeStruct((t, QK_W), BF16),
                 jax.ShapeDtypeStruct((t, QK_W), BF16), jax.ShapeDtypeStruct((t, VEXT_W), BF16)]
    out_specs = [row(CHUNK_W), row(QK_W), row(QK_W), row(VEXT_W)]
    aliases = {}
    if not latent:
        for rank in (KV_RANK, ROPE_DIM):
            out_shape.append(jax.ShapeDtypeStruct((stream.batch, DEPTH, stream.seq, rank), F32))
            out_specs.append(pl.BlockSpec((tm // stream.seq, None, stream.seq, rank),
                                          lambda i: (i, layer, 0, 0)))
        for k, buf in enumerate(caches or ()):
            aliases[len(ins)] = len(out_shape) - 2 + k
            ins.append(buf)
            specs.append(pl.BlockSpec(memory_space=pl.ANY))
    return pl.pallas_call(
        functools.partial(_mixer_in_body, latent, tm, len(aliases)),
        grid=(t // tm,),
        in_specs=specs,
        out_specs=out_specs,
        out_shape=out_shape,
        input_output_aliases=aliases,
        compiler_params=_params("parallel"),
        name="mixer_in",
    )(*ins)


KEY_CHUNK = 256


def _attend_body(has_ctx, *refs):
    if has_ctx:
        q_ref, k_ref, vext_ref, cckv_ref, ckr_ref, wkv_ref, o_ref, kctx_ref, vctx_ref = refs

        @pl.when(pl.program_id(1) == 0)
        def _():
            kv = _dot(cckv_ref[...].astype(BF16), wkv_ref[...])
            _store_keys_values(kv, ckr_ref[...].astype(BF16), kctx_ref, vctx_ref)

        sources = [(kctx_ref, vctx_ref), (k_ref, vext_ref)]
    else:
        q_ref, k_ref, vext_ref, o_ref = refs
        sources = [(k_ref, vext_ref)]
    chunks = [(kr, vr, slice(c * KEY_CHUNK, (c + 1) * KEY_CHUNK))
              for kr, vr in sources for c in range(kr.shape[0] // KEY_CHUNK)]

    def scores(hd):
        qk = slice(hd * HEAD_PAD, (hd + 1) * HEAD_PAD)
        return [_dot_nt(q_ref[:, qk], kr[rows, qk]) for kr, _, rows in chunks]

    s = scores(0)
    for hd in range(N_HEADS):
        m = jnp.max(functools.reduce(jnp.maximum, s), axis=-1, keepdims=True)
        s_next, acc = [], None
        for c, (kr, vr, rows) in enumerate(chunks):
            if hd + 1 < N_HEADS:
                qk = slice((hd + 1) * HEAD_PAD, (hd + 2) * HEAD_PAD)
                s_next.append(_dot_nt(q_ref[:, qk], kr[rows, qk]))
            p = jnp.exp2(s[c] - m).astype(BF16)
            part = _dot(p, vr[rows, hd * V_EXT:(hd + 1) * V_EXT])
            acc = part if acc is None else acc + part
        s = s_next
        o_ref[:, hd * V_DIM:(hd + 1) * V_DIM] = (acc[:, :V_DIM] / acc[:, V_DIM:]).astype(BF16)


def _attend(stream, layer, q, kcat, vext, lw, ctx, tq):
    n = stream.seq
    nq = n // tq
    has_ctx = stream.latent
    ins = [q, kcat, vext]
    specs = [pl.BlockSpec((tq, QK_W), lambda b, i: (b * nq + i, 0)),
             pl.BlockSpec((n, QK_W), lambda b, i: (b, 0)),
             pl.BlockSpec((n, VEXT_W), lambda b, i: (b, 0))]
    scratch = []
    if has_ctx:
        cache_ckv, cache_kr = ctx
        past = cache_ckv.shape[2]
        ins += [cache_ckv, cache_kr, lw["wkv"]]
        specs += [pl.BlockSpec((None, None, past, KV_RANK), lambda b, i: (b, layer, 0, 0)),
                  pl.BlockSpec((None, None, past, LANES), lambda b, i: (b, layer, 0, 0)),
                  _const_spec(lw["wkv"].shape)]
        scratch = [pltpu.VMEM((past, QK_W), BF16), pltpu.VMEM((past, VEXT_W), BF16)]
    return pl.pallas_call(
        functools.partial(_attend_body, has_ctx),
        grid=(stream.batch, nq),
        in_specs=specs,
        out_specs=pl.BlockSpec((tq, MLA_W), lambda b, i: (b * nq + i, 0)),
        out_shape=jax.ShapeDtypeStruct((stream.tokens, MLA_W), BF16),
        scratch_shapes=scratch,
        compiler_params=_params("parallel", "arbitrary"),
        name="attend",
    )(*ins)


MIX_PARTS = 4


def _mixer_residual(co_ref, mo_ref, wo1_ref, wo2_ref, x_ref, ga_ref, g_ref, b_ref):
    rp = x_ref.shape[0] // MIX_PARTS
    parts = [slice(k * rp, (k + 1) * rp) for k in range(MIX_PARTS)]
    mix = [_dot(co_ref[r], wo1_ref[...]) + _dot(mo_ref[r], wo2_ref[...]) for r in parts]
    x1 = [_layernorm(ALPHA * x_ref[r] + ga_ref[...] * mix[k], g_ref[...], b_ref[...])
          for k, r in enumerate(parts)]
    return parts, x1


def _mix_out_body(co_ref, mo_ref, wo1_ref, wo2_ref, x_ref, ga_ref, scf_ref, shf_ref, g_ref, b_ref,
                  rcat_ref, x1_ref, h2_ref, meta_ref, cnt_ref):
    parts, x1 = _mixer_residual(co_ref, mo_ref, wo1_ref, wo2_ref, x_ref, ga_ref, g_ref, b_ref)
    rp = x_ref.shape[0] // MIX_PARTS
    for k, r in enumerate(parts):
        x1_ref[r] = x1[k]
    h2 = [v * (1.0 + scf_ref[...]) + shf_ref[...] for v in x1]
    hh = [v.astype(BF16) for v in h2]
    for k, r in enumerate(parts):
        h2_ref[r] = hh[k]
    hl = [(h2[k] - hh[k].astype(F32)).astype(BF16) for k in range(MIX_PARTS)]
    ra = [_dot(v, rcat_ref[...]) for v in hh]
    rb = [_dot(v, rcat_ref[...]) for v in hl]
    logits = [ra[k][:, :LANES] + (ra[k][:, LANES:] + (rb[k][:, :LANES] + rb[k][:, LANES:]))
              for k in range(MIX_PARTS)]
    lane = lax.broadcasted_iota(jnp.int32, (rp, LANES), 1).astype(F32)
    neg = -jnp.inf
    lg = [jnp.where(lane < N_EXPERTS, v, neg) for v in logits]
    m1 = [jnp.max(v, axis=-1, keepdims=True) for v in lg]
    i1 = [jnp.min(jnp.where(lg[k] == m1[k], lane, float(LANES)), axis=-1, keepdims=True)
          for k in range(MIX_PARTS)]
    lg2 = [jnp.where(lane == i1[k], neg, lg[k]) for k in range(MIX_PARTS)]
    m2 = [jnp.max(v, axis=-1, keepdims=True) for v in lg2]
    i2 = [jnp.min(jnp.where(lg2[k] == m2[k], lane, float(LANES)), axis=-1, keepdims=True)
          for k in range(MIX_PARTS)]
    picked = [jnp.where(lane == i1[k], 1.0, jnp.where(lane == i2[k], 1.0, 0.0)) for k in range(MIX_PARTS)]
    earlier = jnp.where(lax.broadcasted_iota(jnp.int32, (rp, rp), 0)
                        > lax.broadcasted_iota(jnp.int32, (rp, rp), 1), 1.0, 0.0).astype(BF16)
    before = jnp.zeros((1, LANES), F32)
    for k, r in enumerate(parts):
        rank = _dot(earlier, picked[k].astype(BF16)) + before
        before = before + jnp.sum(picked[k], axis=0, keepdims=True)
        rank1 = jnp.sum(jnp.where(lane == i1[k], rank, 0.0), axis=-1, keepdims=True)
        rank2 = jnp.sum(jnp.where(lane == i2[k], rank, 0.0), axis=-1, keepdims=True)
        e2 = jnp.exp(m2[k] - m1[k])
        den = 1.0 + e2
        meta = jnp.zeros((rp, LANES), F32)
        for j, val in enumerate((i1[k], i2[k], 1.0 / den, e2 / den, rank1, rank2)):
            meta = jnp.where(lane == float(j), val, meta)
        meta_ref[r] = meta
    cnt_ref[...] = before


def _mix_out(stream, layer, co, mo, x, mod, lw, tm):
    t = stream.tokens
    row = lambda w: pl.BlockSpec((tm, w), lambda i: (i, 0))
    vec = _const_spec((1, D_MODEL))
    return pl.pallas_call(
        _mix_out_body,
        grid=(t // tm,),
        in_specs=[row(CHUNK_W), row(MLA_W), _const_spec(lw["wo_chunk"].shape), _const_spec(lw["wo_mla"].shape),
                  row(D_MODEL), stream.mod_spec(layer, G_A, tm), stream.mod_spec(layer, SC_F, tm),
                  stream.mod_spec(layer, SH_F, tm), vec, vec, _const_spec(lw["router"].shape)],
        out_specs=[row(D_MODEL), row(D_MODEL), row(LANES), pl.BlockSpec((None, 1, LANES), lambda i: (i, 0, 0))],
        out_shape=[jax.ShapeDtypeStruct((t, D_MODEL), F32), jax.ShapeDtypeStruct((t, D_MODEL), BF16),
                   jax.ShapeDtypeStruct((t, LANES), F32), jax.ShapeDtypeStruct((t // tm, 1, LANES), F32)],
        compiler_params=_params("parallel"),
        name="mix_out",
    )(co, mo, lw["wo_chunk"], lw["wo_mla"], x, mod, mod, mod, lw["ln_mix_g"], lw["ln_mix_b"], lw["router"])


def _mix_ffn_body(co_ref, mo_ref, wo1_ref, wo2_ref, x_ref, ga_ref, scf_ref, shf_ref, gf_ref,
                  gm_ref, bm_ref, g_ref, b_ref, wg_ref, wu_ref, wd_ref, o_ref):
    _, x1 = _mixer_residual(co_ref, mo_ref, wo1_ref, wo2_ref, x_ref, ga_ref, gm_ref, bm_ref)
    x1 = jnp.concatenate(x1, axis=0)
    h = (x1 * (1.0 + scf_ref[...]) + shf_ref[...]).astype(BF16)
    gate = _dot(h, wg_ref[...])
    up = _dot(h, wu_ref[...])
    a = (gate * _sigmoid(gate) * up).astype(BF16)
    f = _dot(a, wd_ref[...])
    o_ref[...] = _layernorm(ALPHA * x1 + gf_ref[...] * f, g_ref[...], b_ref[...])


def _mix_ffn(stream, layer, co, mo, x, mod, lw, tm):
    t = stream.tokens
    row = lambda w: pl.BlockSpec((tm, w), lambda i: (i, 0))
    vec = _const_spec((1, D_MODEL))
    resident = lambda a: pl.BlockSpec(a.shape, lambda i: (0, 0), pipeline_mode=pl.Buffered(1))
    mods = [stream.mod_spec(layer, which, tm) for which in (G_A, SC_F, SH_F, G_F)]
    return pl.pallas_call(
        _mix_ffn_body,
        grid=(t // tm,),
        in_specs=[row(CHUNK_W), row(MLA_W), resident(lw["wo_chunk"]), resident(lw["wo_mla"]), row(D_MODEL),
                  *mods, vec, vec, vec, vec,
                  resident(lw["ffn_gate"]), resident(lw["ffn_up"]), resident(lw["ffn_down"])],
        out_specs=row(D_MODEL),
        out_shape=jax.ShapeDtypeStruct((t, D_MODEL), F32),
        compiler_params=_params("parallel"),
        name="mix_ffn",
    )(co, mo, lw["wo_chunk"], lw["wo_mla"], x, mod, mod, mod, mod,
      lw["ln_mix_g"], lw["ln_mix_b"], lw["ln_ffn_g"], lw["ln_ffn_b"],
      lw["ffn_gate"], lw["ffn_up"], lw["ffn_down"])


RUN_ALIGN = 16
ROUTE_TILE = 512
EXPERT_TILE = 512


def _plan_body(cap, tile, cnt_ref, off_ref, run_ref, base_ref, te_ref, tb_ref, rows_ref, nv_ref):
    nw = cnt_ref.shape[0]
    lane = lax.broadcasted_iota(jnp.int32, (1, LANES), 1).astype(F32)
    run = jnp.floor((cnt_ref[...] + (RUN_ALIGN - 1)) * (1.0 / RUN_ALIGN)) * RUN_ALIGN
    before = jnp.where(lax.broadcasted_iota(jnp.int32, (LANES, LANES), 0)
                       < lax.broadcasted_iota(jnp.int32, (LANES, LANES), 1), 1.0, 0.0).astype(BF16)
    upto = jnp.where(lax.broadcasted_iota(jnp.int32, (LANES, LANES), 0)
                     <= lax.broadcasted_iota(jnp.int32, (LANES, LANES), 1), 1.0, 0.0).astype(BF16)
    above = jnp.where(lax.broadcasted_iota(jnp.int32, (nw, nw), 0)
                      > lax.broadcasted_iota(jnp.int32, (nw, nw), 1), 1.0, 0.0).astype(BF16)
    run_b = run.astype(BF16)
    off_ref[...] = _dot(run_b, before).astype(jnp.int32)
    run_ref[...] = run.astype(jnp.int32)
    base_ref[...] = (lane * cap + _dot(above, run_b)).astype(jnp.int32)
    totals = jnp.sum(run, axis=0, keepdims=True)
    tiles = jnp.floor((totals + (tile - 1)) * (1.0 / tile))
    cum = _dot(tiles.astype(BF16), upto)
    start = cum - tiles
    n_valid = jnp.max(cum, axis=-1, keepdims=True)
    step = jnp.minimum(lane, n_valid - 1.0)
    pick = lambda v, e: jnp.sum(jnp.where(lane == float(e), v, 0.0), axis=-1, keepdims=True)
    te = jnp.zeros_like(step)
    for e in range(N_EXPERTS):
        te = te + jnp.where(pick(cum, e) <= step, 1.0, 0.0)
    start_te, total_te = jnp.zeros_like(step), jnp.zeros_like(step)
    for e in range(N_EXPERTS):
        start_te = jnp.where(te == float(e), pick(start, e), start_te)
        total_te = jnp.where(te == float(e), pick(totals, e), total_te)
    tj = step - start_te
    te_ref[...] = te.astype(jnp.int32)
    tb_ref[...] = (te * (cap // tile) + tj).astype(jnp.int32)
    rows_ref[...] = jnp.minimum(float(tile), total_te - tj * tile).astype(jnp.int32)
    nv_ref[...] = jnp.broadcast_to(n_valid, (1, LANES)).astype(jnp.int32)


def _route_plan(counts, tokens, tm, tile):
    nw = tokens // tm
    cnt = jnp.concatenate([c.reshape(-1, LANES) for c in counts], axis=0)
    worst_pad = (RUN_ALIGN - 1) * nw
    cap = pl.cdiv(tokens + worst_pad, tile) * tile
    n_tiles = (2 * tokens + N_EXPERTS * worst_pad) // tile + N_EXPERTS
    assert n_tiles <= LANES and tm % RUN_ALIGN == 0
    per_run = jax.ShapeDtypeStruct((nw, LANES), jnp.int32)
    per_step = jax.ShapeDtypeStruct((1, LANES), jnp.int32)
    off, run, base, te, tb, rows, nv = pl.pallas_call(
        functools.partial(_plan_body, cap, tile),
        out_shape=[per_run, per_run, per_run, per_step, per_step, per_step, per_step],
        name="moe_plan",
    )(cnt)
    return dict(
        tm=tm, cap=cap, tile=tile, n_tiles=n_tiles, buf_rows=2 * tm + N_EXPERTS * RUN_ALIGN,
        off=off.reshape(-1), run=run.reshape(-1), base=base.reshape(-1),
        tile_expert=te.reshape(-1), tile_block=tb.reshape(-1), tile_rows=rows.reshape(-1),
        n_valid=nv.reshape(-1))


def _run_piece_sizes(tm):
    sizes, b = [], RUN_ALIGN
    while b <= tm:
        sizes.append(b)
        b *= 2
    return sizes


def _row_run_copies(src, dst, src_off, dst_off, n_rows, sem, sizes, action):
    for b in sizes:
        done = n_rows & (-2 * b)
        cp = pltpu.make_async_copy(
            src.at[pl.ds(pl.multiple_of(src_off + done, RUN_ALIGN), b)],
            dst.at[pl.ds(pl.multiple_of(dst_off + done, RUN_ALIGN), b)], sem)

        @pl.when((n_rows & b) != 0)
        def _():
            action(cp)


def _local_rows(meta, off_ref, w):
    i1, i2 = meta[:, 0:1], meta[:, 1:2]
    o1, o2 = jnp.zeros_like(i1), jnp.zeros_like(i2)
    for e in range(N_EXPERTS):
        off_e = off_ref[w * LANES + e].astype(F32)
        o1 = jnp.where(i1 == float(e), off_e, o1)
        o2 = jnp.where(i2 == float(e), off_e, o2)
    return (o1 + meta[:, 4:5]).astype(jnp.int32), (o2 + meta[:, 5:6]).astype(jnp.int32)


def _start(cp):
    cp.start()


def _wait(cp):
    cp.wait()


def _dispatch_body(tm, nw, tile0, off_ref, run_ref, base_ref, h_ref, meta_ref, *refs):
    xs_ref, xw_ref, sems = refs[-3:]
    w = pl.program_id(0)
    slot = w % 2
    sizes = _run_piece_sizes(tm)

    def push(win, buf, action):
        for e in range(N_EXPERTS):
            k = (tile0 + win) * LANES + e
            _row_run_copies(xw_ref.at[buf], xs_ref, off_ref[k], base_ref[k], run_ref[k],
                            sems.at[buf], sizes, action)

    r1, r2 = _local_rows(meta_ref[...], off_ref, tile0 + w)
    col = lax.broadcasted_iota(jnp.int32, (tm, xw_ref.shape[1]), 1)
    sel = jnp.where(col == r1, 1.0, jnp.where(col == r2, 1.0, 0.0)).astype(BF16)
    xw = lax.dot_general(sel, h_ref[...], (((0,), (0,)), ((), ())), preferred_element_type=F32)
    xw_ref[slot] = xw.astype(BF16)
    push(w, slot, _start)

    @pl.when(w > 0)
    def _():
        push(w - 1, 1 - slot, _wait)

    @pl.when(w == nw - 1)
    def _():
        push(w, slot, _wait)


def _dispatch(h2, meta, plan, tile0, xs=None):
    tm = plan["tm"]
    nw = h2.shape[0] // tm
    ins = [plan["off"], plan["run"], plan["base"], h2, meta]
    specs = [pl.BlockSpec((tm, D_MODEL), lambda i, *_: (i, 0)),
             pl.BlockSpec((tm, LANES), lambda i, *_: (i, 0))]
    aliases = {}
    if xs is not None:
        aliases = {len(ins): 0}
        ins.append(xs)
        specs.append(pl.BlockSpec(memory_space=pl.ANY))
    return pl.pallas_call(
        functools.partial(_dispatch_body, tm, nw, tile0),
        grid_spec=pltpu.PrefetchScalarGridSpec(
            num_scalar_prefetch=3,
            grid=(nw,),
            in_specs=specs,
            out_specs=pl.BlockSpec(memory_space=pl.ANY),
            scratch_shapes=[pltpu.VMEM((2, plan["buf_rows"], D_MODEL), BF16),
                            pltpu.SemaphoreType.DMA((2,))]),
        out_shape=jax.ShapeDtypeStruct((N_EXPERTS * plan["cap"], D_MODEL), BF16),
        input_output_aliases=aliases,
        compiler_params=_params("arbitrary"),
        name="moe_dispatch",
    )(*ins)


def _experts_body(te_ref, tb_ref, rows_ref, nv_ref, x_ref, wg_ref, wu_ref, wd_ref, o_ref):
    i = pl.program_id(0)
    rows = rows_ref[i]
    valid = i < nv_ref[0]
    half = x_ref.shape[0] // 2

    def swiglu(n):
        x = x_ref[0:n, :]
        live = lax.broadcasted_iota(jnp.int32, (n, 1), 0) < rows
        x = jnp.where(live, x, jnp.zeros_like(x))
        gate = _dot(x, wg_ref[...])
        up = _dot(x, wu_ref[...])
        a = (gate * _sigmoid(gate) * up).astype(BF16)
        o_ref[0:n, :] = _dot(a, wd_ref[...]).astype(BF16)

    @pl.when(jnp.logical_and(valid, rows > half))
    def _():
        swiglu(2 * half)

    @pl.when(jnp.logical_and(valid, rows <= half))
    def _():
        swiglu(half)


def _experts(xs, plan, lw):
    dff = lw["moe_gate"].shape[2]
    rows = pl.BlockSpec((plan["tile"], D_MODEL), lambda i, te, tb, *_: (tb[i], 0))
    return pl.pallas_call(
        _experts_body,
        grid_spec=pltpu.PrefetchScalarGridSpec(
            num_scalar_prefetch=4,
            grid=(plan["n_tiles"],),
            in_specs=[rows,
                      pl.BlockSpec((None, D_MODEL, dff), lambda i, te, *_: (te[i], 0, 0)),
                      pl.BlockSpec((None, D_MODEL, dff), lambda i, te, *_: (te[i], 0, 0)),
                      pl.BlockSpec((None, dff, D_MODEL), lambda i, te, *_: (te[i], 0, 0))],
            out_specs=rows),
        out_shape=jax.ShapeDtypeStruct(xs.shape, BF16),
        compiler_params=_params("arbitrary"),
        name="moe_experts",
    )(plan["tile_expert"], plan["tile_block"], plan["tile_rows"], plan["n_valid"],
      xs, lw["moe_gate"], lw["moe_up"], lw["moe_down"])


def _combine_body(tm, nw, tile0, off_ref, run_ref, base_ref, ys_ref, meta_ref, x_ref, gf_ref, g_ref,
                  b_ref, o_ref, yw_ref, sems):
    w = pl.program_id(0)
    slot = w % 2
    sizes = _run_piece_sizes(tm)

    def fetch(win, buf, action):
        for e in range(N_EXPERTS):
            k = (tile0 + win) * LANES + e
            _row_run_copies(ys_ref, yw_ref.at[buf], base_ref[k], off_ref[k], run_ref[k],
                            sems.at[buf], sizes, action)

    @pl.when(w == 0)
    def _():
        yw_ref[...] = jnp.zeros_like(yw_ref)
        fetch(0, 0, _start)

    @pl.when(w + 1 < nw)
    def _():
        fetch(w + 1, 1 - slot, _start)

    fetch(w, slot, _wait)
    rp = tm // MIX_PARTS
    parts = [slice(k * rp, (k + 1) * rp) for k in range(MIX_PARTS)]
    meta = [meta_ref[r] for r in parts]
    picks = [_local_rows(v, off_ref, tile0 + w) for v in meta]
    col = lax.broadcasted_iota(jnp.int32, (rp, yw_ref.shape[1]), 1)
    yw = yw_ref[slot]
    gate = [jnp.where(col == r1, v[:, 2:3], jnp.where(col == r2, v[:, 3:4], 0.0)).astype(BF16)
            for v, (r1, r2) in zip(meta, picks)]
    f = [_dot(g, yw) for g in gate]
    for k, r in enumerate(parts):
        o_ref[r] = _layernorm(ALPHA * x_ref[r] + gf_ref[...] * f[k], g_ref[...], b_ref[...])


def _combine(stream, layer, ys, meta, x1, mod, plan, lw, tile0):
    tm = plan["tm"]
    nw = stream.tokens // tm
    row = lambda width: pl.BlockSpec((tm, width), lambda i, *_: (i, 0))
    vec = pl.BlockSpec((1, D_MODEL), lambda i, *_: (0, 0))
    mod_spec = stream.mod_spec(layer, G_F, tm)
    mod_spec = pl.BlockSpec(mod_spec.block_shape, lambda i, *_, f=mod_spec.index_map: f(i))
    return pl.pallas_call(
        functools.partial(_combine_body, tm, nw, tile0),
        grid_spec=pltpu.PrefetchScalarGridSpec(
            num_scalar_prefetch=3,
            grid=(nw,),
            in_specs=[pl.BlockSpec(memory_space=pl.ANY), row(LANES), row(D_MODEL), mod_spec, vec, vec],
            out_specs=row(D_MODEL),
            scratch_shapes=[pltpu.VMEM((2, plan["buf_rows"], D_MODEL), BF16),
                            pltpu.SemaphoreType.DMA((2,))]),
        out_shape=jax.ShapeDtypeStruct((stream.tokens, D_MODEL), F32),
        compiler_params=_params("arbitrary"),
        name="moe_combine",
    )(plan["off"], plan["run"], plan["base"], ys, meta, x1, mod, lw["ln_ffn_g"], lw["ln_ffn_b"])


def _rope_tables(n_tokens):
    rows = n_tokens // GRID_W
    row = np.repeat(np.arange(rows, dtype=np.float64), GRID_W)
    col = np.tile(np.arange(GRID_W, dtype=np.float64), rows)
    half = ROPE_DIM // 2
    inv_freq = ROPE_BASE ** (-np.arange(0, half, 2, dtype=np.float64) / half)
    ar = row[:, None] * inv_freq[None, :]
    ac = col[:, None] * inv_freq[None, :]
    cos = np.concatenate([np.cos(ar), np.cos(ar), np.cos(ac), np.cos(ac)], axis=-1)
    sin = np.concatenate([-np.sin(ar), np.sin(ar), -np.sin(ac), np.sin(ac)], axis=-1)
    return jnp.asarray(np.tile(cos, (1, N_HEADS)), F32), jnp.asarray(np.tile(sin, (1, N_HEADS)), F32)


def _swap_rope_halves(a):
    q = ROPE_DIM // 4
    return jnp.concatenate([a[..., q:2 * q], a[..., :q], a[..., 3 * q:], a[..., 2 * q:3 * q]], axis=-1)


def _layer_weights(l, p):
    w_in = p["w_in"][l]
    o_kr = 2 * CHUNK_W + Q_RANK + KV_RANK
    w_in_ext = jnp.pad(w_in, ((0, 0), (0, 3 * (LANES - ROPE_DIM)))).astype(BF16)
    kr_swapped = _swap_rope_halves(w_in[:, o_kr:o_kr + ROPE_DIM]).astype(BF16)
    w_in_ext = w_in_ext.at[:, o_kr + LANES:o_kr + LANES + ROPE_DIM].set(kr_swapped)

    w_uq = p["w_uq"][l].astype(BF16).reshape(Q_RANK, N_HEADS, QK_NOPE + ROPE_DIM)
    wq_rope = w_uq[:, :, QK_NOPE:]
    wq = jnp.concatenate([w_uq[:, :, :QK_NOPE].reshape(Q_RANK, N_HEADS * QK_NOPE),
                          wq_rope.reshape(Q_RANK, N_HEADS * ROPE_DIM),
                          _swap_rope_halves(wq_rope).reshape(Q_RANK, N_HEADS * ROPE_DIM)], axis=1)

    w_ukv = p["w_ukv"][l].astype(BF16).reshape(KV_RANK, N_HEADS, QK_NOPE + V_DIM)
    wkv = jnp.concatenate([w_ukv[:, :, :QK_NOPE].reshape(KV_RANK, N_HEADS * QK_NOPE),
                           w_ukv[:, :, QK_NOPE:].reshape(KV_RANK, MLA_W)], axis=1)
    w_out = p["w_out"][l]
    lw = {
        "w_in": w_in_ext,
        "q_g": p["q_norm_g"][l].reshape(1, Q_RANK),
        "kv_g": p["kv_norm_g"][l].reshape(1, KV_RANK),
        "wq": wq,
        "wkv": wkv,
        "ln_v_g": p["chunk_ln_g"][l].reshape(1, CHUNK_W),
        "w_s": p["w_spatial"][l].astype(BF16),
        "b_s": jnp.broadcast_to(p["b_spatial"][l][:, :, None], (N_GROUPS, CHUNK, GROUP_W)),
        "wo_chunk": w_out[:CHUNK_W].astype(BF16),
        "wo_mla": w_out[CHUNK_W:].astype(BF16),
        "ln_mix_g": p["ln_mix_g"][l].reshape(1, D_MODEL),
        "ln_mix_b": p["ln_mix_b"][l].reshape(1, D_MODEL),
        "ln_ffn_g": p["ln_ffn_g"][l].reshape(1, D_MODEL),
        "ln_ffn_b": p["ln_ffn_b"][l].reshape(1, D_MODEL),
    }
    if l % 2 == 0:
        lw["ffn_gate"] = p["ffn_w_gate"][l // 2].astype(BF16)
        lw["ffn_up"] = p["ffn_w_up"][l // 2].astype(BF16)
        lw["ffn_down"] = p["ffn_w_down"][l // 2].astype(BF16)
    else:
        r = jnp.pad(p["router_w"][l // 2], ((0, 0), (0, LANES - N_EXPERTS)))
        r_hi = r.astype(BF16)
        lw["router"] = jnp.concatenate([r_hi, (r - r_hi.astype(F32)).astype(BF16)], axis=1)
        lw["moe_gate"] = p["moe_w_gate"][l // 2].astype(BF16)
        lw["moe_up"] = p["moe_w_up"][l // 2].astype(BF16)
        lw["moe_down"] = p["moe_w_down"][l // 2].astype(BF16)
    return lw


def _trunk(streams, xs, mod, weights, rope, ctx):
    caches = None
    for l in range(DEPTH):
        lw = weights[l]
        mixed = []
        for stream, x in zip(streams, xs):
            outs = _mixer_in(stream, l, x, mod, lw, rope, stream.tiles["mixer_in"],
                             None if stream.latent else caches)
            co, q, kcat, vext = outs[:4]
            if not stream.latent:
                caches = outs[4:]
            mixed.append((co, _attend(stream, l, q, kcat, vext, lw, ctx, stream.tiles["attend"])))
        if l % 2 == 0:
            xs = [_mix_ffn(stream, l, co, mo, x, mod, lw, stream.tiles["ffn"])
                  for stream, x, (co, mo) in zip(streams, xs, mixed)]
            continue
        routed = [_mix_out(stream, l, co, mo, x, mod, lw, ROUTE_TILE)
                  for stream, x, (co, mo) in zip(streams, xs, mixed)]
        plan = _route_plan([r[3] for r in routed], sum(s.tokens for s in streams), ROUTE_TILE, EXPERT_TILE)
        tile0, sorted_rows = [], None
        for stream, (_, h2, meta, _) in zip(streams, routed):
            tile0.append(sum(s.tokens for s in streams[:len(tile0)]) // ROUTE_TILE)
            sorted_rows = _dispatch(h2, meta, plan, tile0[-1], sorted_rows)
        ys = _experts(sorted_rows, plan, lw)
        xs = [_combine(stream, l, ys, meta, x1, mod, plan, lw, t0)
              for stream, (x1, _, meta, _), t0 in zip(streams, routed, tile0)]
    return xs, caches


def kernel(x_prompt, x_sample, c, cache_ckv, cache_krope, c_ctx, w_mod, b_mod, w_in, q_norm_g, kv_norm_g, w_uq, w_ukv, chunk_ln_g, w_spatial, b_spatial, w_out, ln_mix_g, ln_mix_b, ln_ffn_g, ln_ffn_b, ffn_w_gate, ffn_w_up, ffn_w_down, router_w, moe_w_gate, moe_w_up, moe_w_down):
    p = dict(w_in=w_in, q_norm_g=q_norm_g, kv_norm_g=kv_norm_g, w_uq=w_uq, w_ukv=w_ukv,
             chunk_ln_g=chunk_ln_g, w_spatial=w_spatial, b_spatial=b_spatial, w_out=w_out,
             ln_mix_g=ln_mix_g, ln_mix_b=ln_mix_b, ln_ffn_g=ln_ffn_g, ln_ffn_b=ln_ffn_b,
             ffn_w_gate=ffn_w_gate, ffn_w_up=ffn_w_up, ffn_w_down=ffn_w_down, router_w=router_w,
             moe_w_gate=moe_w_gate, moe_w_up=moe_w_up, moe_w_down=moe_w_down)
    weights = [_layer_weights(l, p) for l in range(DEPTH)]

    batch, seq, _ = x_prompt.shape
    dec_batch, dec_seq, _ = x_sample.shape
    cond_rows = jnp.concatenate(
        [c_ctx[None, :], c, jnp.zeros((MOD_ROWS - 1 - dec_batch, D_MODEL), F32)], axis=0)
    mod = _modulation(cond_rows, w_mod, b_mod)

    prompt = _Stream(batch, seq, mod_row0=0, per_row_mod=False, latent=False,
                     tiles=dict(mixer_in=512, attend=256, ffn=512))
    sample = _Stream(dec_batch, dec_seq, mod_row0=1, per_row_mod=True, latent=True,
                     tiles=dict(mixer_in=1024, attend=1024, ffn=512))
    rope = _rope_tables(dec_seq)
    ctx = (cache_ckv, jnp.pad(cache_krope, ((0, 0), (0, 0), (0, 0), (0, LANES - ROPE_DIM))))
    (y_prompt, y_sample), (new_ckv, new_krope) = _trunk(
        [prompt, sample],
        [x_prompt.reshape(batch * seq, D_MODEL), x_sample.reshape(dec_batch * dec_seq, D_MODEL)],
        mod, weights, rope, ctx)
    return (y_prompt.reshape(batch, seq, D_MODEL), y_sample.reshape(dec_batch, dec_seq, D_MODEL),
            new_ckv, new_krope)
```

```python
import functools
import math

import jax
import jax.numpy as jnp
import numpy as np
from jax import lax
from jax.experimental import pallas as pl
from jax.experimental.pallas import tpu as pltpu

F32 = jnp.float32
BF16 = jnp.bfloat16

D_MODEL = 1024
DEPTH = 2
GRID_W = 64
CHUNK = 128
N_GROUPS = 4
GROUP_W = 128
CHUNK_W = N_GROUPS * GROUP_W
N_HEADS = 4
QK_NOPE = 128
ROPE_DIM = 64
V_DIM = 128
Q_RANK = 384
KV_RANK = 256
MLA_W = N_HEADS * V_DIM
HEAD_PAD = 256
QK_W = N_HEADS * HEAD_PAD
V_EXT = 2 * V_DIM
VEXT_W = N_HEADS * V_EXT
ROPE_BASE = 10000.0
N_EXPERTS = 8
ALPHA = (2 * DEPTH) ** 0.25
EPS = 1e-6
ATTN_SCALE = math.log2(math.e) / math.sqrt(QK_NOPE + ROPE_DIM)
MOD_ROWS = 16
LANES = 128
V7X_VMEM_BYTES = 64 * 1024 * 1024
VMEM_LIMIT = V7X_VMEM_BYTES * 7 // 8

SH_A, SC_A, G_A, SH_F, SC_F, G_F = range(6)


def _sigmoid(x):
    return 1.0 / (1.0 + jnp.exp(-x))


def _gelu_tanh(x):
    return 0.5 * x * (1.0 + jnp.tanh(math.sqrt(2.0 / math.pi) * (x + 0.044715 * (x * x * x))))


def _layernorm(y, g, b):
    mu = jnp.mean(y, axis=-1, keepdims=True)
    d = y - mu
    var = jnp.mean(d * d, axis=-1, keepdims=True)
    return d * lax.rsqrt(var + EPS) * g + b


def _rmsnorm(y, g):
    return y * lax.rsqrt(jnp.mean(y * y, axis=-1, keepdims=True) + EPS) * g


def _dot(a, b):
    return jnp.dot(a, b, preferred_element_type=F32)


def _dot_nt(a, b):
    return lax.dot_general(a, b, (((1,), (1,)), ((), ())), preferred_element_type=F32)


def _params(*sem):
    return pltpu.CompilerParams(dimension_semantics=sem, vmem_limit_bytes=VMEM_LIMIT)


def _const_spec(shape):
    nd = len(shape)
    return pl.BlockSpec(shape, lambda *_: (0,) * nd)


class _Stream:
    def __init__(self, batch, seq, mod_row0, per_row_mod, latent, tiles):
        self.tiles = tiles
        self.batch = batch
        self.seq = seq
        self.tokens = batch * seq
        self.mod_row0 = mod_row0
        self.per_row_mod = per_row_mod
        self.latent = latent

    def mod_spec(self, layer, which, tm):
        tiles_per_row = self.seq // tm
        row0, per_row = self.mod_row0, self.per_row_mod

        def index(i):
            r = row0 + (i // tiles_per_row if per_row else 0)
            return (layer, r, which, 0, 0)

        return pl.BlockSpec((None, None, None, 1, D_MODEL), index)


def _mod_body(c_ref, w_ref, b_ref, o_ref):
    a = c_ref[...]
    a = (a * _sigmoid(a)).astype(BF16)
    o_ref[:, 0, :] = _dot(a, w_ref[...].astype(BF16)) + b_ref[...]


def _modulation(cond_rows, w_mod, b_mod):
    depth, _, width = w_mod.shape
    n_roles = width // D_MODEL
    return pl.pallas_call(
        _mod_body,
        grid=(depth, n_roles),
        in_specs=[
            pl.BlockSpec((MOD_ROWS, D_MODEL), lambda l, j: (0, 0)),
            pl.BlockSpec((None, D_MODEL, D_MODEL), lambda l, j: (l, 0, j)),
            pl.BlockSpec((None, 1, D_MODEL), lambda l, j: (l, 0, j)),
        ],
        out_specs=pl.BlockSpec((None, MOD_ROWS, None, 1, D_MODEL), lambda l, j: (l, 0, j, 0, 0)),
        out_shape=jax.ShapeDtypeStruct((depth, MOD_ROWS, n_roles, 1, D_MODEL), F32),
        compiler_params=_params("parallel", "parallel"),
        name="modulation",
    )(cond_rows, w_mod, b_mod.reshape(depth, 1, width))


def _store_keys_values(kv, kr, kcat_ref, vext_ref):
    ones = jnp.ones((kv.shape[0], V_EXT - V_DIM), BF16)
    for hd in range(N_HEADS):
        a = hd * HEAD_PAD
        kcat_ref[:, a:a + QK_NOPE] = kv[:, hd * QK_NOPE:(hd + 1) * QK_NOPE].astype(BF16)
        kcat_ref[:, a + QK_NOPE:a + HEAD_PAD] = kr
        b = hd * V_EXT
        vext_ref[:, b:b + V_DIM] = kv[:, (N_HEADS + hd) * V_DIM:(N_HEADS + hd + 1) * V_DIM].astype(BF16)
        vext_ref[:, b + V_DIM:b + V_EXT] = ones


def _mixer_in_body(latent, tm, n_aliased, *refs):
    it = iter(refs)
    x_ref, sc_ref, sh_ref, win_ref, qg_ref, kvg_ref, wq_ref = (next(it) for _ in range(7))
    wkv_ref, lng_ref, ws_ref, bs_ref = (next(it) for _ in range(4))
    cos_ref, sin_ref = (next(it), next(it)) if latent else (None, None)
    for _ in range(n_aliased):
        next(it)
    co_ref, q_ref, kcat_ref, vext_ref = (next(it) for _ in range(4))
    ckv_ref, kr_ref = (None, None) if latent else (next(it), next(it))

    h = (x_ref[...] * (1.0 + sc_ref[...]) + sh_ref[...]).astype(BF16)
    p = _dot(h, win_ref[...])

    for g in range(N_GROUPS):
        cols = slice(g * GROUP_W, (g + 1) * GROUP_W)
        vg = _gelu_tanh(p[:, CHUNK_W + g * GROUP_W:CHUNK_W + (g + 1) * GROUP_W])
        mu = jnp.mean(vg, axis=-1, keepdims=True)
        d = vg - mu
        var = jnp.mean(d * d, axis=-1, keepdims=True)
        vn = (d * lax.rsqrt(var + EPS) * lng_ref[:, cols]).astype(BF16)
        ug = _gelu_tanh(p[:, cols])
        for c in range(tm // CHUNK):
            rows = slice(c * CHUNK, (c + 1) * CHUNK)
            z = _dot(ws_ref[g], vn[rows]) + bs_ref[g]
            co_ref[rows, cols] = (ug[rows] * z).astype(BF16)

    o = 2 * CHUNK_W
    cqn = _rmsnorm(p[:, o:o + Q_RANK], qg_ref[...]).astype(BF16)
    qc = _dot(cqn, wq_ref[...])
    q_rope = qc[:, N_HEADS * QK_NOPE:N_HEADS * (QK_NOPE + ROPE_DIM)]
    if latent:
        cos = cos_ref[...]
        sin = sin_ref[...]
        q_rope = q_rope * cos + qc[:, N_HEADS * (QK_NOPE + ROPE_DIM):] * sin
    for hd in range(N_HEADS):
        a = hd * HEAD_PAD
        q_ref[:, a:a + QK_NOPE] = (qc[:, hd * QK_NOPE:(hd + 1) * QK_NOPE] * ATTN_SCALE).astype(BF16)
        pair = q_rope[:, (hd // 2) * LANES:(hd // 2 + 1) * LANES]
        if hd % 2:
            pair = pltpu.roll(pair, ROPE_DIM, axis=1)
        q_ref[:, a + QK_NOPE:a + HEAD_PAD] = (pair * ATTN_SCALE).astype(BF16)

    o += Q_RANK
    ckvn = _rmsnorm(p[:, o:o + KV_RANK], kvg_ref[...])
    o += KV_RANK
    kr = p[:, o:o + LANES]
    if not latent:
        seq = ckv_ref.shape[1]
        for b in range(ckv_ref.shape[0]):
            ckv_ref[b] = ckvn[b * seq:(b + 1) * seq]
            kr_ref[b] = kr[b * seq:(b + 1) * seq, :ROPE_DIM]
    else:
        kr = kr * cos[:, :LANES] + p[:, o + LANES:o + 2 * LANES] * sin[:, :LANES]
    kr = kr.astype(BF16)
    kv = _dot(ckvn.astype(BF16), wkv_ref[...])
    _store_keys_values(kv, kr, kcat_ref, vext_ref)


def _mixer_in(stream, layer, x, mod, lw, rope, tm, caches=None):
    latent = stream.latent
    t = stream.tokens
    row = lambda w: pl.BlockSpec((tm, w), lambda i: (i, 0))
    win = lw["w_in"]
    win_cols = win.shape[1] if latent else win.shape[1] - LANES
    wq = lw["wq"]
    wq_cols = wq.shape[1] if latent else wq.shape[1] - N_HEADS * ROPE_DIM
    ins = [x, mod, mod, win, lw["q_g"], lw["kv_g"], wq]
    specs = [row(D_MODEL), stream.mod_spec(layer, SC_A, tm), stream.mod_spec(layer, SH_A, tm),
             _const_spec((D_MODEL, win_cols)), _const_spec(lw["q_g"].shape), _const_spec(lw["kv_g"].shape),
             _const_spec((Q_RANK, wq_cols))]
    for name in ("wkv", "ln_v_g", "w_s", "b_s"):
        ins.append(lw[name])
        specs.append(_const_spec(lw[name].shape))
    if latent:
        tiles_per_seq = stream.seq // tm
        pos = pl.BlockSpec((tm, N_HEADS * ROPE_DIM), lambda i: (i % tiles_per_seq, 0))
        ins += [rope[0], rope[1]]
        specs += [pos, pos]
    out_shape = [jax.ShapeDtypeStruct((t, CHUNK_W), BF16), jax.ShapeDtypeStruct((t, QK_W), BF16),
                 jax.ShapeDtypeStruct((t, QK_W), BF16), jax.ShapeDtypeStruct((t, VEXT_W), BF16)]
    out_specs = [row(CHUNK_W), row(QK_W), row(QK_W), row(VEXT_W)]
    aliases = {}
    if not latent:
        for rank in (KV_RANK, ROPE_DIM):
            out_shape.append(jax.ShapeDtypeStruct((stream.batch, DEPTH, stream.seq, rank), F32))
            out_specs.append(pl.BlockSpec((tm // stream.seq, None, stream.seq, rank),
                                          lambda i: (i, layer, 0, 0)))
        for k, buf in enumerate(caches or ()):
            aliases[len(ins)] = len(out_shape) - 2 + k
            ins.append(buf)
            specs.append(pl.BlockSpec(memory_space=pl.ANY))
    return pl.pallas_call(
        functools.partial(_mixer_in_body, latent, tm, len(aliases)),
        grid=(t // tm,),
        in_specs=specs,
        out_specs=out_specs,
        out_shape=out_shape,
        input_output_aliases=aliases,
        compiler_params=_params("parallel"),
        name="mixer_in",
    )(*ins)


KEY_CHUNK = 256


def _attend_body(has_ctx, *refs):
    if has_ctx:
        q_ref, k_ref, vext_ref, cckv_ref, ckr_ref, wkv_ref, o_ref, kctx_ref, vctx_ref = refs

        @pl.when(pl.program_id(1) == 0)
        def _():
            kv = _dot(cckv_ref[...].astype(BF16), wkv_ref[...])
            _store_keys_values(kv, ckr_ref[...].astype(BF16), kctx_ref, vctx_ref)

        sources = [(kctx_ref, vctx_ref), (k_ref, vext_ref)]
    else:
        q_ref, k_ref, vext_ref, o_ref = refs
        sources = [(k_ref, vext_ref)]
    chunks = [(kr, vr, slice(c * KEY_CHUNK, (c + 1) * KEY_CHUNK))
              for kr, vr in sources for c in range(kr.shape[0] // KEY_CHUNK)]

    def scores(hd):
        qk = slice(hd * HEAD_PAD, (hd + 1) * HEAD_PAD)
        return [_dot_nt(q_ref[:, qk], kr[rows, qk]) for kr, _, rows in chunks]

    s = scores(0)
    for hd in range(N_HEADS):
        m = jnp.max(functools.reduce(jnp.maximum, s), axis=-1, keepdims=True)
        s_next, acc = [], None
        for c, (kr, vr, rows) in enumerate(chunks):
            if hd + 1 < N_HEADS:
                qk = slice((hd + 1) * HEAD_PAD, (hd + 2) * HEAD_PAD)
                s_next.append(_dot_nt(q_ref[:, qk], kr[rows, qk]))
            p = jnp.exp2(s[c] - m).astype(BF16)
            part = _dot(p, vr[rows, hd * V_EXT:(hd + 1) * V_EXT])
            acc = part if acc is None else acc + part
        s = s_next
        o_ref[:, hd * V_DIM:(hd + 1) * V_DIM] = (acc[:, :V_DIM] / acc[:, V_DIM:]).astype(BF16)


def _attend(stream, layer, q, kcat, vext, lw, ctx, tq):
    n = stream.seq
    nq = n // tq
    has_ctx = stream.latent
    ins = [q, kcat, vext]
    specs = [pl.BlockSpec((tq, QK_W), lambda b, i: (b * nq + i, 0)),
             pl.BlockSpec((n, QK_W), lambda b, i: (b, 0)),
             pl.BlockSpec((n, VEXT_W), lambda b, i: (b, 0))]
    scratch = []
    if has_ctx:
        cache_ckv, cache_kr = ctx
        past = cache_ckv.shape[2]
        ins += [cache_ckv, cache_kr, lw["wkv"]]
        specs += [pl.BlockSpec((None, None, past, KV_RANK), lambda b, i: (b, layer, 0, 0)),
                  pl.BlockSpec((None, None, past, LANES), lambda b, i: (b, layer, 0, 0)),
                  _const_spec(lw["wkv"].shape)]
        scratch = [pltpu.VMEM((past, QK_W), BF16), pltpu.VMEM((past, VEXT_W), BF16)]
    return pl.pallas_call(
        functools.partial(_attend_body, has_ctx),
        grid=(stream.batch, nq),
        in_specs=specs,
        out_specs=pl.BlockSpec((tq, MLA_W), lambda b, i: (b * nq + i, 0)),
        out_shape=jax.ShapeDtypeStruct((stream.tokens, MLA_W), BF16),
        scratch_shapes=scratch,
        compiler_params=_params("parallel", "arbitrary"),
        name="attend",
    )(*ins)


MIX_PARTS = 4


def _mixer_residual(co_ref, mo_ref, wo1_ref, wo2_ref, x_ref, ga_ref, g_ref, b_ref):
    rp = x_ref.shape[0] // MIX_PARTS
    parts = [slice(k * rp, (k + 1) * rp) for k in range(MIX_PARTS)]
    mix = [_dot(co_ref[r], wo1_ref[...]) + _dot(mo_ref[r], wo2_ref[...]) for r in parts]
    x1 = [_layernorm(ALPHA * x_ref[r] + ga_ref[...] * mix[k], g_ref[...], b_ref[...])
          for k, r in enumerate(parts)]
    return parts, x1


def _mix_out_body(co_ref, mo_ref, wo1_ref, wo2_ref, x_ref, ga_ref, scf_ref, shf_ref, g_ref, b_ref,
                  rcat_ref, x1_ref, h2_ref, meta_ref, cnt_ref):
    parts, x1 = _mixer_residual(co_ref, mo_ref, wo1_ref, wo2_ref, x_ref, ga_ref, g_ref, b_ref)
    rp = x_ref.shape[0] // MIX_PARTS
    for k, r in enumerate(parts):
        x1_ref[r] = x1[k]
    h2 = [v * (1.0 + scf_ref[...]) + shf_ref[...] for v in x1]
    hh = [v.astype(BF16) for v in h2]
    for k, r in enumerate(parts):
        h2_ref[r] = hh[k]
    hl = [(h2[k] - hh[k].astype(F32)).astype(BF16) for k in range(MIX_PARTS)]
    ra = [_dot(v, rcat_ref[...]) for v in hh]
    rb = [_dot(v, rcat_ref[...]) for v in hl]
    logits = [ra[k][:, :LANES] + (ra[k][:, LANES:] + (rb[k][:, :LANES] + rb[k][:, LANES:]))
              for k in range(MIX_PARTS)]
    lane = lax.broadcasted_iota(jnp.int32, (rp, LANES), 1).astype(F32)
    neg = -jnp.inf
    lg = [jnp.where(lane < N_EXPERTS, v, neg) for v in logits]
    m1 = [jnp.max(v, axis=-1, keepdims=True) for v in lg]
    i1 = [jnp.min(jnp.where(lg[k] == m1[k], lane, float(LANES)), axis=-1, keepdims=True)
          for k in range(MIX_PARTS)]
    lg2 = [jnp.where(lane == i1[k], neg, lg[k]) for k in range(MIX_PARTS)]
    m2 = [jnp.max(v, axis=-1, keepdims=True) for v in lg2]
    i2 = [jnp.min(jnp.where(lg2[k] == m2[k], lane, float(LANES)), axis=-1, keepdims=True)
          for k in range(MIX_PARTS)]
    picked = [jnp.where(lane == i1[k], 1.0, jnp.where(lane == i2[k], 1.0, 0.0)) for k in range(MIX_PARTS)]
    earlier = jnp.where(lax.broadcasted_iota(jnp.int32, (rp, rp), 0)
                        > lax.broadcasted_iota(jnp.int32, (rp, rp), 1), 1.0, 0.0).astype(BF16)
    before = jnp.zeros((1, LANES), F32)
    for k, r in enumerate(parts):
        rank = _dot(earlier, picked[k].astype(BF16)) + before
        before = before + jnp.sum(picked[k], axis=0, keepdims=True)
        rank1 = jnp.sum(jnp.where(lane == i1[k], rank, 0.0), axis=-1, keepdims=True)
        rank2 = jnp.sum(jnp.where(lane == i2[k], rank, 0.0), axis=-1, keepdims=True)
        e2 = jnp.exp(m2[k] - m1[k])
        den = 1.0 + e2
        meta = jnp.zeros((rp, LANES), F32)
        for j, val in enumerate((i1[k], i2[k], 1.0 / den, e2 / den, rank1, rank2)):
            meta = jnp.where(lane == float(j), val, meta)
        meta_ref[r] = meta
    cnt_ref[...] = before


def _mix_out(stream, layer, co, mo, x, mod, lw, tm):
    t = stream.tokens
    row = lambda w: pl.BlockSpec((tm, w), lambda i: (i, 0))
    vec = _const_spec((1, D_MODEL))
    return pl.pallas_call(
        _mix_out_body,
        grid=(t // tm,),
        in_specs=[row(CHUNK_W), row(MLA_W), _const_spec(lw["wo_chunk"].shape), _const_spec(lw["wo_mla"].shape),
                  row(D_MODEL), stream.mod_spec(layer, G_A, tm), stream.mod_spec(layer, SC_F, tm),
                  stream.mod_spec(layer, SH_F, tm), vec, vec, _const_spec(lw["router"].shape)],
        out_specs=[row(D_MODEL), row(D_MODEL), row(LANES), pl.BlockSpec((None, 1, LANES), lambda i: (i, 0, 0))],
        out_shape=[jax.ShapeDtypeStruct((t, D_MODEL), F32), jax.ShapeDtypeStruct((t, D_MODEL), BF16),
                   jax.ShapeDtypeStruct((t, LANES), F32), jax.ShapeDtypeStruct((t // tm, 1, LANES), F32)],
        compiler_params=_params("parallel"),
        name="mix_out",
    )(co, mo, lw["wo_chunk"], lw["wo_mla"], x, mod, mod, mod, lw["ln_mix_g"], lw["ln_mix_b"], lw["router"])


def _mix_ffn_body(co_ref, mo_ref, wo1_ref, wo2_ref, x_ref, ga_ref, scf_ref, shf_ref, gf_ref,
                  gm_ref, bm_ref, g_ref, b_ref, wg_ref, wu_ref, wd_ref, o_ref):
    _, x1 = _mixer_residual(co_ref, mo_ref, wo1_ref, wo2_ref, x_ref, ga_ref, gm_ref, bm_ref)
    x1 = jnp.concatenate(x1, axis=0)
    h = (x1 * (1.0 + scf_ref[...]) + shf_ref[...]).astype(BF16)
    gate = _dot(h, wg_ref[...])
    up = _dot(h, wu_ref[...])
    a = (gate * _sigmoid(gate) * up).astype(BF16)
    f = _dot(a, wd_ref[...])
    o_ref[...] = _layernorm(ALPHA * x1 + gf_ref[...] * f, g_ref[...], b_ref[...])


def _mix_ffn(stream, layer, co, mo, x, mod, lw, tm):
    t = stream.tokens
    row = lambda w: pl.BlockSpec((tm, w), lambda i: (i, 0))
    vec = _const_spec((1, D_MODEL))
    resident = lambda a: pl.BlockSpec(a.shape, lambda i: (0, 0), pipeline_mode=pl.Buffered(1))
    mods = [stream.mod_spec(layer, which, tm) for which in (G_A, SC_F, SH_F, G_F)]
    return pl.pallas_call(
        _mix_ffn_body,
        grid=(t // tm,),
        in_specs=[row(CHUNK_W), row(MLA_W), resident(lw["wo_chunk"]), resident(lw["wo_mla"]), row(D_MODEL),
                  *mods, vec, vec, vec, vec,
                  resident(lw["ffn_gate"]), resident(lw["ffn_up"]), resident(lw["ffn_down"])],
        out_specs=row(D_MODEL),
        out_shape=jax.ShapeDtypeStruct((t, D_MODEL), F32),
        compiler_params=_params("parallel"),
        name="mix_ffn",
    )(co, mo, lw["wo_chunk"], lw["wo_mla"], x, mod, mod, mod, mod,
      lw["ln_mix_g"], lw["ln_mix_b"], lw["ln_ffn_g"], lw["ln_ffn_b"],
      lw["ffn_gate"], lw["ffn_up"], lw["ffn_down"])


RUN_ALIGN = 16
ROUTE_TILE = 512
EXPERT_TILE = 512


def _plan_body(cap, tile, cnt_ref, off_ref, run_ref, base_ref, te_ref, tb_ref, rows_ref, nv_ref):
    nw = cnt_ref.shape[0]
    lane = lax.broadcasted_iota(jnp.int32, (1, LANES), 1).astype(F32)
    run = jnp.floor((cnt_ref[...] + (RUN_ALIGN - 1)) * (1.0 / RUN_ALIGN)) * RUN_ALIGN
    before = jnp.where(lax.broadcasted_iota(jnp.int32, (LANES, LANES), 0)
                       < lax.broadcasted_iota(jnp.int32, (LANES, LANES), 1), 1.0, 0.0).astype(BF16)
    upto = jnp.where(lax.broadcasted_iota(jnp.int32, (LANES, LANES), 0)
                     <= lax.broadcasted_iota(jnp.int32, (LANES, LANES), 1), 1.0, 0.0).astype(BF16)
    above = jnp.where(lax.broadcasted_iota(jnp.int32, (nw, nw), 0)
                      > lax.broadcasted_iota(jnp.int32, (nw, nw), 1), 1.0, 0.0).astype(BF16)
    run_b = run.astype(BF16)
    off_ref[...] = _dot(run_b, before).astype(jnp.int32)
    run_ref[...] = run.astype(jnp.int32)
    base_ref[...] = (lane * cap + _dot(above, run_b)).astype(jnp.int32)
    totals = jnp.sum(run, axis=0, keepdims=True)
    tiles = jnp.floor((totals + (tile - 1)) * (1.0 / tile))
    cum = _dot(tiles.astype(BF16), upto)
    start = cum - tiles
    n_valid = jnp.max(cum, axis=-1, keepdims=True)
    step = jnp.minimum(lane, n_valid - 1.0)
    pick = lambda v, e: jnp.sum(jnp.where(lane == float(e), v, 0.0), axis=-1, keepdims=True)
    te = jnp.zeros_like(step)
    for e in range(N_EXPERTS):
        te = te + jnp.where(pick(cum, e) <= step, 1.0, 0.0)
    start_te, total_te = jnp.zeros_like(step), jnp.zeros_like(step)
    for e in range(N_EXPERTS):
        start_te = jnp.where(te == float(e), pick(start, e), start_te)
        total_te = jnp.where(te == float(e), pick(totals, e), total_te)
    tj = step - start_te
    te_ref[...] = te.astype(jnp.int32)
    tb_ref[...] = (te * (cap // tile) + tj).astype(jnp.int32)
    rows_ref[...] = jnp.minimum(float(tile), total_te - tj * tile).astype(jnp.int32)
    nv_ref[...] = jnp.broadcast_to(n_valid, (1, LANES)).astype(jnp.int32)


def _route_plan(counts, tokens, tm, tile):
    nw = tokens // tm
    cnt = jnp.concatenate([c.reshape(-1, LANES) for c in counts], axis=0)
    worst_pad = (RUN_ALIGN - 1) * nw
    cap = pl.cdiv(tokens + worst_pad, tile) * tile
    n_tiles = (2 * tokens + N_EXPERTS * worst_pad) // tile + N_EXPERTS
    assert n_tiles <= LANES and tm % RUN_ALIGN == 0
    per_run = jax.ShapeDtypeStruct((nw, LANES), jnp.int32)
    per_step = jax.ShapeDtypeStruct((1, LANES), jnp.int32)
    off, run, base, te, tb, rows, nv = pl.pallas_call(
        functools.partial(_plan_body, cap, tile),
        out_shape=[per_run, per_run, per_run, per_step, per_step, per_step, per_step],
        name="moe_plan",
    )(cnt)
    return dict(
        tm=tm, cap=cap, tile=tile, n_tiles=n_tiles, buf_rows=2 * tm + N_EXPERTS * RUN_ALIGN,
        off=off.reshape(-1), run=run.reshape(-1), base=base.reshape(-1),
        tile_expert=te.reshape(-1), tile_block=tb.reshape(-1), tile_rows=rows.reshape(-1),
        n_valid=nv.reshape(-1))


def _run_piece_sizes(tm):
    sizes, b = [], RUN_ALIGN
    while b <= tm:
        sizes.append(b)
        b *= 2
    return sizes


def _row_run_copies(src, dst, src_off, dst_off, n_rows, sem, sizes, action):
    for b in sizes:
        done = n_rows & (-2 * b)
        cp = pltpu.make_async_copy(
            src.at[pl.ds(pl.multiple_of(src_off + done, RUN_ALIGN), b)],
            dst.at[pl.ds(pl.multiple_of(dst_off + done, RUN_ALIGN), b)], sem)

        @pl.when((n_rows & b) != 0)
        def _():
            action(cp)


def _local_rows(meta, off_ref, w):
    i1, i2 = meta[:, 0:1], meta[:, 1:2]
    o1, o2 = jnp.zeros_like(i1), jnp.zeros_like(i2)
    for e in range(N_EXPERTS):
        off_e = off_ref[w * LANES + e].astype(F32)
        o1 = jnp.where(i1 == float(e), off_e, o1)
        o2 = jnp.where(i2 == float(e), off_e, o2)
    return (o1 + meta[:, 4:5]).astype(jnp.int32), (o2 + meta[:, 5:6]).astype(jnp.int32)


def _start(cp):
    cp.start()


def _wait(cp):
    cp.wait()


def _dispatch_body(tm, nw, tile0, off_ref, run_ref, base_ref, h_ref, meta_ref, *refs):
    xs_ref, xw_ref, sems = refs[-3:]
    w = pl.program_id(0)
    slot = w % 2
    sizes = _run_piece_sizes(tm)

    def push(win, buf, action):
        for e in range(N_EXPERTS):
            k = (tile0 + win) * LANES + e
            _row_run_copies(xw_ref.at[buf], xs_ref, off_ref[k], base_ref[k], run_ref[k],
                            sems.at[buf], sizes, action)

    r1, r2 = _local_rows(meta_ref[...], off_ref, tile0 + w)
    col = lax.broadcasted_iota(jnp.int32, (tm, xw_ref.shape[1]), 1)
    sel = jnp.where(col == r1, 1.0, jnp.where(col == r2, 1.0, 0.0)).astype(BF16)
    xw = lax.dot_general(sel, h_ref[...], (((0,), (0,)), ((), ())), preferred_element_type=F32)
    xw_ref[slot] = xw.astype(BF16)
    push(w, slot, _start)

    @pl.when(w > 0)
    def _():
        push(w - 1, 1 - slot, _wait)

    @pl.when(w == nw - 1)
    def _():
        push(w, slot, _wait)


def _dispatch(h2, meta, plan, tile0, xs=None):
    tm = plan["tm"]
    nw = h2.shape[0] // tm
    ins = [plan["off"], plan["run"], plan["base"], h2, meta]
    specs = [pl.BlockSpec((tm, D_MODEL), lambda i, *_: (i, 0)),
             pl.BlockSpec((tm, LANES), lambda i, *_: (i, 0))]
    aliases = {}
    if xs is not None:
        aliases = {len(ins): 0}
        ins.append(xs)
        specs.append(pl.BlockSpec(memory_space=pl.ANY))
    return pl.pallas_call(
        functools.partial(_dispatch_body, tm, nw, tile0),
        grid_spec=pltpu.PrefetchScalarGridSpec(
            num_scalar_prefetch=3,
            grid=(nw,),
            in_specs=specs,
            out_specs=pl.BlockSpec(memory_space=pl.ANY),
            scratch_shapes=[pltpu.VMEM((2, plan["buf_rows"], D_MODEL), BF16),
                            pltpu.SemaphoreType.DMA((2,))]),
        out_shape=jax.ShapeDtypeStruct((N_EXPERTS * plan["cap"], D_MODEL), BF16),
        input_output_aliases=aliases,
        compiler_params=_params("arbitrary"),
        name="moe_dispatch",
    )(*ins)


def _experts_body(te_ref, tb_ref, rows_ref, nv_ref, x_ref, wg_ref, wu_ref, wd_ref, o_ref):
    i = pl.program_id(0)
    rows = rows_ref[i]
    valid = i < nv_ref[0]
    half = x_ref.shape[0] // 2

    def swiglu(n):
        x = x_ref[0:n, :]
        live = lax.broadcasted_iota(jnp.int32, (n, 1), 0) < rows
        x = jnp.where(live, x, jnp.zeros_like(x))
        gate = _dot(x, wg_ref[...])
        up = _dot(x, wu_ref[...])
        a = (gate * _sigmoid(gate) * up).astype(BF16)
        o_ref[0:n, :] = _dot(a, wd_ref[...]).astype(BF16)

    @pl.when(jnp.logical_and(valid, rows > half))
    def _():
        swiglu(2 * half)

    @pl.when(jnp.logical_and(valid, rows <= half))
    def _():
        swiglu(half)


def _experts(xs, plan, lw):
    dff = lw["moe_gate"].shape[2]
    rows = pl.BlockSpec((plan["tile"], D_MODEL), lambda i, te, tb, *_: (tb[i], 0))
    return pl.pallas_call(
        _experts_body,
        grid_spec=pltpu.PrefetchScalarGridSpec(
            num_scalar_prefetch=4,
            grid=(plan["n_tiles"],),
            in_specs=[rows,
                      pl.BlockSpec((None, D_MODEL, dff), lambda i, te, *_: (te[i], 0, 0)),
                      pl.BlockSpec((None, D_MODEL, dff), lambda i, te, *_: (te[i], 0, 0)),
                      pl.BlockSpec((None, dff, D_MODEL), lambda i, te, *_: (te[i], 0, 0))],
            out_specs=rows),
        out_shape=jax.ShapeDtypeStruct(xs.shape, BF16),
        compiler_params=_params("arbitrary"),
        name="moe_experts",
    )(plan["tile_expert"], plan["tile_block"], plan["tile_rows"], plan["n_valid"],
      xs, lw["moe_gate"], lw["moe_up"], lw["moe_down"])


def _combine_body(tm, nw, tile0, off_ref, run_ref, base_ref, ys_ref, meta_ref, x_ref, gf_ref, g_ref,
                  b_ref, o_ref, yw_ref, sems):
    w = pl.program_id(0)
    slot = w % 2
    sizes = _run_piece_sizes(tm)

    def fetch(win, buf, action):
        for e in range(N_EXPERTS):
            k = (tile0 + win) * LANES + e
            _row_run_copies(ys_ref, yw_ref.at[buf], base_ref[k], off_ref[k], run_ref[k],
                            sems.at[buf], sizes, action)

    @pl.when(w == 0)
    def _():
        yw_ref[...] = jnp.zeros_like(yw_ref)
        fetch(0, 0, _start)

    @pl.when(w + 1 < nw)
    def _():
        fetch(w + 1, 1 - slot, _start)

    fetch(w, slot, _wait)
    rp = tm // MIX_PARTS
    parts = [slice(k * rp, (k + 1) * rp) for k in range(MIX_PARTS)]
    meta = [meta_ref[r] for r in parts]
    picks = [_local_rows(v, off_ref, tile0 + w) for v in meta]
    col = lax.broadcasted_iota(jnp.int32, (rp, yw_ref.shape[1]), 1)
    yw = yw_ref[slot]
    gate = [jnp.where(col == r1, v[:, 2:3], jnp.where(col == r2, v[:, 3:4], 0.0)).astype(BF16)
            for v, (r1, r2) in zip(meta, picks)]
    f = [_dot(g, yw) for g in gate]
    for k, r in enumerate(parts):
        o_ref[r] = _layernorm(ALPHA * x_ref[r] + gf_ref[...] * f[k], g_ref[...], b_ref[...])


def _combine(stream, layer, ys, meta, x1, mod, plan, lw, tile0):
    tm = plan["tm"]
    nw = stream.tokens // tm
    row = lambda width: pl.BlockSpec((tm, width), lambda i, *_: (i, 0))
    vec = pl.BlockSpec((1, D_MODEL), lambda i, *_: (0, 0))
    mod_spec = stream.mod_spec(layer, G_F, tm)
    mod_spec = pl.BlockSpec(mod_spec.block_shape, lambda i, *_, f=mod_spec.index_map: f(i))
    return pl.pallas_call(
        functools.partial(_combine_body, tm, nw, tile0),
        grid_spec=pltpu.PrefetchScalarGridSpec(
            num_scalar_prefetch=3,
            grid=(nw,),
            in_specs=[pl.BlockSpec(memory_space=pl.ANY), row(LANES), row(D_MODEL), mod_spec, vec, vec],
            out_specs=row(D_MODEL),
            scratch_shapes=[pltpu.VMEM((2, plan["buf_rows"], D_MODEL), BF16),
                            pltpu.SemaphoreType.DMA((2,))]),
        out_shape=jax.ShapeDtypeStruct((stream.tokens, D_MODEL), F32),
        compiler_params=_params("arbitrary"),
        name="moe_combine",
    )(plan["off"], plan["run"], plan["base"], ys, meta, x1, mod, lw["ln_ffn_g"], lw["ln_ffn_b"])


def _rope_tables(n_tokens):
    rows = n_tokens // GRID_W
    row = np.repeat(np.arange(rows, dtype=np.float64), GRID_W)
    col = np.tile(np.arange(GRID_W, dtype=np.float64), rows)
    half = ROPE_DIM // 2
    inv_freq = ROPE_BASE ** (-np.arange(0, half, 2, dtype=np.float64) / half)
    ar = row[:, None] * inv_freq[None, :]
    ac = col[:, None] * inv_freq[None, :]
    cos = np.concatenate([np.cos(ar), np.cos(ar), np.cos(ac), np.cos(ac)], axis=-1)
    sin = np.concatenate([-np.sin(ar), np.sin(ar), -np.sin(ac), np.sin(ac)], axis=-1)
    return jnp.asarray(np.tile(cos, (1, N_HEADS)), F32), jnp.asarray(np.tile(sin, (1, N_HEADS)), F32)


def _swap_rope_halves(a):
    q = ROPE_DIM // 4
    return jnp.concatenate([a[..., q:2 * q], a[..., :q], a[..., 3 * q:], a[..., 2 * q:3 * q]], axis=-1)


def _layer_weights(l, p):
    w_in = p["w_in"][l].astype(BF16)
    o_kr = 2 * CHUNK_W + Q_RANK + KV_RANK
    zpad = jnp.zeros((D_MODEL, LANES - ROPE_DIM), BF16)
    w_in_ext = jnp.concatenate(
        [w_in, zpad, _swap_rope_halves(w_in[:, o_kr:o_kr + ROPE_DIM]), zpad], axis=1)

    w_uq = p["w_uq"][l].astype(BF16).reshape(Q_RANK, N_HEADS, QK_NOPE + ROPE_DIM)
    wq_rope = w_uq[:, :, QK_NOPE:]
    wq = jnp.concatenate([w_uq[:, :, :QK_NOPE].reshape(Q_RANK, N_HEADS * QK_NOPE),
                          wq_rope.reshape(Q_RANK, N_HEADS * ROPE_DIM),
                          _swap_rope_halves(wq_rope).reshape(Q_RANK, N_HEADS * ROPE_DIM)], axis=1)

    w_ukv = p["w_ukv"][l].astype(BF16).reshape(KV_RANK, N_HEADS, QK_NOPE + V_DIM)
    wkv = jnp.concatenate([w_ukv[:, :, :QK_NOPE].reshape(KV_RANK, N_HEADS * QK_NOPE),
                           w_ukv[:, :, QK_NOPE:].reshape(KV_RANK, MLA_W)], axis=1)
    w_out = p["w_out"][l]
    lw = {
        "w_in": w_in_ext,
        "q_g": p["q_norm_g"][l].reshape(1, Q_RANK),
        "kv_g": p["kv_norm_g"][l].reshape(1, KV_RANK),
        "wq": wq,
        "wkv": wkv,
        "ln_v_g": p["chunk_ln_g"][l].reshape(1, CHUNK_W),
        "w_s": p["w_spatial"][l].astype(BF16),
        "b_s": jnp.broadcast_to(p["b_spatial"][l][:, :, None], (N_GROUPS, CHUNK, GROUP_W)),
        "wo_chunk": w_out[:CHUNK_W].astype(BF16),
        "wo_mla": w_out[CHUNK_W:].astype(BF16),
        "ln_mix_g": p["ln_mix_g"][l].reshape(1, D_MODEL),
        "ln_mix_b": p["ln_mix_b"][l].reshape(1, D_MODEL),
        "ln_ffn_g": p["ln_ffn_g"][l].reshape(1, D_MODEL),
        "ln_ffn_b": p["ln_ffn_b"][l].reshape(1, D_MODEL),
    }
    if l % 2 == 0:
        lw["ffn_gate"] = p["ffn_w_gate"][l // 2].astype(BF16)
        lw["ffn_up"] = p["ffn_w_up"][l // 2].astype(BF16)
        lw["ffn_down"] = p["ffn_w_down"][l // 2].astype(BF16)
    else:
        r = jnp.pad(p["router_w"][l // 2], ((0, 0), (0, LANES - N_EXPERTS)))
        r_hi = r.astype(BF16)
        lw["router"] = jnp.concatenate([r_hi, (r - r_hi.astype(F32)).astype(BF16)], axis=1)
        lw["moe_gate"] = p["moe_w_gate"][l // 2].astype(BF16)
        lw["moe_up"] = p["moe_w_up"][l // 2].astype(BF16)
        lw["moe_down"] = p["moe_w_down"][l // 2].astype(BF16)
    return lw


def _trunk(streams, xs, mod, weights, rope, ctx):
    caches = None
    for l in range(DEPTH):
        lw = weights[l]
        mixed = []
        for stream, x in zip(streams, xs):
            outs = _mixer_in(stream, l, x, mod, lw, rope, stream.tiles["mixer_in"],
                             None if stream.latent else caches)
            co, q, kcat, vext = outs[:4]
            if not stream.latent:
                caches = outs[4:]
            mixed.append((co, _attend(stream, l, q, kcat, vext, lw, ctx, stream.tiles["attend"])))
        if l % 2 == 0:
            xs = [_mix_ffn(stream, l, co, mo, x, mod, lw, stream.tiles["ffn"])
                  for stream, x, (co, mo) in zip(streams, xs, mixed)]
            continue
        routed = [_mix_out(stream, l, co, mo, x, mod, lw, ROUTE_TILE)
                  for stream, x, (co, mo) in zip(streams, xs, mixed)]
        plan = _route_plan([r[3] for r in routed], sum(s.tokens for s in streams), ROUTE_TILE, EXPERT_TILE)
        tile0, sorted_rows = [], None
        for stream, (_, h2, meta, _) in zip(streams, routed):
            tile0.append(sum(s.tokens for s in streams[:len(tile0)]) // ROUTE_TILE)
            sorted_rows = _dispatch(h2, meta, plan, tile0[-1], sorted_rows)
        ys = _experts(sorted_rows, plan, lw)
        xs = [_combine(stream, l, ys, meta, x1, mod, plan, lw, t0)
              for stream, (x1, _, meta, _), t0 in zip(streams, routed, tile0)]
    return xs, caches


def kernel(x_prompt, x_sample, c, cache_ckv, cache_krope, c_ctx, w_mod, b_mod, w_in, q_norm_g, kv_norm_g, w_uq, w_ukv, chunk_ln_g, w_spatial, b_spatial, w_out, ln_mix_g, ln_mix_b, ln_ffn_g, ln_ffn_b, ffn_w_gate, ffn_w_up, ffn_w_down, router_w, moe_w_gate, moe_w_up, moe_w_down):
    p = dict(w_in=w_in, q_norm_g=q_norm_g, kv_norm_g=kv_norm_g, w_uq=w_uq, w_ukv=w_ukv,
             chunk_ln_g=chunk_ln_g, w_spatial=w_spatial, b_spatial=b_spatial, w_out=w_out,
             ln_mix_g=ln_mix_g, ln_mix_b=ln_mix_b, ln_ffn_g=ln_ffn_g, ln_ffn_b=ln_ffn_b,
             ffn_w_gate=ffn_w_gate, ffn_w_up=ffn_w_up, ffn_w_down=ffn_w_down, router_w=router_w,
             moe_w_gate=moe_w_gate, moe_w_up=moe_w_up, moe_w_down=moe_w_down)
    weights = [_layer_weights(l, p) for l in range(DEPTH)]

    batch, seq, _ = x_prompt.shape
    dec_batch, dec_seq, _ = x_sample.shape
    cond_rows = jnp.concatenate(
        [c_ctx[None, :], c, jnp.zeros((MOD_ROWS - 1 - dec_batch, D_MODEL), F32)], axis=0)
    mod = _modulation(cond_rows, w_mod, b_mod)

    prompt = _Stream(batch, seq, mod_row0=0, per_row_mod=False, latent=False,
                     tiles=dict(mixer_in=512, attend=256, ffn=512))
    sample = _Stream(dec_batch, dec_seq, mod_row0=1, per_row_mod=True, latent=True,
                     tiles=dict(mixer_in=1024, attend=1024, ffn=512))
    rope = _rope_tables(dec_seq)
    ctx = (cache_ckv, jnp.pad(cache_krope, ((0, 0), (0, 0), (0, 0), (0, LANES - ROPE_DIM))))
    (y_prompt, y_sample), (new_ckv, new_krope) = _trunk(
        [prompt, sample],
        [x_prompt.reshape(batch * seq, D_MODEL), x_sample.reshape(dec_batch * dec_seq, D_MODEL)],
        mod, weights, rope, ctx)
    return (y_prompt.reshape(batch, seq, D_MODEL), y_sample.reshape(dec_batch, dec_seq, D_MODEL),
            new_ckv, new_krope)
```

```python
import functools
import math

import jax
import jax.numpy as jnp
import numpy as np
from jax import lax
from jax.experimental import pallas as pl
from jax.experimental.pallas import tpu as pltpu

F32 = jnp.float32
BF16 = jnp.bfloat16

D_MODEL = 1024
DEPTH = 2
GRID_W = 64
CHUNK = 128
N_GROUPS = 4
GROUP_W = 128
CHUNK_W = N_GROUPS * GROUP_W
N_HEADS = 4
QK_NOPE = 128
ROPE_DIM = 64
V_DIM = 128
Q_RANK = 384
KV_RANK = 256
MLA_W = N_HEADS * V_DIM
HEAD_PAD = 256
QK_W = N_HEADS * HEAD_PAD
V_EXT = 2 * V_DIM
VEXT_W = N_HEADS * V_EXT
ROPE_BASE = 10000.0
N_EXPERTS = 8
ALPHA = (2 * DEPTH) ** 0.25
EPS = 1e-6
ATTN_SCALE = math.log2(math.e) / math.sqrt(QK_NOPE + ROPE_DIM)
MOD_ROWS = 16
LANES = 128
V7X_VMEM_BYTES = 64 * 1024 * 1024
VMEM_LIMIT = V7X_VMEM_BYTES * 7 // 8

SH_A, SC_A, G_A, SH_F, SC_F, G_F = range(6)


def _sigmoid(x):
    return 1.0 / (1.0 + jnp.exp(-x))


def _gelu_tanh(x):
    return 0.5 * x * (1.0 + jnp.tanh(math.sqrt(2.0 / math.pi) * (x + 0.044715 * (x * x * x))))


def _layernorm(y, g, b):
    mu = jnp.mean(y, axis=-1, keepdims=True)
    d = y - mu
    var = jnp.mean(d * d, axis=-1, keepdims=True)
    return d * lax.rsqrt(var + EPS) * g + b


def _rmsnorm(y, g):
    return y * lax.rsqrt(jnp.mean(y * y, axis=-1, keepdims=True) + EPS) * g


def _dot(a, b):
    return jnp.dot(a, b, preferred_element_type=F32)


def _dot_nt(a, b):
    return lax.dot_general(a, b, (((1,), (1,)), ((), ())), preferred_element_type=F32)


def _params(*sem):
    return pltpu.CompilerParams(dimension_semantics=sem, vmem_limit_bytes=VMEM_LIMIT)


def _const_spec(shape):
    nd = len(shape)
    return pl.BlockSpec(shape, lambda *_: (0,) * nd)


class _Stream:
    def __init__(self, batch, seq, mod_row0, per_row_mod, latent, tiles):
        self.tiles = tiles
        self.batch = batch
        self.seq = seq
        self.tokens = batch * seq
        self.mod_row0 = mod_row0
        self.per_row_mod = per_row_mod
        self.latent = latent

    def mod_spec(self, layer, which, tm):
        tiles_per_row = self.seq // tm
        row0, per_row = self.mod_row0, self.per_row_mod

        def index(i):
            r = row0 + (i // tiles_per_row if per_row else 0)
            return (layer, r, which, 0, 0)

        return pl.BlockSpec((None, None, None, 1, D_MODEL), index)


def _mod_body(c_ref, w_ref, b_ref, o_ref):
    a = c_ref[...]
    a = (a * _sigmoid(a)).astype(BF16)
    o_ref[:, 0, :] = _dot(a, w_ref[...].astype(BF16)) + b_ref[...]


def _modulation(cond_rows, w_mod, b_mod):
    depth, _, width = w_mod.shape
    n_roles = width // D_MODEL
    return pl.pallas_call(
        _mod_body,
        grid=(depth, n_roles),
        in_specs=[
            pl.BlockSpec((MOD_ROWS, D_MODEL), lambda l, j: (0, 0)),
            pl.BlockSpec((None, D_MODEL, D_MODEL), lambda l, j: (l, 0, j)),
            pl.BlockSpec((None, 1, D_MODEL), lambda l, j: (l, 0, j)),
        ],
        out_specs=pl.BlockSpec((None, MOD_ROWS, None, 1, D_MODEL), lambda l, j: (l, 0, j, 0, 0)),
        out_shape=jax.ShapeDtypeStruct((depth, MOD_ROWS, n_roles, 1, D_MODEL), F32),
        compiler_params=_params("parallel", "parallel"),
        name="modulation",
    )(cond_rows, w_mod, b_mod.reshape(depth, 1, width))


def _store_keys_values(kv, kr, kcat_ref, vext_ref):
    ones = jnp.ones((kv.shape[0], V_EXT - V_DIM), BF16)
    for hd in range(N_HEADS):
        a = hd * HEAD_PAD
        kcat_ref[:, a:a + QK_NOPE] = kv[:, hd * QK_NOPE:(hd + 1) * QK_NOPE].astype(BF16)
        kcat_ref[:, a + QK_NOPE:a + HEAD_PAD] = kr
        b = hd * V_EXT
        vext_ref[:, b:b + V_DIM] = kv[:, (N_HEADS + hd) * V_DIM:(N_HEADS + hd + 1) * V_DIM].astype(BF16)
        vext_ref[:, b + V_DIM:b + V_EXT] = ones


def _mixer_in_body(latent, tm, n_aliased, *refs):
    it = iter(refs)
    x_ref, sc_ref, sh_ref, win_ref, qg_ref, kvg_ref, wq_ref = (next(it) for _ in range(7))
    wkv_ref, lng_ref, ws_ref, bs_ref = (next(it) for _ in range(4))
    cos_ref, sin_ref = (next(it), next(it)) if latent else (None, None)
    for _ in range(n_aliased):
        next(it)
    co_ref, q_ref, kcat_ref, vext_ref = (next(it) for _ in range(4))
    ckv_ref, kr_ref = (None, None) if latent else (next(it), next(it))

    h = (x_ref[...] * (1.0 + sc_ref[...]) + sh_ref[...]).astype(BF16)
    p = _dot(h, win_ref[...])

    for g in range(N_GROUPS):
        cols = slice(g * GROUP_W, (g + 1) * GROUP_W)
        vg = _gelu_tanh(p[:, CHUNK_W + g * GROUP_W:CHUNK_W + (g + 1) * GROUP_W])
        mu = jnp.mean(vg, axis=-1, keepdims=True)
        d = vg - mu
        var = jnp.mean(d * d, axis=-1, keepdims=True)
        vn = (d * lax.rsqrt(var + EPS) * lng_ref[:, cols]).astype(BF16)
        ug = _gelu_tanh(p[:, cols])
        for c in range(tm // CHUNK):
            rows = slice(c * CHUNK, (c + 1) * CHUNK)
            z = _dot(ws_ref[g], vn[rows]) + bs_ref[g]
            co_ref[rows, cols] = (ug[rows] * z).astype(BF16)

    o = 2 * CHUNK_W
    cqn = _rmsnorm(p[:, o:o + Q_RANK], qg_ref[...]).astype(BF16)
    qc = _dot(cqn, wq_ref[...])
    q_rope = qc[:, N_HEADS * QK_NOPE:N_HEADS * (QK_NOPE + ROPE_DIM)]
    if latent:
        cos = cos_ref[...]
        sin = sin_ref[...]
        q_rope = q_rope * cos + qc[:, N_HEADS * (QK_NOPE + ROPE_DIM):] * sin
    for hd in range(N_HEADS):
        a = hd * HEAD_PAD
        q_ref[:, a:a + QK_NOPE] = (qc[:, hd * QK_NOPE:(hd + 1) * QK_NOPE] * ATTN_SCALE).astype(BF16)
        pair = q_rope[:, (hd // 2) * LANES:(hd // 2 + 1) * LANES]
        if hd % 2:
            pair = pltpu.roll(pair, ROPE_DIM, axis=1)
        q_ref[:, a + QK_NOPE:a + HEAD_PAD] = (pair * ATTN_SCALE).astype(BF16)

    o += Q_RANK
    ckvn = _rmsnorm(p[:, o:o + KV_RANK], kvg_ref[...])
    o += KV_RANK
    kr = p[:, o:o + LANES]
    if not latent:
        seq = ckv_ref.shape[1]
        for b in range(ckv_ref.shape[0]):
            ckv_ref[b] = ckvn[b * seq:(b + 1) * seq]
            kr_ref[b] = kr[b * seq:(b + 1) * seq, :ROPE_DIM]
    else:
        kr = kr * cos[:, :LANES] + p[:, o + LANES:o + 2 * LANES] * sin[:, :LANES]
    kr = kr.astype(BF16)
    kv = _dot(ckvn.astype(BF16), wkv_ref[...])
    _store_keys_values(kv, kr, kcat_ref, vext_ref)


def _mixer_in(stream, layer, x, mod, lw, rope, tm, caches=None):
    latent = stream.latent
    t = stream.tokens
    row = lambda w: pl.BlockSpec((tm, w), lambda i: (i, 0))
    win = lw["w_in"]
    win_cols = win.shape[1] if latent else win.shape[1] - LANES
    wq = lw["wq"]
    wq_cols = wq.shape[1] if latent else wq.shape[1] - N_HEADS * ROPE_DIM
    ins = [x, mod, mod, win, lw["q_g"], lw["kv_g"], wq]
    specs = [row(D_MODEL), stream.mod_spec(layer, SC_A, tm), stream.mod_spec(layer, SH_A, tm),
             _const_spec((D_MODEL, win_cols)), _const_spec(lw["q_g"].shape), _const_spec(lw["kv_g"].shape),
             _const_spec((Q_RANK, wq_cols))]
    for name in ("wkv", "ln_v_g", "w_s", "b_s"):
        ins.append(lw[name])
        specs.append(_const_spec(lw[name].shape))
    if latent:
        tiles_per_seq = stream.seq // tm
        pos = pl.BlockSpec((tm, N_HEADS * ROPE_DIM), lambda i: (i % tiles_per_seq, 0))
        ins += [rope[0], rope[1]]
        specs += [pos, pos]
    out_shape = [jax.ShapeDtypeStruct((t, CHUNK_W), BF16), jax.ShapeDtypeStruct((t, QK_W), BF16),
                 jax.ShapeDtypeStruct((t, QK_W), BF16), jax.ShapeDtypeStruct((t, VEXT_W), BF16)]
    out_specs = [row(CHUNK_W), row(QK_W), row(QK_W), row(VEXT_W)]
    aliases = {}
    if not latent:
        for rank in (KV_RANK, ROPE_DIM):
            out_shape.append(jax.ShapeDtypeStruct((stream.batch, DEPTH, stream.seq, rank), F32))
            out_specs.append(pl.BlockSpec((tm // stream.seq, None, stream.seq, rank),
                                          lambda i: (i, layer, 0, 0)))
        for k, buf in enumerate(caches or ()):
            aliases[len(ins)] = len(out_shape) - 2 + k
            ins.append(buf)
            specs.append(pl.BlockSpec(memory_space=pl.ANY))
    return pl.pallas_call(
        functools.partial(_mixer_in_body, latent, tm, len(aliases)),
        grid=(t // tm,),
        in_specs=specs,
        out_specs=out_specs,
        out_shape=out_shape,
        input_output_aliases=aliases,
        compiler_params=_params("parallel"),
        name="mixer_in",
    )(*ins)


KEY_CHUNK = 256


def _attend_body(has_ctx, *refs):
    if has_ctx:
        q_ref, k_ref, vext_ref, cckv_ref, ckr_ref, wkv_ref, o_ref, kctx_ref, vctx_ref = refs

        @pl.when(pl.program_id(1) == 0)
        def _():
            kv = _dot(cckv_ref[...].astype(BF16), wkv_ref[...])
            _store_keys_values(kv, ckr_ref[...].astype(BF16), kctx_ref, vctx_ref)

        sources = [(kctx_ref, vctx_ref), (k_ref, vext_ref)]
    else:
        q_ref, k_ref, vext_ref, o_ref = refs
        sources = [(k_ref, vext_ref)]
    chunks = [(kr, vr, slice(c * KEY_CHUNK, (c + 1) * KEY_CHUNK))
              for kr, vr in sources for c in range(kr.shape[0] // KEY_CHUNK)]

    def scores(hd):
        qk = slice(hd * HEAD_PAD, (hd + 1) * HEAD_PAD)
        return [_dot_nt(q_ref[:, qk], kr[rows, qk]) for kr, _, rows in chunks]

    s = scores(0)
    for hd in range(N_HEADS):
        m = jnp.max(functools.reduce(jnp.maximum, s), axis=-1, keepdims=True)
        s_next, acc = [], None
        for c, (kr, vr, rows) in enumerate(chunks):
            if hd + 1 < N_HEADS:
                qk = slice((hd + 1) * HEAD_PAD, (hd + 2) * HEAD_PAD)
                s_next.append(_dot_nt(q_ref[:, qk], kr[rows, qk]))
            p = jnp.exp2(s[c] - m).astype(BF16)
            part = _dot(p, vr[rows, hd * V_EXT:(hd + 1) * V_EXT])
            acc = part if acc is None else acc + part
        s = s_next
        o_ref[:, hd * V_DIM:(hd + 1) * V_DIM] = (acc[:, :V_DIM] / acc[:, V_DIM:]).astype(BF16)


def _attend(stream, layer, q, kcat, vext, lw, ctx, tq):
    n = stream.seq
    nq = n // tq
    has_ctx = stream.latent
    ins = [q, kcat, vext]
    specs = [pl.BlockSpec((tq, QK_W), lambda b, i: (b * nq + i, 0)),
             pl.BlockSpec((n, QK_W), lambda b, i: (b, 0)),
             pl.BlockSpec((n, VEXT_W), lambda b, i: (b, 0))]
    scratch = []
    if has_ctx:
        cache_ckv, cache_kr = ctx
        past = cache_ckv.shape[2]
        ins += [cache_ckv, cache_kr, lw["wkv"]]
        specs += [pl.BlockSpec((None, None, past, KV_RANK), lambda b, i: (b, layer, 0, 0)),
                  pl.BlockSpec((None, None, past, LANES), lambda b, i: (b, layer, 0, 0)),
                  _const_spec(lw["wkv"].shape)]
        scratch = [pltpu.VMEM((past, QK_W), BF16), pltpu.VMEM((past, VEXT_W), BF16)]
    return pl.pallas_call(
        functools.partial(_attend_body, has_ctx),
        grid=(stream.batch, nq),
        in_specs=specs,
        out_specs=pl.BlockSpec((tq, MLA_W), lambda b, i: (b * nq + i, 0)),
        out_shape=jax.ShapeDtypeStruct((stream.tokens, MLA_W), BF16),
        scratch_shapes=scratch,
        compiler_params=_params("parallel", "arbitrary"),
        name="attend",
    )(*ins)


MIX_PARTS = 4


def _mixer_residual(co_ref, mo_ref, wo1_ref, wo2_ref, x_ref, ga_ref, g_ref, b_ref):
    rp = x_ref.shape[0] // MIX_PARTS
    parts = [slice(k * rp, (k + 1) * rp) for k in range(MIX_PARTS)]
    mix = [_dot(co_ref[r], wo1_ref[...]) + _dot(mo_ref[r], wo2_ref[...]) for r in parts]
    x1 = [_layernorm(ALPHA * x_ref[r] + ga_ref[...] * mix[k], g_ref[...], b_ref[...])
          for k, r in enumerate(parts)]
    return parts, x1


def _mix_out_body(co_ref, mo_ref, wo1_ref, wo2_ref, x_ref, ga_ref, scf_ref, shf_ref, g_ref, b_ref,
                  rcat_ref, x1_ref, h2_ref, meta_ref, cnt_ref):
    parts, x1 = _mixer_residual(co_ref, mo_ref, wo1_ref, wo2_ref, x_ref, ga_ref, g_ref, b_ref)
    rp = x_ref.shape[0] // MIX_PARTS
    for k, r in enumerate(parts):
        x1_ref[r] = x1[k]
    h2 = [v * (1.0 + scf_ref[...]) + shf_ref[...] for v in x1]
    hh = [v.astype(BF16) for v in h2]
    for k, r in enumerate(parts):
        h2_ref[r] = hh[k]
    hl = [(h2[k] - hh[k].astype(F32)).astype(BF16) for k in range(MIX_PARTS)]
    ra = [_dot(v, rcat_ref[...]) for v in hh]
    rb = [_dot(v, rcat_ref[...]) for v in hl]
    logits = [ra[k][:, :LANES] + (ra[k][:, LANES:] + (rb[k][:, :LANES] + rb[k][:, LANES:]))
              for k in range(MIX_PARTS)]
    lane = lax.broadcasted_iota(jnp.int32, (rp, LANES), 1).astype(F32)
    neg = -jnp.inf
    lg = [jnp.where(lane < N_EXPERTS, v, neg) for v in logits]
    m1 = [jnp.max(v, axis=-1, keepdims=True) for v in lg]
    i1 = [jnp.min(jnp.where(lg[k] == m1[k], lane, float(LANES)), axis=-1, keepdims=True)
          for k in range(MIX_PARTS)]
    lg2 = [jnp.where(lane == i1[k], neg, lg[k]) for k in range(MIX_PARTS)]
    m2 = [jnp.max(v, axis=-1, keepdims=True) for v in lg2]
    i2 = [jnp.min(jnp.where(lg2[k] == m2[k], lane, float(LANES)), axis=-1, keepdims=True)
          for k in range(MIX_PARTS)]
    picked = [jnp.where(lane == i1[k], 1.0, jnp.where(lane == i2[k], 1.0, 0.0)) for k in range(MIX_PARTS)]
    earlier = jnp.where(lax.broadcasted_iota(jnp.int32, (rp, rp), 0)
                        > lax.broadcasted_iota(jnp.int32, (rp, rp), 1), 1.0, 0.0).astype(BF16)
    before = jnp.zeros((1, LANES), F32)
    for k, r in enumerate(parts):
        rank = _dot(earlier, picked[k].astype(BF16)) + before
        before = before + jnp.sum(picked[k], axis=0, keepdims=True)
        rank1 = jnp.sum(jnp.where(lane == i1[k], rank, 0.0), axis=-1, keepdims=True)
        rank2 = jnp.sum(jnp.where(lane == i2[k], rank, 0.0), axis=-1, keepdims=True)
        e2 = jnp.exp(m2[k] - m1[k])
        den = 1.0 + e2
        meta = jnp.zeros((rp, LANES), F32)
        for j, val in enumerate((i1[k], i2[k], 1.0 / den, e2 / den, rank1, rank2)):
            meta = jnp.where(lane == float(j), val, meta)
        meta_ref[r] = meta
    cnt_ref[...] = before


def _mix_out(stream, layer, co, mo, x, mod, lw, tm):
    t = stream.tokens
    row = lambda w: pl.BlockSpec((tm, w), lambda i: (i, 0))
    vec = _const_spec((1, D_MODEL))
    return pl.pallas_call(
        _mix_out_body,
        grid=(t // tm,),
        in_specs=[row(CHUNK_W), row(MLA_W), _const_spec(lw["wo_chunk"].shape), _const_spec(lw["wo_mla"].shape),
                  row(D_MODEL), stream.mod_spec(layer, G_A, tm), stream.mod_spec(layer, SC_F, tm),
                  stream.mod_spec(layer, SH_F, tm), vec, vec, _const_spec(lw["router"].shape)],
        out_specs=[row(D_MODEL), row(D_MODEL), row(LANES), pl.BlockSpec((None, 1, LANES), lambda i: (i, 0, 0))],
        out_shape=[jax.ShapeDtypeStruct((t, D_MODEL), F32), jax.ShapeDtypeStruct((t, D_MODEL), BF16),
                   jax.ShapeDtypeStruct((t, LANES), F32), jax.ShapeDtypeStruct((t // tm, 1, LANES), F32)],
        compiler_params=_params("parallel"),
        name="mix_out",
    )(co, mo, lw["wo_chunk"], lw["wo_mla"], x, mod, mod, mod, lw["ln_mix_g"], lw["ln_mix_b"], lw["router"])


def _mix_ffn_body(co_ref, mo_ref, wo1_ref, wo2_ref, x_ref, ga_ref, scf_ref, shf_ref, gf_ref,
                  gm_ref, bm_ref, g_ref, b_ref, wg_ref, wu_ref, wd_ref, o_ref):
    _, x1 = _mixer_residual(co_ref, mo_ref, wo1_ref, wo2_ref, x_ref, ga_ref, gm_ref, bm_ref)
    x1 = jnp.concatenate(x1, axis=0)
    h = (x1 * (1.0 + scf_ref[...]) + shf_ref[...]).astype(BF16)
    gate = _dot(h, wg_ref[...])
    up = _dot(h, wu_ref[...])
    a = (gate * _sigmoid(gate) * up).astype(BF16)
    f = _dot(a, wd_ref[...])
    o_ref[...] = _layernorm(ALPHA * x1 + gf_ref[...] * f, g_ref[...], b_ref[...])


def _mix_ffn(stream, layer, co, mo, x, mod, lw, tm):
    t = stream.tokens
    row = lambda w: pl.BlockSpec((tm, w), lambda i: (i, 0))
    vec = _const_spec((1, D_MODEL))
    resident = lambda a: pl.BlockSpec(a.shape, lambda i: (0, 0), pipeline_mode=pl.Buffered(1))
    mods = [stream.mod_spec(layer, which, tm) for which in (G_A, SC_F, SH_F, G_F)]
    return pl.pallas_call(
        _mix_ffn_body,
        grid=(t // tm,),
        in_specs=[row(CHUNK_W), row(MLA_W), resident(lw["wo_chunk"]), resident(lw["wo_mla"]), row(D_MODEL),
                  *mods, vec, vec, vec, vec,
                  resident(lw["ffn_gate"]), resident(lw["ffn_up"]), resident(lw["ffn_down"])],
        out_specs=row(D_MODEL),
        out_shape=jax.ShapeDtypeStruct((t, D_MODEL), F32),
        compiler_params=_params("parallel"),
        name="mix_ffn",
    )(co, mo, lw["wo_chunk"], lw["wo_mla"], x, mod, mod, mod, mod,
      lw["ln_mix_g"], lw["ln_mix_b"], lw["ln_ffn_g"], lw["ln_ffn_b"],
      lw["ffn_gate"], lw["ffn_up"], lw["ffn_down"])


RUN_ALIGN = 16
ROUTE_TILE = 512
EXPERT_TILE = 512


def _plan_body(cap, tile, cnt_ref, off_ref, run_ref, base_ref, te_ref, tb_ref, rows_ref, nv_ref):
    nw = cnt_ref.shape[0]
    lane = lax.broadcasted_iota(jnp.int32, (1, LANES), 1).astype(F32)
    run = jnp.floor((cnt_ref[...] + (RUN_ALIGN - 1)) * (1.0 / RUN_ALIGN)) * RUN_ALIGN
    before = jnp.where(lax.broadcasted_iota(jnp.int32, (LANES, LANES), 0)
                       < lax.broadcasted_iota(jnp.int32, (LANES, LANES), 1), 1.0, 0.0).astype(BF16)
    upto = jnp.where(lax.broadcasted_iota(jnp.int32, (LANES, LANES), 0)
                     <= lax.broadcasted_iota(jnp.int32, (LANES, LANES), 1), 1.0, 0.0).astype(BF16)
    above = jnp.where(lax.broadcasted_iota(jnp.int32, (nw, nw), 0)
                      > lax.broadcasted_iota(jnp.int32, (nw, nw), 1), 1.0, 0.0).astype(BF16)
    run_b = run.astype(BF16)
    off_ref[...] = _dot(run_b, before).astype(jnp.int32)
    run_ref[...] = run.astype(jnp.int32)
    base_ref[...] = (lane * cap + _dot(above, run_b)).astype(jnp.int32)
    totals = jnp.sum(run, axis=0, keepdims=True)
    tiles = jnp.floor((totals + (tile - 1)) * (1.0 / tile))
    cum = _dot(tiles.astype(BF16), upto)
    start = cum - tiles
    n_valid = jnp.max(cum, axis=-1, keepdims=True)
    step = jnp.minimum(lane, n_valid - 1.0)
    pick = lambda v, e: jnp.sum(jnp.where(lane == float(e), v, 0.0), axis=-1, keepdims=True)
    te = jnp.zeros_like(step)
    for e in range(N_EXPERTS):
        te = te + jnp.where(pick(cum, e) <= step, 1.0, 0.0)
    start_te, total_te = jnp.zeros_like(step), jnp.zeros_like(step)
    for e in range(N_EXPERTS):
        start_te = jnp.where(te == float(e), pick(start, e), start_te)
        total_te = jnp.where(te == float(e), pick(totals, e), total_te)
    tj = step - start_te
    te_ref[...] = te.astype(jnp.int32)
    tb_ref[...] = (te * (cap // tile) + tj).astype(jnp.int32)
    rows_ref[...] = jnp.minimum(float(tile), total_te - tj * tile).astype(jnp.int32)
    nv_ref[...] = jnp.broadcast_to(n_valid, (1, LANES)).astype(jnp.int32)


def _route_plan(counts, tokens, tm, tile):
    nw = tokens // tm
    cnt = jnp.concatenate([c.reshape(-1, LANES) for c in counts], axis=0)
    worst_pad = (RUN_ALIGN - 1) * nw
    cap = pl.cdiv(tokens + worst_pad, tile) * tile
    n_tiles = (2 * tokens + N_EXPERTS * worst_pad) // tile + N_EXPERTS
    assert n_tiles <= LANES and tm % RUN_ALIGN == 0
    per_run = jax.ShapeDtypeStruct((nw, LANES), jnp.int32)
    per_step = jax.ShapeDtypeStruct((1, LANES), jnp.int32)
    off, run, base, te, tb, rows, nv = pl.pallas_call(
        functools.partial(_plan_body, cap, tile),
        out_shape=[per_run, per_run, per_run, per_step, per_step, per_step, per_step],
        name="moe_plan",
    )(cnt)
    return dict(
        tm=tm, cap=cap, tile=tile, n_tiles=n_tiles, buf_rows=2 * tm + N_EXPERTS * RUN_ALIGN,
        off=off.reshape(-1), run=run.reshape(-1), base=base.reshape(-1),
        tile_expert=te.reshape(-1), tile_block=tb.reshape(-1), tile_rows=rows.reshape(-1),
        n_valid=nv.reshape(-1))


def _run_piece_sizes(tm):
    sizes, b = [], RUN_ALIGN
    while b <= tm:
        sizes.append(b)
        b *= 2
    return sizes


def _row_run_copies(src, dst, src_off, dst_off, n_rows, sem, sizes, action):
    for b in sizes:
        done = n_rows & (-2 * b)
        cp = pltpu.make_async_copy(
            src.at[pl.ds(pl.multiple_of(src_off + done, RUN_ALIGN), b)],
            dst.at[pl.ds(pl.multiple_of(dst_off + done, RUN_ALIGN), b)], sem)

        @pl.when((n_rows & b) != 0)
        def _():
            action(cp)


def _local_rows(meta, off_ref, w):
    i1, i2 = meta[:, 0:1], meta[:, 1:2]
    o1, o2 = jnp.zeros_like(i1), jnp.zeros_like(i2)
    for e in range(N_EXPERTS):
        off_e = off_ref[w * LANES + e].astype(F32)
        o1 = jnp.where(i1 == float(e), off_e, o1)
        o2 = jnp.where(i2 == float(e), off_e, o2)
    return (o1 + meta[:, 4:5]).astype(jnp.int32), (o2 + meta[:, 5:6]).astype(jnp.int32)


def _start(cp):
    cp.start()


def _wait(cp):
    cp.wait()


def _dispatch_body(tm, nw, tile0, off_ref, run_ref, base_ref, h_ref, meta_ref, *refs):
    xs_ref, xw_ref, sems = refs[-3:]
    w = pl.program_id(0)
    slot = w % 2
    sizes = _run_piece_sizes(tm)

    def push(win, buf, action):
        for e in range(N_EXPERTS):
            k = (tile0 + win) * LANES + e
            _row_run_copies(xw_ref.at[buf], xs_ref, off_ref[k], base_ref[k], run_ref[k],
                            sems.at[buf], sizes, action)

    meta_t = meta_ref[...].T
    o1, o2 = jnp.zeros((1, tm), F32), jnp.zeros((1, tm), F32)
    for e in range(N_EXPERTS):
        off_e = off_ref[(tile0 + w) * LANES + e].astype(F32)
        o1 = jnp.where(meta_t[0:1] == float(e), off_e, o1)
        o2 = jnp.where(meta_t[1:2] == float(e), off_e, o2)
    r1 = (o1 + meta_t[4:5]).astype(jnp.int32)
    r2 = (o2 + meta_t[5:6]).astype(jnp.int32)
    row = lax.broadcasted_iota(jnp.int32, (xw_ref.shape[1], tm), 0)
    sel = jnp.where(row == r1, 1.0, jnp.where(row == r2, 1.0, 0.0)).astype(BF16)
    xw = _dot(sel, h_ref[...])
    xw_ref[slot] = xw.astype(BF16)
    push(w, slot, _start)

    @pl.when(w > 0)
    def _():
        push(w - 1, 1 - slot, _wait)

    @pl.when(w == nw - 1)
    def _():
        push(w, slot, _wait)


def _dispatch(h2, meta, plan, tile0, xs=None):
    tm = plan["tm"]
    nw = h2.shape[0] // tm
    ins = [plan["off"], plan["run"], plan["base"], h2, meta]
    specs = [pl.BlockSpec((tm, D_MODEL), lambda i, *_: (i, 0)),
             pl.BlockSpec((tm, LANES), lambda i, *_: (i, 0))]
    aliases = {}
    if xs is not None:
        aliases = {len(ins): 0}
        ins.append(xs)
        specs.append(pl.BlockSpec(memory_space=pl.ANY))
    return pl.pallas_call(
        functools.partial(_dispatch_body, tm, nw, tile0),
        grid_spec=pltpu.PrefetchScalarGridSpec(
            num_scalar_prefetch=3,
            grid=(nw,),
            in_specs=specs,
            out_specs=pl.BlockSpec(memory_space=pl.ANY),
            scratch_shapes=[pltpu.VMEM((2, plan["buf_rows"], D_MODEL), BF16),
                            pltpu.SemaphoreType.DMA((2,))]),
        out_shape=jax.ShapeDtypeStruct((N_EXPERTS * plan["cap"], D_MODEL), BF16),
        input_output_aliases=aliases,
        compiler_params=_params("arbitrary"),
        name="moe_dispatch",
    )(*ins)


def _experts_body(te_ref, tb_ref, rows_ref, nv_ref, x_ref, wg_ref, wu_ref, wd_ref, o_ref):
    i = pl.program_id(0)
    rows = rows_ref[i]
    valid = i < nv_ref[0]
    half = x_ref.shape[0] // 2

    def swiglu(n):
        x = x_ref[0:n, :]
        live = lax.broadcasted_iota(jnp.int32, (n, 1), 0) < rows
        x = jnp.where(live, x, jnp.zeros_like(x))
        gate = _dot(x, wg_ref[...])
        up = _dot(x, wu_ref[...])
        a = (gate * _sigmoid(gate) * up).astype(BF16)
        o_ref[0:n, :] = _dot(a, wd_ref[...]).astype(BF16)

    @pl.when(jnp.logical_and(valid, rows > half))
    def _():
        swiglu(2 * half)

    @pl.when(jnp.logical_and(valid, rows <= half))
    def _():
        swiglu(half)


def _experts(xs, plan, lw):
    dff = lw["moe_gate"].shape[2]
    rows = pl.BlockSpec((plan["tile"], D_MODEL), lambda i, te, tb, *_: (tb[i], 0))
    return pl.pallas_call(
        _experts_body,
        grid_spec=pltpu.PrefetchScalarGridSpec(
            num_scalar_prefetch=4,
            grid=(plan["n_tiles"],),
            in_specs=[rows,
                      pl.BlockSpec((None, D_MODEL, dff), lambda i, te, *_: (te[i], 0, 0)),
                      pl.BlockSpec((None, D_MODEL, dff), lambda i, te, *_: (te[i], 0, 0)),
                      pl.BlockSpec((None, dff, D_MODEL), lambda i, te, *_: (te[i], 0, 0))],
            out_specs=rows),
        out_shape=jax.ShapeDtypeStruct(xs.shape, BF16),
        compiler_params=_params("arbitrary"),
        name="moe_experts",
    )(plan["tile_expert"], plan["tile_block"], plan["tile_rows"], plan["n_valid"],
      xs, lw["moe_gate"], lw["moe_up"], lw["moe_down"])


def _combine_body(tm, nw, tile0, off_ref, run_ref, base_ref, ys_ref, meta_ref, x_ref, gf_ref, g_ref,
                  b_ref, o_ref, yw_ref, sems):
    w = pl.program_id(0)
    slot = w % 2
    sizes = _run_piece_sizes(tm)

    def fetch(win, buf, action):
        for e in range(N_EXPERTS):
            k = (tile0 + win) * LANES + e
            _row_run_copies(ys_ref, yw_ref.at[buf], base_ref[k], off_ref[k], run_ref[k],
                            sems.at[buf], sizes, action)

    @pl.when(w == 0)
    def _():
        yw_ref[...] = jnp.zeros_like(yw_ref)
        fetch(0, 0, _start)

    @pl.when(w + 1 < nw)
    def _():
        fetch(w + 1, 1 - slot, _start)

    fetch(w, slot, _wait)
    rp = tm // MIX_PARTS
    parts = [slice(k * rp, (k + 1) * rp) for k in range(MIX_PARTS)]
    meta = [meta_ref[r] for r in parts]
    picks = [_local_rows(v, off_ref, tile0 + w) for v in meta]
    col = lax.broadcasted_iota(jnp.int32, (rp, yw_ref.shape[1]), 1)
    yw = yw_ref[slot]
    gate = [jnp.where(col == r1, v[:, 2:3], jnp.where(col == r2, v[:, 3:4], 0.0)).astype(BF16)
            for v, (r1, r2) in zip(meta, picks)]
    f = [_dot(g, yw) for g in gate]
    for k, r in enumerate(parts):
        o_ref[r] = _layernorm(ALPHA * x_ref[r] + gf_ref[...] * f[k], g_ref[...], b_ref[...])


def _combine(stream, layer, ys, meta, x1, mod, plan, lw, tile0):
    tm = plan["tm"]
    nw = stream.tokens // tm
    row = lambda width: pl.BlockSpec((tm, width), lambda i, *_: (i, 0))
    vec = pl.BlockSpec((1, D_MODEL), lambda i, *_: (0, 0))
    mod_spec = stream.mod_spec(layer, G_F, tm)
    mod_spec = pl.BlockSpec(mod_spec.block_shape, lambda i, *_, f=mod_spec.index_map: f(i))
    return pl.pallas_call(
        functools.partial(_combine_body, tm, nw, tile0),
        grid_spec=pltpu.PrefetchScalarGridSpec(
            num_scalar_prefetch=3,
            grid=(nw,),
            in_specs=[pl.BlockSpec(memory_space=pl.ANY), row(LANES), row(D_MODEL), mod_spec, vec, vec],
            out_specs=row(D_MODEL),
            scratch_shapes=[pltpu.VMEM((2, plan["buf_rows"], D_MODEL), BF16),
                            pltpu.SemaphoreType.DMA((2,))]),
        out_shape=jax.ShapeDtypeStruct((stream.tokens, D_MODEL), F32),
        compiler_params=_params("arbitrary"),
        name="moe_combine",
    )(plan["off"], plan["run"], plan["base"], ys, meta, x1, mod, lw["ln_ffn_g"], lw["ln_ffn_b"])


def _rope_tables(n_tokens):
    rows = n_tokens // GRID_W
    row = np.repeat(np.arange(rows, dtype=np.float64), GRID_W)
    col = np.tile(np.arange(GRID_W, dtype=np.float64), rows)
    half = ROPE_DIM // 2
    inv_freq = ROPE_BASE ** (-np.arange(0, half, 2, dtype=np.float64) / half)
    ar = row[:, None] * inv_freq[None, :]
    ac = col[:, None] * inv_freq[None, :]
    cos = np.concatenate([np.cos(ar), np.cos(ar), np.cos(ac), np.cos(ac)], axis=-1)
    sin = np.concatenate([-np.sin(ar), np.sin(ar), -np.sin(ac), np.sin(ac)], axis=-1)
    return jnp.asarray(np.tile(cos, (1, N_HEADS)), F32), jnp.asarray(np.tile(sin, (1, N_HEADS)), F32)


def _swap_rope_halves(a):
    q = ROPE_DIM // 4
    return jnp.concatenate([a[..., q:2 * q], a[..., :q], a[..., 3 * q:], a[..., 2 * q:3 * q]], axis=-1)


def _layer_weights(l, p):
    w_in = p["w_in"][l].astype(BF16)
    o_kr = 2 * CHUNK_W + Q_RANK + KV_RANK
    zpad = jnp.zeros((D_MODEL, LANES - ROPE_DIM), BF16)
    w_in_ext = jnp.concatenate(
        [w_in, zpad, _swap_rope_halves(w_in[:, o_kr:o_kr + ROPE_DIM]), zpad], axis=1)

    w_uq = p["w_uq"][l].astype(BF16).reshape(Q_RANK, N_HEADS, QK_NOPE + ROPE_DIM)
    wq_rope = w_uq[:, :, QK_NOPE:]
    wq = jnp.concatenate([w_uq[:, :, :QK_NOPE].reshape(Q_RANK, N_HEADS * QK_NOPE),
                          wq_rope.reshape(Q_RANK, N_HEADS * ROPE_DIM),
                          _swap_rope_halves(wq_rope).reshape(Q_RANK, N_HEADS * ROPE_DIM)], axis=1)

    w_ukv = p["w_ukv"][l].astype(BF16).reshape(KV_RANK, N_HEADS, QK_NOPE + V_DIM)
    wkv = jnp.concatenate([w_ukv[:, :, :QK_NOPE].reshape(KV_RANK, N_HEADS * QK_NOPE),
                           w_ukv[:, :, QK_NOPE:].reshape(KV_RANK, MLA_W)], axis=1)
    w_out = p["w_out"][l]
    lw = {
        "w_in": w_in_ext,
        "q_g": p["q_norm_g"][l].reshape(1, Q_RANK),
        "kv_g": p["kv_norm_g"][l].reshape(1, KV_RANK),
        "wq": wq,
        "wkv": wkv,
        "ln_v_g": p["chunk_ln_g"][l].reshape(1, CHUNK_W),
        "w_s": p["w_spatial"][l].astype(BF16),
        "b_s": jnp.broadcast_to(p["b_spatial"][l][:, :, None], (N_GROUPS, CHUNK, GROUP_W)),
        "wo_chunk": w_out[:CHUNK_W].astype(BF16),
        "wo_mla": w_out[CHUNK_W:].astype(BF16),
        "ln_mix_g": p["ln_mix_g"][l].reshape(1, D_MODEL),
        "ln_mix_b": p["ln_mix_b"][l].reshape(1, D_MODEL),
        "ln_ffn_g": p["ln_ffn_g"][l].reshape(1, D_MODEL),
        "ln_ffn_b": p["ln_ffn_b"][l].reshape(1, D_MODEL),
    }
    if l % 2 == 0:
        lw["ffn_gate"] = p["ffn_w_gate"][l // 2].astype(BF16)
        lw["ffn_up"] = p["ffn_w_up"][l // 2].astype(BF16)
        lw["ffn_down"] = p["ffn_w_down"][l // 2].astype(BF16)
    else:
        r = jnp.pad(p["router_w"][l // 2], ((0, 0), (0, LANES - N_EXPERTS)))
        r_hi = r.astype(BF16)
        lw["router"] = jnp.concatenate([r_hi, (r - r_hi.astype(F32)).astype(BF16)], axis=1)
        lw["moe_gate"] = p["moe_w_gate"][l // 2].astype(BF16)
        lw["moe_up"] = p["moe_w_up"][l // 2].astype(BF16)
        lw["moe_down"] = p["moe_w_down"][l // 2].astype(BF16)
    return lw


def _trunk(streams, xs, mod, weights, rope, ctx):
    caches = None
    for l in range(DEPTH):
        lw = weights[l]
        mixed = []
        for stream, x in zip(streams, xs):
            outs = _mixer_in(stream, l, x, mod, lw, rope, stream.tiles["mixer_in"],
                             None if stream.latent else caches)
            co, q, kcat, vext = outs[:4]
            if not stream.latent:
                caches = outs[4:]
            mixed.append((co, _attend(stream, l, q, kcat, vext, lw, ctx, stream.tiles["attend"])))
        if l % 2 == 0:
            xs = [_mix_ffn(stream, l, co, mo, x, mod, lw, stream.tiles["ffn"])
                  for stream, x, (co, mo) in zip(streams, xs, mixed)]
            continue
        routed = [_mix_out(stream, l, co, mo, x, mod, lw, ROUTE_TILE)
                  for stream, x, (co, mo) in zip(streams, xs, mixed)]
        plan = _route_plan([r[3] for r in routed], sum(s.tokens for s in streams), ROUTE_TILE, EXPERT_TILE)
        tile0, sorted_rows = [], None
        for stream, (_, h2, meta, _) in zip(streams, routed):
            tile0.append(sum(s.tokens for s in streams[:len(tile0)]) // ROUTE_TILE)
            sorted_rows = _dispatch(h2, meta, plan, tile0[-1], sorted_rows)
        ys = _experts(sorted_rows, plan, lw)
        xs = [_combine(stream, l, ys, meta, x1, mod, plan, lw, t0)
              for stream, (x1, _, meta, _), t0 in zip(streams, routed, tile0)]
    return xs, caches


def kernel(x_prompt, x_sample, c, cache_ckv, cache_krope, c_ctx, w_mod, b_mod, w_in, q_norm_g, kv_norm_g, w_uq, w_ukv, chunk_ln_g, w_spatial, b_spatial, w_out, ln_mix_g, ln_mix_b, ln_ffn_g, ln_ffn_b, ffn_w_gate, ffn_w_up, ffn_w_down, router_w, moe_w_gate, moe_w_up, moe_w_down):
    p = dict(w_in=w_in, q_norm_g=q_norm_g, kv_norm_g=kv_norm_g, w_uq=w_uq, w_ukv=w_ukv,
             chunk_ln_g=chunk_ln_g, w_spatial=w_spatial, b_spatial=b_spatial, w_out=w_out,
             ln_mix_g=ln_mix_g, ln_mix_b=ln_mix_b, ln_ffn_g=ln_ffn_g, ln_ffn_b=ln_ffn_b,
             ffn_w_gate=ffn_w_gate, ffn_w_up=ffn_w_up, ffn_w_down=ffn_w_down, router_w=router_w,
             moe_w_gate=moe_w_gate, moe_w_up=moe_w_up, moe_w_down=moe_w_down)
    weights = [_layer_weights(l, p) for l in range(DEPTH)]

    batch, seq, _ = x_prompt.shape
    dec_batch, dec_seq, _ = x_sample.shape
    cond_rows = jnp.concatenate(
        [c_ctx[None, :], c, jnp.zeros((MOD_ROWS - 1 - dec_batch, D_MODEL), F32)], axis=0)
    mod = _modulation(cond_rows, w_mod, b_mod)

    prompt = _Stream(batch, seq, mod_row0=0, per_row_mod=False, latent=False,
                     tiles=dict(mixer_in=512, attend=256, ffn=512))
    sample = _Stream(dec_batch, dec_seq, mod_row0=1, per_row_mod=True, latent=True,
                     tiles=dict(mixer_in=1024, attend=1024, ffn=512))
    rope = _rope_tables(dec_seq)
    ctx = (cache_ckv, jnp.pad(cache_krope, ((0, 0), (0, 0), (0, 0), (0, LANES - ROPE_DIM))))
    (y_prompt, y_sample), (new_ckv, new_krope) = _trunk(
        [prompt, sample],
        [x_prompt.reshape(batch * seq, D_MODEL), x_sample.reshape(dec_batch * dec_seq, D_MODEL)],
        mod, weights, rope, ctx)
    return (y_prompt.reshape(batch, seq, D_MODEL), y_sample.reshape(dec_batch, dec_seq, D_MODEL),
            new_ckv, new_krope)
```
